```python
import math
import jax
import jax.numpy as jnp
from jax import lax
import numpy as np

D_MODEL = 1024
BATCH = 16
SEQ = 4096
DEPTH = 2

GRID_W = 64
CTX_LEN = 256
HEAD_DIM = 64
EPS = 1e-6

F_GROUPS = 4
F_WIDTH = F_GROUPS * HEAD_DIM

ATT_HEADS = 6
ATT_KV_HEADS = 2
ATT_WIDTH = ATT_HEADS * HEAD_DIM
KV_WIDTH = ATT_KV_HEADS * HEAD_DIM
WINDOW = 128
BLOCK = 128
ROPE_THETA = 10000.0

SSD_HEADS = 6
SSD_WIDTH = SSD_HEADS * HEAD_DIM
SSD_GROUPS = 2
SSD_STATE = 128
SSD_CHUNK = 128
CONV_K = 5
CONV_DIM = SSD_WIDTH + 2 * SSD_GROUPS * SSD_STATE

MIX_WIDTH = F_WIDTH + ATT_WIDTH + SSD_WIDTH
IN_SPLITS = (F_WIDTH, ATT_WIDTH, KV_WIDTH, KV_WIDTH, SSD_WIDTH, CONV_DIM, 2 * SSD_HEADS)
IN_WIDTH = sum(IN_SPLITS)

N_EXPERTS = 16
EXPERT_FF = 2816
EC_CAPACITY = 2

kernel_name = 'hybrid_diffusion_fourier_swa_ssd_ecmoe'


def rmsnorm(x, g):
    xf = x.astype(jnp.float32)
    y = xf * lax.rsqrt(jnp.mean(xf * xf, axis=-1, keepdims=True) + EPS)
    return (y * g.astype(jnp.float32)).astype(x.dtype)


def modulate(h, shift, scale):
    return h * (1 + scale) + shift


def split_cols(p):
    offs = np.cumsum(IN_SPLITS)[:-1].tolist()
    return jnp.split(p, offs, axis=-1)


def fourier_mix(u):
    bsz, n, _ = u.shape
    uf = u.astype(jnp.float32).reshape(bsz, n, F_GROUPS, HEAD_DIM)
    y = jnp.fft.fft2(uf, axes=(1, 3), norm='ortho').real
    return y.reshape(bsz, n, F_WIDTH).astype(u.dtype)


def axial_rope(x):
    n = x.shape[1]
    rows = n // GRID_W
    row = jnp.broadcast_to(jnp.arange(rows, dtype=jnp.float32)[:, None], (rows, GRID_W)).reshape(n)
    col = jnp.broadcast_to(jnp.arange(GRID_W, dtype=jnp.float32)[None, :], (rows, GRID_W)).reshape(n)
    half = HEAD_DIM // 2
    quarter = half // 2
    inv = ROPE_THETA ** (-jnp.arange(quarter, dtype=jnp.float32) / quarter)

    def rot(v, pos):
        ang = pos[:, None] * inv
        cos = jnp.cos(ang)[None, :, None, :]
        sin = jnp.sin(ang)[None, :, None, :]
        v1, v2 = v[..., :quarter], v[..., quarter:]
        return jnp.concatenate([v1 * cos - v2 * sin, v1 * sin + v2 * cos], axis=-1)

    xf = x.astype(jnp.float32)
    return jnp.concatenate([rot(xf[..., :half], row), rot(xf[..., half:], col)], axis=-1).astype(x.dtype)


def window_mask(n):
    nb = n // BLOCK
    blk = jnp.arange(nb)[:, None, None]
    qpos = blk * BLOCK + jnp.arange(BLOCK)[None, :, None]
    kpos = (blk - 1) * BLOCK + jnp.arange(3 * BLOCK)[None, None, :]
    return (jnp.abs(qpos - kpos) <= WINDOW) & (kpos >= 0) & (kpos < n)


def window_attention(q, k, v, kc, vc, sink):
    bsz, n = q.shape[:2]
    nb = n // BLOCK
    r = ATT_HEADS // ATT_KV_HEADS
    scale = HEAD_DIM ** -0.5
    qb = q.reshape(bsz, nb, BLOCK, ATT_KV_HEADS, r, HEAD_DIM)

    def band(t):
        tp = jnp.pad(t, ((0, 0), (BLOCK, BLOCK), (0, 0), (0, 0)))
        tp = tp.reshape(bsz, nb + 2, BLOCK, ATT_KV_HEADS, HEAD_DIM)
        return jnp.concatenate([tp[:, :-2], tp[:, 1:-1], tp[:, 2:]], axis=2)

    kb, vb = band(k), band(v)
    s_loc = jnp.einsum('bnqgrd,bnkgd->bngrqk', qb, kb).astype(jnp.float32) * scale
    s_loc = jnp.where(window_mask(n)[None, :, None, None], s_loc, -jnp.inf)
    s_ctx = jnp.einsum('bnqgrd,blgd->bngrql', qb, kc).astype(jnp.float32) * scale
    s_snk = jnp.broadcast_to(sink.astype(jnp.float32).reshape(1, 1, ATT_KV_HEADS, r, 1, 1), s_loc.shape[:-1] + (1,))
    p = jax.nn.softmax(jnp.concatenate([s_loc, s_ctx, s_snk], axis=-1), axis=-1).astype(v.dtype)
    o = (jnp.einsum('bngrqk,bnkgd->bnqgrd', p[..., :3 * BLOCK], vb)
         + jnp.einsum('bngrql,blgd->bnqgrd', p[..., 3 * BLOCK:-1], vc))
    return o.reshape(bsz, n, ATT_WIDTH)


def context_attention(q, k, v, sink):
    bsz, n = q.shape[:2]
    r = ATT_HEADS // ATT_KV_HEADS
    qg = q.reshape(bsz, n, ATT_KV_HEADS, r, HEAD_DIM)
    s = jnp.einsum('bqgrd,bkgd->bgrqk', qg, k).astype(jnp.float32) * HEAD_DIM ** -0.5
    snk = jnp.broadcast_to(sink.astype(jnp.float32).reshape(1, ATT_KV_HEADS, r, 1, 1), s.shape[:-1] + (1,))
    p = jax.nn.softmax(jnp.concatenate([s, snk], axis=-1), axis=-1)[..., :-1].astype(v.dtype)
    o = jnp.einsum('bgrqk,bkgd->bqgrd', p, v)
    return o.reshape(bsz, n, ATT_WIDTH)


def dwconv_silu(t, w, b):
    y = lax.conv_general_dilated(t, w[:, None, :].astype(t.dtype), window_strides=(1,),
                                 padding=[(CONV_K // 2, CONV_K // 2)],
                                 dimension_numbers=('NWC', 'WIO', 'NWC'),
                                 feature_group_count=t.shape[-1])
    return jax.nn.silu(y + b.astype(t.dtype))


def ssd_scan(x, dt, a, bm, cm, h0, with_output):
    bsz, n = x.shape[:2]
    nc = n // SSD_CHUNK
    r = SSD_HEADS // SSD_GROUPS
    xc = x.astype(jnp.float32).reshape(bsz, nc, SSD_CHUNK, SSD_GROUPS, r, HEAD_DIM)
    dtc = dt.reshape(bsz, nc, SSD_CHUNK, SSD_GROUPS, r)
    bc = bm.astype(jnp.float32).reshape(bsz, nc, SSD_CHUNK, SSD_GROUPS, SSD_STATE)
    cc = cm.astype(jnp.float32).reshape(bsz, nc, SSD_CHUNK, SSD_GROUPS, SSD_STATE)
    a_cs = jnp.cumsum(dtc * a.reshape(SSD_GROUPS, r), axis=2)
    xd = xc * dtc[..., None]
    decay_to_end = jnp.exp(a_cs[:, :, -1:] - a_cs)
    states = jnp.einsum('bcjgn,bcjgr,bcjgrp->bcgrpn', bc, decay_to_end, xd)
    chunk_decay = jnp.exp(a_cs[:, :, -1])

    def step(h, inp):
        s_c, d_c = inp
        return h * d_c[..., None, None] + s_c, h

    h_final, h_start = lax.scan(step, h0, (jnp.moveaxis(states, 1, 0), jnp.moveaxis(chunk_decay, 1, 0)))
    if not with_output:
        return None, h_final
    h_start = jnp.moveaxis(h_start, 0, 1)
    seg = a_cs[:, :, :, None] - a_cs[:, :, None, :]
    lower = jnp.tril(jnp.ones((SSD_CHUNK, SSD_CHUNK), dtype=bool))[:, :, None, None]
    decay_in = jnp.exp(jnp.where(lower, seg, -jnp.inf))
    cb = jnp.einsum('bcign,bcjgn->bcgij', cc, bc)
    y_diag = jnp.einsum('bcgij,bcijgr,bcjgrp->bcigrp', cb, decay_in, xd)
    y_off = jnp.einsum('bcign,bcgrpn,bcigr->bcigrp', cc, h_start, jnp.exp(a_cs))
    return (y_diag + y_off).reshape(bsz, n, SSD_HEADS, HEAD_DIM), h_final


def ssd_mixer(z, xbc, dt_raw, conv_w, conv_b, dt_bias, a_log, d_skip, g_ssd, h0_f, h0_b, with_output):
    bsz, n, _ = xbc.shape
    xbc = dwconv_silu(xbc, conv_w, conv_b)
    xs, bm, cm = jnp.split(xbc, [SSD_WIDTH, SSD_WIDTH + SSD_GROUPS * SSD_STATE], axis=-1)
    xs = xs.reshape(bsz, n, SSD_HEADS, HEAD_DIM)
    bm = bm.reshape(bsz, n, SSD_GROUPS, SSD_STATE)
    cm = cm.reshape(bsz, n, SSD_GROUPS, SSD_STATE)
    dt = jax.nn.softplus(dt_raw.astype(jnp.float32).reshape(bsz, n, 2, SSD_HEADS) + dt_bias.astype(jnp.float32))
    a = -jnp.exp(a_log.astype(jnp.float32))

    def rev(t):
        return jnp.flip(t, axis=1)

    y_f, h_f = ssd_scan(xs, dt[:, :, 0], a[0], bm, cm, h0_f, with_output)
    y_b, h_b = ssd_scan(rev(xs), rev(dt[:, :, 1]), a[1], rev(bm), rev(cm), h0_b, with_output)
    if not with_output:
        return None, h_f, h_b
    y = y_f + rev(y_b) + d_skip.astype(jnp.float32)[:, None] * xs.astype(jnp.float32)
    y = y.reshape(bsz, n, SSD_WIDTH).astype(z.dtype) * jax.nn.silu(z)
    return rmsnorm(y, g_ssd), h_f, h_b


def hybrid_mixer(u_ctx, u_lat, w_in, conv_w, conv_b, dt_bias, a_log, d_skip, g_ssd, attn_sink, w_out, ctx_out):
    bsz, n_ctx, _ = u_ctx.shape
    n_lat = u_lat.shape[1]
    f_c, q_c, k_c, v_c, z_c, xbc_c, dt_c = split_cols(u_ctx @ w_in)
    f_l, q_l, k_l, v_l, z_l, xbc_l, dt_l = split_cols(u_lat @ w_in)

    def heads(t, n, h):
        return t.reshape(bsz, n, h, HEAD_DIM)

    k_c = heads(k_c, n_ctx, ATT_KV_HEADS)
    v_c = heads(v_c, n_ctx, ATT_KV_HEADS)
    q_l = axial_rope(heads(q_l, n_lat, ATT_HEADS))
    k_l = axial_rope(heads(k_l, n_lat, ATT_KV_HEADS))
    v_l = heads(v_l, n_lat, ATT_KV_HEADS)
    a_lat = window_attention(q_l, k_l, v_l, k_c, v_c, attn_sink)

    h0 = jnp.zeros((bsz, SSD_GROUPS, SSD_HEADS // SSD_GROUPS, HEAD_DIM, SSD_STATE), jnp.float32)
    s_ctx, hf_c, hb_c = ssd_mixer(z_c, xbc_c, dt_c, conv_w, conv_b, dt_bias, a_log, d_skip, g_ssd, h0, h0, ctx_out)
    s_lat, _, _ = ssd_mixer(z_l, xbc_l, dt_l, conv_w, conv_b, dt_bias, a_log, d_skip, g_ssd, hf_c, hb_c, True)

    out_lat = jnp.concatenate([fourier_mix(f_l), a_lat, s_lat], axis=-1) @ w_out
    if not ctx_out:
        return out_lat, None
    a_ctx = context_attention(heads(q_c, n_ctx, ATT_HEADS), k_c, v_c, attn_sink)
    out_ctx = jnp.concatenate([fourier_mix(f_c), a_ctx, s_ctx], axis=-1) @ w_out
    return out_lat, out_ctx


def expert_choice_ffn(u, w_router, w_gate, w_up, w_down):
    bsz, n, d = u.shape
    cap = EC_CAPACITY * n // N_EXPERTS
    aff = jax.nn.softmax((u @ w_router).astype(jnp.float32), axis=-1)
    gate, idx = lax.top_k(jnp.transpose(aff, (2, 0, 1)), cap)
    bidx = jnp.arange(bsz)[:, None]

    def expert(args):
        ie, ge, wg, wu, wd = args
        xe = u[bidx, ie]
        he = jax.nn.silu(xe @ wg) * (xe @ wu)
        return (he @ wd) * ge[..., None].astype(u.dtype)

    y = lax.map(expert, (idx, gate, w_gate, w_up, w_down))
    return jnp.zeros_like(u).at[bidx[None], idx].add(y)


def setup_inputs(seed: int = 0) -> dict:
    key = jax.random.key(seed)
    ks = jax.random.split(key, 24)
    f32 = jnp.float32
    L, D = DEPTH, D_MODEL

    def nrm(k, shape, s):
        return s * jax.random.normal(k, shape, f32)

    dt0 = jnp.exp(jax.random.uniform(ks[13], (L, 2, SSD_HEADS), f32, math.log(1e-3), math.log(1e-1)))
    return {
        'x': nrm(ks[0], (BATCH, SEQ, D), 1.0),
        'c': nrm(ks[1], (BATCH, D), 1.0),
        'ctx': nrm(ks[2], (BATCH, CTX_LEN, D), 1.0),
        'c_ctx': nrm(ks[3], (D,), 1.0),
        'w_ada': nrm(ks[4], (L, D, 6 * D), 0.5 * D ** -0.5),
        'b_ada': nrm(ks[5], (L, 6 * D), 0.01),
        'g_mix_pre': 1.0 + nrm(ks[6], (L, D), 0.05),
        'g_mix_post': 1.0 + nrm(ks[7], (L, D), 0.05),
        'g_ffn_pre': 1.0 + nrm(ks[8], (L, D), 0.05),
        'g_ffn_post': 1.0 + nrm(ks[9], (L, D), 0.05),
        'w_in': nrm(ks[10], (L, D, IN_WIDTH), D ** -0.5),
        'conv_w': nrm(ks[11], (L, CONV_K, CONV_DIM), CONV_K ** -0.5),
        'conv_b': nrm(ks[12], (L, CONV_DIM), 0.01),
        'dt_bias': dt0 + jnp.log(-jnp.expm1(-dt0)),
        'a_log': jnp.log(jax.random.uniform(ks[14], (L, 2, SSD_HEADS), f32, 1.0, 16.0)),
        'd_skip': 1.0 + nrm(ks[15], (L, SSD_HEADS), 0.1),
        'g_ssd': 1.0 + nrm(ks[16], (L, SSD_WIDTH), 0.05),
        'attn_sink': nrm(ks[17], (L, ATT_HEADS), 0.5),
        'w_out': nrm(ks[18], (L, MIX_WIDTH, D), MIX_WIDTH ** -0.5),
        'w_router': nrm(ks[19], (L, D, N_EXPERTS), D ** -0.5),
        'w_gate': nrm(ks[20], (L, N_EXPERTS, D, EXPERT_FF), D ** -0.5),
        'w_up': nrm(ks[21], (L, N_EXPERTS, D, EXPERT_FF), D ** -0.5),
        'w_down': nrm(ks[22], (L, N_EXPERTS, EXPERT_FF, D), EXPERT_FF ** -0.5),
    }


def reference(x, c, ctx, c_ctx, w_ada, b_ada, g_mix_pre, g_mix_post, g_ffn_pre, g_ffn_post,
              w_in, conv_w, conv_b, dt_bias, a_log, d_skip, g_ssd, attn_sink, w_out,
              w_router, w_gate, w_up, w_down):
    h_lat, h_ctx = x, ctx
    silu_c = jax.nn.silu(c)
    silu_cc = jax.nn.silu(c_ctx)
    for l in range(DEPTH):
        last = l == DEPTH - 1
        mod_lat = silu_c @ w_ada[l] + b_ada[l]
        mod_ctx = silu_cc @ w_ada[l] + b_ada[l]
        sh1, sc1, gt1, sh2, sc2, gt2 = jnp.split(mod_lat[:, None, :], 6, axis=-1)
        sh1c, sc1c, gt1c, sh2c, sc2c, gt2c = jnp.split(mod_ctx, 6, axis=-1)

        u_lat = modulate(rmsnorm(h_lat, g_mix_pre[l]), sh1, sc1)
        u_ctx = modulate(rmsnorm(h_ctx, g_mix_pre[l]), sh1c, sc1c)
        m_lat, m_ctx = hybrid_mixer(u_ctx, u_lat, w_in[l], conv_w[l], conv_b[l], dt_bias[l], a_log[l],
                                    d_skip[l], g_ssd[l], attn_sink[l], w_out[l], not last)
        h_lat = h_lat + gt1 * rmsnorm(m_lat, g_mix_post[l])
        f_lat = expert_choice_ffn(modulate(rmsnorm(h_lat, g_ffn_pre[l]), sh2, sc2),
                                  w_router[l], w_gate[l], w_up[l], w_down[l])
        h_lat = h_lat + gt2 * rmsnorm(f_lat, g_ffn_post[l])

        if not last:
            h_ctx = h_ctx + gt1c * rmsnorm(m_ctx, g_mix_post[l])
            f_ctx = expert_choice_ffn(modulate(rmsnorm(h_ctx, g_ffn_pre[l]), sh2c, sc2c),
                                      w_router[l], w_gate[l], w_up[l], w_down[l])
            h_ctx = h_ctx + gt2c * rmsnorm(f_ctx, g_ffn_post[l])
    return h_lat
```

```python
import functools
import math

import jax
import jax.numpy as jnp
from jax import lax
from jax.experimental import pallas as pl
from jax.experimental.pallas import tpu as pltpu

F32 = jnp.float32
BF16 = jnp.bfloat16

D_MODEL = 1024
SEQ = 4096
CTX_LEN = 256
S_ALL = SEQ + CTX_LEN
GRID_W = 64
HEAD_DIM = 64
EPS = 1e-6
F_WIDTH = 256
ATT_HEADS = 6
ATT_KV_HEADS = 2
ATT_WIDTH = 384
KV_WIDTH = 128
BLOCK = 128
ROPE_THETA = 10000.0
SSD_HEADS = 6
SSD_WIDTH = 384
SSD_GROUPS = 2
SSD_STATE = 128
CHUNK = 128
CONV_K = 5
CONV_DIM = 896
IN_WIDTH = 2188
IN_PAD = 2304
N_EXPERTS = 16
EXPERT_FF = 2816
EC_CAPACITY = 2

LANE = 128
ROW_TILE = 256
VMEM_LIMIT = 56 * 1024 * 1024


def _cparams(sem, vmem=None):
    return pltpu.CompilerParams(dimension_semantics=sem, vmem_limit_bytes=vmem)


def _split_bf16(x, n):
    parts, r = [], x
    for _ in range(n):
        p = r.astype(BF16)
        parts.append(p)
        r = r - p.astype(F32)
    return parts


def _rms(x, g):
    return x * lax.rsqrt(jnp.mean(x * x, axis=-1, keepdims=True) + EPS) * g


def _silu(x):
    return x * jax.nn.sigmoid(x)


def _mod_kernel(cc_ref, w_ref, b_ref, o_ref):
    s = _silu(cc_ref[...])
    o_ref[0] = jnp.dot(s.astype(BF16), w_ref[0].astype(BF16), preferred_element_type=F32) + b_ref[0]


def _modulation(cc, w_ada, b_ada):
    n_l, d, d6 = w_ada.shape
    r = cc.shape[0]
    tn = 1024
    return pl.pallas_call(
        _mod_kernel,
        grid=(n_l, d6 // tn),
        in_specs=[pl.BlockSpec((r, d), lambda l, j: (0, 0)),
                  pl.BlockSpec((1, d, tn), lambda l, j: (l, 0, j)),
                  pl.BlockSpec((1, 1, tn), lambda l, j: (l, 0, j))],
        out_specs=pl.BlockSpec((1, r, tn), lambda l, j: (l, 0, j)),
        out_shape=jax.ShapeDtypeStruct((n_l, r, d6), F32),
        compiler_params=_cparams(("arbitrary", "arbitrary")),
        name="adaln_mod",
    )(cc, w_ada, b_ada.reshape(n_l, 1, d6))


def _in_kernel(*refs, residual):
    if residual:
        (h_ref, f_ref, gpost_ref, mod_ref, g_ref, w_ref, cos_ref, sa_ref, sb_ref,
         hout_ref, fo_ref, q_ref, k_ref, v_ref, z_ref, xbc_ref, dt_ref) = refs
    else:
        (h_ref, mod_ref, g_ref, w_ref, cos_ref, sa_ref, sb_ref,
         fo_ref, q_ref, k_ref, v_ref, z_ref, xbc_ref, dt_ref) = refs
    x = h_ref[0]
    mod = mod_ref[0, 0]
    if residual:
        x = x + mod[5:6] * _rms(f_ref[0], gpost_ref[...])
        hout_ref[0] = x
    u = _rms(x, g_ref[...]) * (1.0 + mod[1:2]) + mod[0:1]
    p = jnp.dot(u.astype(BF16), w_ref[...], preferred_element_type=F32)
    cos, sa, sb = cos_ref[...], sa_ref[...], sb_ref[...]

    def rope(t):
        return t * cos + pltpu.roll(t, LANE - 16, 1) * sa + pltpu.roll(t, 16, 1) * sb

    fo_ref[0] = p[:, 0:256].astype(BF16)
    q_ref[0] = jnp.concatenate([rope(p[:, 256 + LANE * s:256 + LANE * (s + 1)]) for s in range(3)],
                               axis=1).astype(BF16)
    k_ref[0] = rope(p[:, 640:768]).astype(BF16)
    v_ref[0] = p[:, 768:896].astype(BF16)
    z_ref[0] = p[:, 896:1280].astype(BF16)
    xbc_ref[0] = p[:, 1280:2176].astype(BF16)
    dt_ref[0] = p[:, 2176:2304]


def _in_proj(h, mod_l, mod_prev, g_pre, w_in_p, rope_tabs, f_prev=None, g_post_prev=None):
    bsz, s, d = h.shape
    tm = ROW_TILE
    n_lat = SEQ // tm
    residual = f_prev is not None
    row = lambda b, i: (b, i, 0)
    mod_map = lambda b, i: (jnp.where(i < n_lat, b, bsz), 0, 0, 0)
    const2 = lambda b, i: (0, 0)
    tab = pl.BlockSpec((tm, LANE), lambda b, i: (i, 0))
    in_specs = [pl.BlockSpec((1, tm, d), row)]
    args = [h]
    if residual:
        in_specs += [pl.BlockSpec((1, tm, d), row), pl.BlockSpec((1, d), const2)]
        args += [f_prev, g_post_prev.reshape(1, d)]
    mod_used = mod_l if not residual else jnp.concatenate([mod_l[:, :5], mod_prev[:, 5:6]], axis=1)
    in_specs += [pl.BlockSpec((1, 1, 6, d), mod_map), pl.BlockSpec((1, d), const2),
                 pl.BlockSpec((d, IN_PAD), const2), tab, tab, tab]
    args += [mod_used.reshape(mod_used.shape[0], 1, 6, d), g_pre.reshape(1, d), w_in_p, *rope_tabs]
    widths = [(256, BF16), (384, BF16), (128, BF16), (128, BF16), (384, BF16), (896, BF16), (128, F32)]
    out_specs = [pl.BlockSpec((1, tm, w), row) for w, _ in widths]
    out_shape = [jax.ShapeDtypeStruct((bsz, s, w), dt) for w, dt in widths]
    if residual:
        out_specs = [pl.BlockSpec((1, tm, d), row)] + out_specs
        out_shape = [jax.ShapeDtypeStruct((bsz, s, d), F32)] + out_shape
    return pl.pallas_call(
        functools.partial(_in_kernel, residual=residual),
        grid=(bsz, s // tm),
        in_specs=in_specs, out_specs=out_specs, out_shape=out_shape,
        compiler_params=_cparams(("parallel", "arbitrary"), VMEM_LIMIT),
        name="prenorm_inproj",
    )(*args)


def _f1_kernel(u_ref, w_ref, a_ref, b_ref):
    r = jnp.dot(u_ref[0], w_ref[...], preferred_element_type=F32)
    a_ref[...] = r[:, :F_WIDTH].astype(BF16)
    b_ref[...] = r[:, F_WIDTH:].astype(BF16)


def _f2_kernel(c_ref, s_ref, a_ref, b_ref, *rest):
    o_ref = rest[-1]
    y = (jnp.dot(c_ref[...], a_ref[...], preferred_element_type=F32)
         - jnp.dot(s_ref[...], b_ref[...], preferred_element_type=F32))
    o_ref[0] = y.astype(BF16)


def _dft_tables(n):
    k = jnp.arange(n, dtype=jnp.int32)
    m = (k[:, None] * k[None, :]) % n
    ang = m.astype(F32) * (2.0 * math.pi / n)
    sc = n ** -0.5
    return (jnp.cos(ang) * sc).astype(BF16), (jnp.sin(ang) * sc).astype(BF16)


def _channel_tables():
    c64, s64 = _dft_tables(HEAD_DIM)
    eye = jnp.eye(F_WIDTH // HEAD_DIM, dtype=BF16)
    return jnp.concatenate([jnp.kron(eye, c64), jnp.kron(eye, s64)], axis=1)


def _fourier(fo, chan_tab, cn, sn, row0, n, out_prev=None):
    bsz, s, _ = fo.shape
    tm = min(n, 512)
    nt = n // tm
    blk0 = row0 // tm
    a_all, b_all = pl.pallas_call(
        _f1_kernel,
        grid=(bsz, nt),
        in_specs=[pl.BlockSpec((1, tm, F_WIDTH), lambda b, i: (b, blk0 + i, 0)),
                  pl.BlockSpec((F_WIDTH, 2 * F_WIDTH), lambda b, i: (0, 0))],
        out_specs=[pl.BlockSpec((tm, F_WIDTH), lambda b, i: (i, b))] * 2,
        out_shape=[jax.ShapeDtypeStruct((n, bsz * F_WIDTH), BF16)] * 2,
        compiler_params=_cparams(("parallel", "arbitrary")),
        name="fourier_channels",
    )(fo, chan_tab)
    in_specs = [pl.BlockSpec((tm, n), lambda i, b: (i, 0)),
                pl.BlockSpec((tm, n), lambda i, b: (i, 0)),
                pl.BlockSpec((n, F_WIDTH), lambda i, b: (0, b)),
                pl.BlockSpec((n, F_WIDTH), lambda i, b: (0, b))]
    args = [cn, sn, a_all, b_all]
    aliases = {}
    if out_prev is not None:
        in_specs.append(pl.BlockSpec(memory_space=pl.ANY))
        args.append(out_prev)
        aliases = {4: 0}
    return pl.pallas_call(
        _f2_kernel,
        grid=(nt, bsz),
        in_specs=in_specs,
        out_specs=pl.BlockSpec((1, tm, F_WIDTH), lambda i, b: (b, blk0 + i, 0)),
        out_shape=jax.ShapeDtypeStruct((bsz, s, F_WIDTH), BF16),
        input_output_aliases=aliases,
        compiler_params=_cparams(("parallel", "arbitrary"), VMEM_LIMIT),
        name="fourier_positions",
    )(*args)


def _attend(q, keys, vals, masks, sink_ref):
    n = q.shape[0]
    r = ATT_HEADS // ATT_KV_HEADS
    outs = []
    for g in range(ATT_KV_HEADS):
        qg = jnp.concatenate([q[:, HEAD_DIM * (r * g + j):HEAD_DIM * (r * g + j + 1)] for j in range(r)], axis=0)
        rid = lax.broadcasted_iota(jnp.int32, (r * n, 1), 0)
        snk = jnp.full((r * n, 1), sink_ref[r * g], F32)
        for j in range(1, r):
            snk = jnp.where(rid >= j * n, sink_ref[r * g + j], snk)
        scores = []
        mx = snk
        for kk, mk in zip(keys, masks):
            kg = kk[:, HEAD_DIM * g:HEAD_DIM * (g + 1)]
            sc = lax.dot_general(qg, kg, (((1,), (1,)), ((), ())), preferred_element_type=F32)
            if mk is not None:
                sc = jnp.where(mk, sc, -1e30)
            scores.append(sc)
            mx = jnp.maximum(mx, jnp.max(sc, axis=-1, keepdims=True))
        den = jnp.exp(snk - mx)
        acc = jnp.zeros((r * n, HEAD_DIM), F32)
        for sc, vv in zip(scores, vals):
            p = jnp.exp(sc - mx)
            den = den + jnp.sum(p, axis=-1, keepdims=True)
            acc = acc + jnp.dot(p.astype(BF16), vv[:, HEAD_DIM * g:HEAD_DIM * (g + 1)],
                                preferred_element_type=F32)
        o = acc / den
        outs += [o[j * n:(j + 1) * n] for j in range(r)]
    return jnp.concatenate(outs, axis=1)


def _attn_lat_kernel(sink_ref, q_ref, kp_ref, kc_ref, kn_ref, kx_ref, vp_ref, vc_ref, vn_ref, vx_ref, o_ref):
    nb = pl.num_programs(1)
    n = pl.program_id(1)
    rows = (ATT_HEADS // ATT_KV_HEADS) * BLOCK
    i = lax.broadcasted_iota(jnp.int32, (rows, BLOCK), 0) & (BLOCK - 1)
    j = lax.broadcasted_iota(jnp.int32, (rows, BLOCK), 1)
    m_prev = (j >= i) & (n > 0)
    m_next = (j <= i) & (n < nb - 1)
    o_ref[0] = _attend(q_ref[0], [kp_ref[0], kc_ref[0], kn_ref[0], kx_ref[0]],
                       [vp_ref[0], vc_ref[0], vn_ref[0], vx_ref[0]],
                       [m_prev, None, m_next, None], sink_ref).astype(BF16)


def _attn_ctx_kernel(sink_ref, q_ref, kx_ref, vx_ref, prev_ref, o_ref):
    o_ref[0] = _attend(q_ref[0], [kx_ref[0]], [vx_ref[0]], [None], sink_ref).astype(BF16)


def _attention(q, k, v, sink, with_ctx):
    bsz, s, _ = q.shape
    nb = SEQ // BLOCK
    ctx_blk = SEQ // CTX_LEN
    smem = pl.BlockSpec(memory_space=pltpu.SMEM)
    kv = lambda f: pl.BlockSpec((1, BLOCK, KV_WIDTH), f)
    prev = lambda b, n: (b, jnp.maximum(n - 1, 0), 0)
    cur = lambda b, n: (b, n, 0)
    nxt = lambda b, n: (b, jnp.minimum(n + 1, nb - 1), 0)
    ctx = pl.BlockSpec((1, CTX_LEN, KV_WIDTH), lambda b, n: (b, ctx_blk, 0))
    sink8 = jnp.pad(sink, (0, 8 - ATT_HEADS))
    out = pl.pallas_call(
        _attn_lat_kernel,
        grid=(bsz, nb),
        in_specs=[smem, pl.BlockSpec((1, BLOCK, ATT_WIDTH), cur),
                  kv(prev), kv(cur), kv(nxt), ctx, kv(prev), kv(cur), kv(nxt), ctx],
        out_specs=pl.BlockSpec((1, BLOCK, ATT_WIDTH), cur),
        out_shape=jax.ShapeDtypeStruct((bsz, s, ATT_WIDTH), BF16),
        compiler_params=_cparams(("parallel", "arbitrary")),
        name="window_attention",
    )(sink8, q, k, k, k, k, v, v, v, v)
    if not with_ctx:
        return out
    cq = lambda b: (b, ctx_blk, 0)
    return pl.pallas_call(
        _attn_ctx_kernel,
        grid=(bsz,),
        in_specs=[smem, pl.BlockSpec((1, CTX_LEN, ATT_WIDTH), cq),
                  pl.BlockSpec((1, CTX_LEN, KV_WIDTH), cq), pl.BlockSpec((1, CTX_LEN, KV_WIDTH), cq),
                  pl.BlockSpec(memory_space=pl.ANY)],
        out_specs=pl.BlockSpec((1, CTX_LEN, ATT_WIDTH), cq),
        out_shape=jax.ShapeDtypeStruct((bsz, s, ATT_WIDTH), BF16),
        input_output_aliases={4: 0},
        compiler_params=_cparams(("parallel",)),
        name="context_attention",
    )(sink8, q, k, v, out)


HALO = 16


def _conv_kernel(xp_ref, xc_ref, xn_ref, w_ref, b_ref, o_ref):
    c = pl.program_id(1)
    n_lat = SEQ // CHUNK
    last = pl.num_programs(1) - 1
    has_prev = (c != 0) & (c != n_lat)
    has_next = (c != n_lat - 1) & (c != last)
    xp = jnp.where(has_prev, xp_ref[0].astype(F32), 0.0)
    xn = jnp.where(has_next, xn_ref[0].astype(F32), 0.0)
    xx = jnp.concatenate([xp, xc_ref[0].astype(F32), xn], axis=0)
    rows = xx.shape[0]
    acc = jnp.zeros((CHUNK, xx.shape[1]), F32) + b_ref[...]
    for t in range(CONV_K):
        sh = (CONV_K // 2 - t) % rows
        xs = xx if sh == 0 else pltpu.roll(xx, sh, 0)
        acc = acc + w_ref[t:t + 1, :] * xs[HALO:HALO + CHUNK]
    o_ref[0] = _silu(acc).astype(BF16)


def _conv_silu(xbc, conv_w, conv_b):
    bsz, s, cdim = xbc.shape
    nc = s // CHUNK
    per = CHUNK // HALO
    nh = s // HALO
    return pl.pallas_call(
        _conv_kernel,
        grid=(bsz, nc),
        in_specs=[pl.BlockSpec((1, HALO, cdim), lambda b, c: (b, jnp.maximum(c * per - 1, 0), 0)),
                  pl.BlockSpec((1, CHUNK, cdim), lambda b, c: (b, c, 0)),
                  pl.BlockSpec((1, HALO, cdim), lambda b, c: (b, jnp.minimum(c * per + per, nh - 1), 0)),
                  pl.BlockSpec((8, cdim), lambda b, c: (0, 0)),
                  pl.BlockSpec((1, cdim), lambda b, c: (0, 0))],
        out_specs=pl.BlockSpec((1, CHUNK, cdim), lambda b, c: (b, c, 0)),
        out_shape=jax.ShapeDtypeStruct((bsz, s, cdim), BF16),
        compiler_params=_cparams(("parallel", "arbitrary")),
        name="dwconv_silu",
    )(xbc, xbc, xbc, jnp.pad(conv_w, ((0, 8 - CONV_K), (0, 0))), conv_b.reshape(1, cdim))


def _ssd_kernel(x_ref, dt_ref, dtb_ref, alog_ref, y_ref, state_ref, *, reverse):
    step = pl.program_id(1)

    @pl.when(step == 0)
    def _():
        state_ref[...] = jnp.zeros_like(state_ref)

    xc = x_ref[0]
    xs = xc[:, :SSD_WIDTH].astype(F32)
    gs = SSD_GROUPS * SSD_STATE
    bm = xc[:, SSD_WIDTH:SSD_WIDTH + gs]
    cm = xc[:, SSD_WIDTH + gs:SSD_WIDTH + 2 * gs]
    raw = dt_ref[0] + dtb_ref[...]
    dt = jnp.maximum(raw, 0.0) + jnp.log(1.0 + jnp.exp(-jnp.abs(raw)))
    dta = dt * (-jnp.exp(alog_ref[...]))
    ri = lax.broadcasted_iota(jnp.int32, (CHUNK, CHUNK), 0)
    ci = lax.broadcasted_iota(jnp.int32, (CHUNK, CHUNK), 1)
    keep = (ri <= ci) if reverse else (ri >= ci)
    tri = keep.astype(BF16)
    tri_t = ((ri >= ci) if reverse else (ri <= ci)).astype(BF16)
    acs = sum(jnp.dot(tri, p, preferred_element_type=F32) for p in _split_bf16(dta, 3))
    acs_t = sum(jnp.dot(p, tri_t, preferred_element_type=F32) for p in _split_bf16(dta.T, 3))
    edge = 0 if reverse else CHUNK - 1
    tot = acs[edge:edge + 1, :]
    dte = jnp.exp(tot - acs)
    rh = SSD_HEADS // SSD_GROUPS
    gw = rh * HEAD_DIM
    lane0 = SSD_HEADS if reverse else 0

    def expand(m):
        return jnp.concatenate(
            [jnp.broadcast_to(m[:, lane0 + h:lane0 + h + 1], (m.shape[0], HEAD_DIM)) for h in range(SSD_HEADS)],
            axis=1)

    xd = xs * expand(dt)
    xdw = (xd * expand(dte)).astype(BF16)
    xdb = xd.astype(BF16)
    eacs = jnp.exp(expand(acs))
    etot = jnp.exp(expand(tot))
    state = state_ref[...]
    ys = []
    new_state = []
    for g in range(SSD_GROUPS):
        bg = bm[:, g * SSD_STATE:(g + 1) * SSD_STATE]
        cg = cm[:, g * SSD_STATE:(g + 1) * SSD_STATE]
        cb = lax.dot_general(cg, bg, (((1,), (1,)), ((), ())), preferred_element_type=F32)
        sg = state[:, g * gw:(g + 1) * gw]
        y_off = jnp.dot(cg, sg.astype(BF16), preferred_element_type=F32) * eacs[:, g * gw:(g + 1) * gw]
        for r in range(rh):
            h = g * rh + r
            ln = lane0 + h
            dec = jnp.where(keep, jnp.exp(acs[:, ln:ln + 1] - acs_t[ln:ln + 1, :]), 0.0)
            yd = jnp.dot((cb * dec).astype(BF16), xdb[:, h * HEAD_DIM:(h + 1) * HEAD_DIM],
                         preferred_element_type=F32)
            ys.append(yd + y_off[:, r * HEAD_DIM:(r + 1) * HEAD_DIM])
        bt = bg.astype(F32).T.astype(BF16)
        new_state.append(jnp.dot(bt, xdw[:, g * gw:(g + 1) * gw], preferred_element_type=F32))
    y_ref[0] = jnp.concatenate(ys, axis=1)
    state_ref[...] = state * etot + jnp.concatenate(new_state, axis=1)


def _ssd_scan(xconv, dt_raw, dt_bias, a_log, reverse):
    bsz, s, cdim = xconv.shape
    nc = s // CHUNK
    n_lat = SEQ // CHUNK
    if reverse:
        cmap = lambda b, t: (b, nc - 1 - t, 0)
    else:
        cmap = lambda b, t: (b, (t + n_lat) % nc, 0)
    pad = lambda p: jnp.pad(p.reshape(1, -1), ((0, 0), (0, LANE - 2 * SSD_HEADS)))
    return pl.pallas_call(
        functools.partial(_ssd_kernel, reverse=reverse),
        grid=(bsz, nc),
        in_specs=[pl.BlockSpec((1, CHUNK, cdim), cmap),
                  pl.BlockSpec((1, CHUNK, LANE), cmap),
                  pl.BlockSpec((1, LANE), lambda b, t: (0, 0)),
                  pl.BlockSpec((1, LANE), lambda b, t: (0, 0))],
        out_specs=pl.BlockSpec((1, CHUNK, SSD_WIDTH), cmap),
        out_shape=jax.ShapeDtypeStruct((bsz, s, SSD_WIDTH), F32),
        scratch_shapes=[pltpu.VMEM((SSD_STATE, SSD_WIDTH), F32)],
        compiler_params=_cparams(("parallel", "arbitrary")),
        name="ssd_scan_bwd" if reverse else "ssd_scan_fwd",
    )(xconv, dt_raw, pad(dt_bias), pad(a_log))


def _out_kernel(fo_ref, at_ref, yf_ref, yb_ref, xs_ref, z_ref, h_ref, mod_ref, dsk_ref, gssd_ref,
                gpost_ref, gpre_ref, wo_ref, wr_ref, hout_ref, u_ref, lg_ref):
    xs = xs_ref[0].astype(F32)
    y = (yf_ref[0] + yb_ref[0] + dsk_ref[...] * xs) * _silu(z_ref[0].astype(F32))
    sm = _rms(y, gssd_ref[...]).astype(BF16)
    m = (jnp.dot(fo_ref[0], wo_ref[0:256, :], preferred_element_type=F32)
         + jnp.dot(at_ref[0], wo_ref[256:640, :], preferred_element_type=F32)
         + jnp.dot(sm, wo_ref[640:1024, :], preferred_element_type=F32))
    mod = mod_ref[0, 0]
    hn = h_ref[0] + mod[2:3] * _rms(m, gpost_ref[...])
    hout_ref[0] = hn
    u = _rms(hn, gpre_ref[...]) * (1.0 + mod[4:5]) + mod[3:4]
    half = D_MODEL // 2
    lo = pltpu.bitcast(u[:, :half].astype(BF16).astype(F32), jnp.uint32) >> 16
    hi = pltpu.bitcast(u[:, half:].astype(BF16).astype(F32), jnp.uint32) & jnp.uint32(0xFFFF0000)
    u_ref[0] = lo | hi
    uh, ul = _split_bf16(u, 2)
    wh, wl = wr_ref[0], wr_ref[1]
    nt = (((1,), (1,)), ((), ()))
    lg_ref[0] = (lax.dot_general(wh, uh, nt, preferred_element_type=F32)
                 + lax.dot_general(wh, ul, nt, preferred_element_type=F32)
                 + lax.dot_general(wl, uh, nt, preferred_element_type=F32))


def _out_proj(fo, at, yf, yb, xconv, z, h, mod_l, d_skip, g_ssd, g_post, g_pre, w_out, w_router, n_rows):
    bsz, s, d = h.shape
    tm = ROW_TILE
    n_lat = SEQ // tm
    row = lambda b, i: (b, i, 0)
    c2 = lambda b, i: (0, 0)
    wr = jnp.stack(_split_bf16(w_router.T, 2))
    return pl.pallas_call(
        _out_kernel,
        grid=(bsz, n_rows // tm),
        in_specs=[pl.BlockSpec((1, tm, F_WIDTH), row), pl.BlockSpec((1, tm, ATT_WIDTH), row),
                  pl.BlockSpec((1, tm, SSD_WIDTH), row), pl.BlockSpec((1, tm, SSD_WIDTH), row),
                  pl.BlockSpec((1, tm, SSD_WIDTH), row), pl.BlockSpec((1, tm, SSD_WIDTH), row),
                  pl.BlockSpec((1, tm, d), row),
                  pl.BlockSpec((1, 1, 6, d), lambda b, i: (jnp.where(i < n_lat, b, bsz), 0, 0, 0)),
                  pl.BlockSpec((1, SSD_WIDTH), c2), pl.BlockSpec((1, SSD_WIDTH), c2),
                  pl.BlockSpec((1, d), c2), pl.BlockSpec((1, d), c2),
                  pl.BlockSpec((d, d), c2), pl.BlockSpec((2, N_EXPERTS, d), lambda b, i: (0, 0, 0))],
        out_specs=[pl.BlockSpec((1, tm, d), row), pl.BlockSpec((1, tm, d // 2), row),
                   pl.BlockSpec((1, N_EXPERTS, tm), lambda b, i: (b, 0, i))],
        out_shape=[jax.ShapeDtypeStruct((bsz, n_rows, d), F32),
                   jax.ShapeDtypeStruct((bsz, n_rows, d // 2), jnp.uint32),
                   jax.ShapeDtypeStruct((bsz, N_EXPERTS, n_rows), F32)],
        compiler_params=_cparams(("parallel", "arbitrary"), VMEM_LIMIT),
        name="outproj_norms_router",
    )(fo, at, yf, yb, xconv, z, h, mod_l.reshape(mod_l.shape[0], 1, 6, d),
      jnp.repeat(d_skip, HEAD_DIM).reshape(1, SSD_WIDTH), g_ssd.reshape(1, SSD_WIDTH),
      g_post.reshape(1, d), g_pre.reshape(1, d), w_out.astype(BF16), wr)


def _prefix_lanes(mask, upper):
    t = mask.shape[1]
    carry = jnp.zeros((mask.shape[0], 1), F32)
    outs = []
    for j in range(t // LANE):
        pre = jnp.dot(mask[:, j * LANE:(j + 1) * LANE].astype(BF16), upper, preferred_element_type=F32) + carry
        outs.append(pre)
        carry = pre[:, LANE - 1:LANE]
    return jnp.concatenate(outs, axis=1)


def _route_kernel(lg_ref, idx_ref, aff_ref, *, cap):
    l = lg_ref[0]
    n_e, t = l.shape
    e = jnp.exp(l - jnp.max(l, axis=0, keepdims=True))
    aff = e / jnp.sum(e, axis=0, keepdims=True)
    aff_ref[0] = aff
    keys = pltpu.bitcast(aff, jnp.int32)

    def search(it, lo):
        cand = lo | (jnp.int32(1) << (30 - it))
        cnt = jnp.sum((keys >= cand).astype(F32), axis=1, keepdims=True)
        return jnp.where(cnt >= cap, cand, lo)

    thr = lax.fori_loop(0, 31, search, jnp.zeros((n_e, 1), jnp.int32))
    gt = keys > thr
    eq = keys == thr
    need = cap - jnp.sum(gt.astype(F32), axis=1, keepdims=True)
    ri = lax.broadcasted_iota(jnp.int32, (LANE, LANE), 0)
    ci = lax.broadcasted_iota(jnp.int32, (LANE, LANE), 1)
    upper = (ri <= ci).astype(BF16)
    eq_rank = _prefix_lanes(eq.astype(F32), upper)
    sel = gt | (eq & (eq_rank <= need))
    csum = _prefix_lanes(sel.astype(F32), upper)
    csum_t = jnp.concatenate([csum, jnp.zeros((LANE - n_e, t), F32)], axis=0).T
    slot = lax.broadcasted_iota(jnp.int32, (1, cap), 1).astype(F32)
    tc = min(t, 512)
    rows = []
    for x in range(n_e):
        cnt = jnp.zeros((1, cap), F32)
        for j in range(t // tc):
            below = csum_t[j * tc:(j + 1) * tc, x:x + 1] <= slot
            cnt = cnt + jnp.sum(below.astype(F32), axis=0, keepdims=True)
        rows.append(cnt)
    idx_ref[0] = jnp.concatenate(rows, axis=0).astype(jnp.int32)


def _route(logits_t, row0, t, cap):
    bsz, n_e, _ = logits_t.shape
    return pl.pallas_call(
        functools.partial(_route_kernel, cap=cap),
        grid=(bsz,),
        in_specs=[pl.BlockSpec((1, n_e, t), lambda b: (b, 0, row0 // t))],
        out_specs=[pl.BlockSpec((1, n_e, cap), lambda b: (b, 0, 0)),
                   pl.BlockSpec((1, n_e, t), lambda b: (b, 0, 0))],
        out_shape=[jax.ShapeDtypeStruct((bsz, n_e, cap), jnp.int32),
                   jax.ShapeDtypeStruct((bsz, n_e, t), F32)],
        compiler_params=_cparams(("parallel",), VMEM_LIMIT),
        name="expert_choice_route",
    )(logits_t)


def _smem_row(n):
    return pl.BlockSpec((1, 1, 1, n), lambda b, e: (b, e, 0, 0), memory_space=pltpu.SMEM)


def _gather_kernel(idx_ref, u_ref, o_ref, *, cap):
    def body(i, carry):
        t = idx_ref[0, 0, 0, i]
        o_ref[0, pl.ds(i, 1), :] = u_ref[0, pl.ds(t, 1), :]
        return carry

    lax.fori_loop(0, cap, body, 0, unroll=8)


def _gather(u_packed, idx, row0, t, cap):
    bsz, _, half = u_packed.shape
    n_e = idx.shape[1]
    return pl.pallas_call(
        functools.partial(_gather_kernel, cap=cap),
        grid=(bsz, n_e),
        in_specs=[_smem_row(cap), pl.BlockSpec((1, t, half), lambda b, e: (b, row0 // t, 0))],
        out_specs=pl.BlockSpec((1, cap, half), lambda b, e: (e, b, 0)),
        out_shape=jax.ShapeDtypeStruct((n_e, bsz * cap, half), jnp.uint32),
        compiler_params=_cparams(("parallel", "arbitrary"), VMEM_LIMIT),
        name="expert_gather",
    )(idx[:, :, None, :], u_packed)


def _ffn_kernel(x_ref, wg_ref, wu_ref, wd_ref, o_ref, xs_ref, acc_ref):
    f = pl.program_id(2)

    @pl.when(f == 0)
    def _():
        w = x_ref[0]
        half = w.shape[1]
        xs_ref[:, :half] = pltpu.bitcast(w << 16, F32).astype(BF16)
        xs_ref[:, half:] = pltpu.bitcast(w & jnp.uint32(0xFFFF0000), F32).astype(BF16)
        acc_ref[...] = jnp.zeros_like(acc_ref)

    x = xs_ref[...]
    g = jnp.dot(x, wg_ref[0].astype(BF16), preferred_element_type=F32)
    u = jnp.dot(x, wu_ref[0].astype(BF16), preferred_element_type=F32)
    hm = (_silu(g) * u).astype(BF16)
    acc_ref[...] += jnp.dot(hm, wd_ref[0].astype(BF16), preferred_element_type=F32)

    @pl.when(f == pl.num_programs(2) - 1)
    def _():
        o_ref[0] = acc_ref[...].astype(BF16)


def _expert_ffn(xe, w_gate, w_up, w_down):
    n_e, m, half = xe.shape
    d = 2 * half
    ff = w_gate.shape[-1]
    tm = min(m, 2048)
    tf = 256
    return pl.pallas_call(
        _ffn_kernel,
        grid=(n_e, m // tm, ff // tf),
        in_specs=[pl.BlockSpec((1, tm, half), lambda e, i, f: (e, i, 0)),
                  pl.BlockSpec((1, d, tf), lambda e, i, f: (e, 0, f)),
                  pl.BlockSpec((1, d, tf), lambda e, i, f: (e, 0, f)),
                  pl.BlockSpec((1, tf, d), lambda e, i, f: (e, f, 0))],
        out_specs=pl.BlockSpec((1, tm, d), lambda e, i, f: (e, i, 0)),
        out_shape=jax.ShapeDtypeStruct((n_e, m, d), BF16),
        scratch_shapes=[pltpu.VMEM((tm, d), BF16), pltpu.VMEM((tm, d), F32)],
        compiler_params=_cparams(("parallel", "parallel", "arbitrary"), VMEM_LIMIT),
        name="expert_ffn",
    )(xe, w_gate, w_up, w_down)


COMB_ROWS = 16


def _combine_kernel(idx_ref, aff_ref, y_ref, *rest, cap):
    o_ref = rest[-1]

    @pl.when(pl.program_id(1) == 0)
    def _():
        o_ref[...] = jnp.zeros_like(o_ref)

    def body(c, carry):
        base = pl.multiple_of(c * COMB_ROWS, COMB_ROWS)
        tile = y_ref[0, pl.ds(base, COMB_ROWS), :].astype(F32)
        for r in range(COMB_ROWS):
            t = idx_ref[0, 0, 0, base + r]
            o_ref[0, pl.ds(t, 1), :] += aff_ref[0, 0, 0, t] * tile[r:r + 1, :]
        return carry

    lax.fori_loop(0, cap // COMB_ROWS, body, 0)


def _combine(y, idx, aff, row0, t, cap, s_total, out_prev=None):
    n_e, m, d = y.shape
    bsz = idx.shape[0]
    in_specs = [_smem_row(cap), _smem_row(t), pl.BlockSpec((1, cap, d), lambda b, e: (e, b, 0))]
    args = [idx[:, :, None, :], aff[:, :, None, :], y]
    aliases = {}
    if out_prev is not None:
        in_specs.append(pl.BlockSpec(memory_space=pl.ANY))
        args.append(out_prev)
        aliases = {3: 0}
    return pl.pallas_call(
        functools.partial(_combine_kernel, cap=cap),
        grid=(bsz, n_e),
        in_specs=in_specs,
        out_specs=pl.BlockSpec((1, t, d), lambda b, e: (b, row0 // t, 0)),
        out_shape=jax.ShapeDtypeStruct((bsz, s_total, d), F32),
        input_output_aliases=aliases,
        compiler_params=_cparams(("parallel", "arbitrary"), VMEM_LIMIT),
        name="expert_combine",
    )(*args)


def _moe(u_packed, logits_t, w_gate, w_up, w_down, with_ctx):
    bsz, rows, _ = u_packed.shape
    out = None
    sets = [(0, SEQ)] + ([(SEQ, CTX_LEN)] if with_ctx else [])
    for row0, t in sets:
        cap = EC_CAPACITY * t // N_EXPERTS
        idx, aff = _route(logits_t, row0, t, cap)
        xe = _gather(u_packed, idx, row0, t, cap)
        y = _expert_ffn(xe, w_gate, w_up, w_down)
        out = _combine(y, idx, aff, row0, t, cap, rows, out)
    return out


def _final_kernel(h_ref, f_ref, mod_ref, g_ref, o_ref):
    o_ref[0] = h_ref[0] + mod_ref[0, 0][5:6] * _rms(f_ref[0], g_ref[...])


def _final_residual(h, f, mod_l, g_post):
    bsz, n, d = h.shape
    tm = ROW_TILE
    row = lambda b, i: (b, i, 0)
    return pl.pallas_call(
        _final_kernel,
        grid=(bsz, n // tm),
        in_specs=[pl.BlockSpec((1, tm, d), row), pl.BlockSpec((1, tm, d), row),
                  pl.BlockSpec((1, 1, 6, d), lambda b, i: (b, 0, 0, 0)), pl.BlockSpec((1, d), lambda b, i: (0, 0))],
        out_specs=pl.BlockSpec((1, tm, d), row),
        out_shape=jax.ShapeDtypeStruct((bsz, n, d), F32),
        compiler_params=_cparams(("parallel", "arbitrary")),
        name="ffn_residual",
    )(h, f, mod_l.reshape(mod_l.shape[0], 1, 6, d), g_post.reshape(1, d))


def _rope_tables():
    n = jnp.arange(SEQ, dtype=jnp.int32)
    row = (n // GRID_W).astype(F32)
    col = (n % GRID_W).astype(F32)
    quarter = HEAD_DIM // 4
    inv = ROPE_THETA ** (-jnp.arange(quarter, dtype=F32) / quarter)
    lane = jnp.arange(LANE)
    pos = jnp.where((lane % HEAD_DIM < HEAD_DIM // 2)[None, :], row[:, None], col[:, None])
    ang = pos * inv[lane % quarter][None, :]
    first = (lane % (2 * quarter) < quarter)[None, :]
    cos = jnp.cos(ang)
    sin = jnp.sin(ang)
    sa = jnp.where(first, -sin, 0.0)
    sb = jnp.where(first, 0.0, sin)
    padc = lambda t, v: jnp.concatenate([t, jnp.full((CTX_LEN, LANE), v, F32)], axis=0)
    return padc(cos, 1.0), padc(sa, 0.0), padc(sb, 0.0)


def kernel(x, c, ctx, c_ctx, w_ada, b_ada, g_mix_pre, g_mix_post, g_ffn_pre, g_ffn_post, w_in, conv_w, conv_b,
           dt_bias, a_log, d_skip, g_ssd, attn_sink, w_out, w_router, w_gate, w_up, w_down):
    depth = w_ada.shape[0]
    bsz = x.shape[0]
    d = D_MODEL
    cc = jnp.concatenate([c, c_ctx[None, :], jnp.zeros((7, d), F32)], axis=0)[: bsz + 8]
    mod = _modulation(cc, w_ada, b_ada).reshape(depth, bsz + 8, 6, d)
    rope_tabs = _rope_tables()
    chan_tab = _channel_tables()
    cn_lat, sn_lat = _dft_tables(SEQ)
    cn_ctx, sn_ctx = _dft_tables(CTX_LEN)
    q_scale = jnp.concatenate([jnp.ones((F_WIDTH,), F32), jnp.full((ATT_WIDTH,), HEAD_DIM ** -0.5, F32),
                               jnp.ones((IN_WIDTH - F_WIDTH - ATT_WIDTH,), F32)])

    h = jnp.concatenate([x, ctx], axis=1)
    f_prev = None
    for l in range(depth):
        last = l == depth - 1
        w_in_p = jnp.pad(w_in[l] * q_scale[None, :], ((0, 0), (0, IN_PAD - IN_WIDTH))).astype(BF16)
        outs = _in_proj(h, mod[l], mod[l - 1] if l else None, g_mix_pre[l], w_in_p, rope_tabs,
                        f_prev, g_ffn_post[l - 1] if l else None)
        if f_prev is not None:
            h, outs = outs[0], outs[1:]
        fo, q, k, v, z, xbc, dt_raw = outs
        fmix = _fourier(fo, chan_tab, cn_lat, sn_lat, 0, SEQ)
        if not last:
            fmix = _fourier(fo, chan_tab, cn_ctx, sn_ctx, SEQ, CTX_LEN, fmix)
        att = _attention(q, k, v, attn_sink[l], not last)
        xconv = _conv_silu(xbc, conv_w[l], conv_b[l])
        y_f = _ssd_scan(xconv, dt_raw, dt_bias[l], a_log[l], False)
        y_b = _ssd_scan(xconv, dt_raw, dt_bias[l], a_log[l], True)
        n_rows = SEQ if last else S_ALL
        h, u_packed, logits_t = _out_proj(fmix, att, y_f, y_b, xconv, z, h, mod[l], d_skip[l], g_ssd[l],
                                          g_mix_post[l], g_ffn_pre[l], w_out[l], w_router[l], n_rows)
        f_prev = _moe(u_packed, logits_t, w_gate[l], w_up[l], w_down[l], not last)
    return _final_residual(h, f_prev, mod[depth - 1], g_ffn_post[depth - 1])
```

```python
import functools
import math

import jax
import jax.numpy as jnp
from jax import lax
from jax.experimental import pallas as pl
from jax.experimental.pallas import tpu as pltpu

F32 = jnp.float32
BF16 = jnp.bfloat16

D_MODEL = 1024
SEQ = 4096
CTX_LEN = 256
S_ALL = SEQ + CTX_LEN
GRID_W = 64
HEAD_DIM = 64
EPS = 1e-6
F_WIDTH = 256
ATT_HEADS = 6
ATT_KV_HEADS = 2
ATT_WIDTH = 384
KV_WIDTH = 128
BLOCK = 128
ROPE_THETA = 10000.0
SSD_HEADS = 6
SSD_WIDTH = 384
SSD_GROUPS = 2
SSD_STATE = 128
CHUNK = 128
CONV_K = 5
CONV_DIM = 896
IN_WIDTH = 2188
IN_PAD = 2304
N_EXPERTS = 16
EXPERT_FF = 2816
EC_CAPACITY = 2

LANE = 128
ROW_TILE = 256
VMEM_LIMIT = 56 * 1024 * 1024


def _cparams(sem, vmem=None):
    return pltpu.CompilerParams(dimension_semantics=sem, vmem_limit_bytes=vmem)


def _split_bf16(x, n):
    parts, r = [], x
    for _ in range(n):
        p = r.astype(BF16)
        parts.append(p)
        r = r - p.astype(F32)
    return parts


def _rms(x, g):
    return x * lax.rsqrt(jnp.mean(x * x, axis=-1, keepdims=True) + EPS) * g


def _silu(x):
    return x * jax.nn.sigmoid(x)


def _mod_kernel(cc_ref, w_ref, b_ref, o_ref):
    s = _silu(cc_ref[...])
    o_ref[0] = jnp.dot(s.astype(BF16), w_ref[0].astype(BF16), preferred_element_type=F32) + b_ref[0]


def _modulation(cc, w_ada, b_ada):
    n_l, d, d6 = w_ada.shape
    r = cc.shape[0]
    tn = 1024
    return pl.pallas_call(
        _mod_kernel,
        grid=(n_l, d6 // tn),
        in_specs=[pl.BlockSpec((r, d), lambda l, j: (0, 0)),
                  pl.BlockSpec((1, d, tn), lambda l, j: (l, 0, j)),
                  pl.BlockSpec((1, 1, tn), lambda l, j: (l, 0, j))],
        out_specs=pl.BlockSpec((1, r, tn), lambda l, j: (l, 0, j)),
        out_shape=jax.ShapeDtypeStruct((n_l, r, d6), F32),
        compiler_params=_cparams(("arbitrary", "arbitrary")),
        name="adaln_mod",
    )(cc, w_ada, b_ada.reshape(n_l, 1, d6))


def _stream_specs(h, tm):
    n_lat = SEQ // tm
    if isinstance(h, tuple):
        lat, ctx = h
        d = lat.shape[-1]
        return ([pl.BlockSpec((1, tm, d), lambda b, i: (b, jnp.minimum(i, n_lat - 1), 0)),
                 pl.BlockSpec((1, tm, d), lambda b, i: (b, jnp.maximum(i - n_lat, 0), 0))], [lat, ctx])
    return [pl.BlockSpec((1, tm, h.shape[-1]), lambda b, i: (b, i, 0))], [h]


def _stream_tile(refs, tm):
    if len(refs) == 2:
        return jnp.where(pl.program_id(1) < SEQ // tm, refs[0][0], refs[1][0])
    return refs[0][0]


def _in_kernel(*refs, residual, n_h):
    h_refs, refs = refs[:n_h], refs[n_h:]
    if residual:
        (f_ref, gpost_ref, mod_ref, g_ref, w_ref, cos_ref, sa_ref, sb_ref,
         hout_ref, fo_ref, q_ref, k_ref, v_ref, z_ref, xbc_ref, dt_ref) = refs
    else:
        (mod_ref, g_ref, w_ref, cos_ref, sa_ref, sb_ref,
         fo_ref, q_ref, k_ref, v_ref, z_ref, xbc_ref, dt_ref) = refs
    x = _stream_tile(h_refs, ROW_TILE)
    mod = mod_ref[0, 0]
    if residual:
        x = x + mod[5:6] * _rms(f_ref[0], gpost_ref[...])
        hout_ref[0] = x
    u = _rms(x, g_ref[...]) * (1.0 + mod[1:2]) + mod[0:1]
    p = jnp.dot(u.astype(BF16), w_ref[...], preferred_element_type=F32)
    cos, sa, sb = cos_ref[...], sa_ref[...], sb_ref[...]

    def rope(t):
        return t * cos + pltpu.roll(t, LANE - 16, 1) * sa + pltpu.roll(t, 16, 1) * sb

    fo_ref[0] = p[:, 0:256].astype(BF16)
    q_ref[0] = jnp.concatenate([rope(p[:, 256 + LANE * s:256 + LANE * (s + 1)]) for s in range(3)],
                               axis=1).astype(BF16)
    k_ref[0] = rope(p[:, 640:768]).astype(BF16)
    v_ref[0] = p[:, 768:896].astype(BF16)
    z_ref[0] = p[:, 896:1280].astype(BF16)
    xbc_ref[0] = p[:, 1280:2176].astype(BF16)
    dt_ref[0] = p[:, 2176:2304]


def _in_proj(h, mod_l, mod_prev, g_pre, w_in_p, rope_tabs, f_prev=None, g_post_prev=None):
    bsz = mod_l.shape[0] - 8
    s, d = S_ALL, D_MODEL
    tm = ROW_TILE
    n_lat = SEQ // tm
    residual = f_prev is not None
    row = lambda b, i: (b, i, 0)
    mod_map = lambda b, i: (jnp.where(i < n_lat, b, bsz), 0, 0, 0)
    const2 = lambda b, i: (0, 0)
    tab = pl.BlockSpec((tm, LANE), lambda b, i: (i, 0))
    in_specs, args = _stream_specs(h, tm)
    n_h = len(args)
    if residual:
        in_specs += [pl.BlockSpec((1, tm, d), row), pl.BlockSpec((1, d), const2)]
        args += [f_prev, g_post_prev.reshape(1, d)]
    mod_used = mod_l if not residual else jnp.concatenate([mod_l[:, :5], mod_prev[:, 5:6]], axis=1)
    in_specs += [pl.BlockSpec((1, 1, 6, d), mod_map), pl.BlockSpec((1, d), const2),
                 pl.BlockSpec((d, IN_PAD), const2), tab, tab, tab]
    args += [mod_used.reshape(mod_used.shape[0], 1, 6, d), g_pre.reshape(1, d), w_in_p, *rope_tabs]
    widths = [(256, BF16), (384, BF16), (128, BF16), (128, BF16), (384, BF16), (896, BF16), (128, F32)]
    out_specs = [pl.BlockSpec((1, tm, w), row) for w, _ in widths]
    out_shape = [jax.ShapeDtypeStruct((bsz, s, w), dt) for w, dt in widths]
    if residual:
        out_specs = [pl.BlockSpec((1, tm, d), row)] + out_specs
        out_shape = [jax.ShapeDtypeStruct((bsz, s, d), F32)] + out_shape
    return pl.pallas_call(
        functools.partial(_in_kernel, residual=residual, n_h=n_h),
        grid=(bsz, s // tm),
        in_specs=in_specs, out_specs=out_specs, out_shape=out_shape,
        compiler_params=_cparams(("parallel", "arbitrary"), VMEM_LIMIT),
        name="prenorm_inproj",
    )(*args)


def _f1_kernel(u_ref, w_ref, a_ref, b_ref):
    r = jnp.dot(u_ref[0], w_ref[...], preferred_element_type=F32)
    a_ref[...] = r[:, :F_WIDTH].astype(BF16)
    b_ref[...] = r[:, F_WIDTH:].astype(BF16)


def _f2_kernel(c_ref, s_ref, a_ref, b_ref, *rest):
    o_ref = rest[-1]
    y = (jnp.dot(c_ref[...], a_ref[...], preferred_element_type=F32)
         - jnp.dot(s_ref[...], b_ref[...], preferred_element_type=F32))
    o_ref[0] = y.astype(BF16)


def _dft_tables(n):
    k = jnp.arange(n, dtype=jnp.int32)
    m = (k[:, None] * k[None, :]) % n
    ang = m.astype(F32) * (2.0 * math.pi / n)
    sc = n ** -0.5
    return (jnp.cos(ang) * sc).astype(BF16), (jnp.sin(ang) * sc).astype(BF16)


def _channel_tables():
    c64, s64 = _dft_tables(HEAD_DIM)
    eye = jnp.eye(F_WIDTH // HEAD_DIM, dtype=BF16)
    return jnp.concatenate([jnp.kron(eye, c64), jnp.kron(eye, s64)], axis=1)


def _fourier(fo, chan_tab, cn, sn, row0, n, out_prev=None):
    bsz, s, _ = fo.shape
    tm = min(n, 512)
    nt = n // tm
    blk0 = row0 // tm
    a_all, b_all = pl.pallas_call(
        _f1_kernel,
        grid=(bsz, nt),
        in_specs=[pl.BlockSpec((1, tm, F_WIDTH), lambda b, i: (b, blk0 + i, 0)),
                  pl.BlockSpec((F_WIDTH, 2 * F_WIDTH), lambda b, i: (0, 0))],
        out_specs=[pl.BlockSpec((tm, F_WIDTH), lambda b, i: (i, b))] * 2,
        out_shape=[jax.ShapeDtypeStruct((n, bsz * F_WIDTH), BF16)] * 2,
        compiler_params=_cparams(("parallel", "arbitrary")),
        name="fourier_channels",
    )(fo, chan_tab)
    in_specs = [pl.BlockSpec((tm, n), lambda i, b: (i, 0)),
                pl.BlockSpec((tm, n), lambda i, b: (i, 0)),
                pl.BlockSpec((n, F_WIDTH), lambda i, b: (0, b)),
                pl.BlockSpec((n, F_WIDTH), lambda i, b: (0, b))]
    args = [cn, sn, a_all, b_all]
    aliases = {}
    if out_prev is not None:
        in_specs.append(pl.BlockSpec(memory_space=pl.ANY))
        args.append(out_prev)
        aliases = {4: 0}
    return pl.pallas_call(
        _f2_kernel,
        grid=(nt, bsz),
        in_specs=in_specs,
        out_specs=pl.BlockSpec((1, tm, F_WIDTH), lambda i, b: (b, blk0 + i, 0)),
        out_shape=jax.ShapeDtypeStruct((bsz, s, F_WIDTH), BF16),
        input_output_aliases=aliases,
        compiler_params=_cparams(("parallel", "arbitrary"), VMEM_LIMIT),
        name="fourier_positions",
    )(*args)


def _head_perm():
    r = ATT_HEADS // ATT_KV_HEADS
    p = jnp.arange(ATT_WIDTH)
    head = p // LANE + r * ((p % LANE) // HEAD_DIM)
    return head * HEAD_DIM + p % HEAD_DIM


def _attend(q, k, v, masks, sink_ref):
    n = q.shape[0]
    r = ATT_HEADS // ATT_KV_HEADS
    lo = lax.broadcasted_iota(jnp.int32, (1, LANE), 1) < HEAD_DIM
    rid = lax.broadcasted_iota(jnp.int32, (r * n, 1), 0)
    outs = []
    for g in range(ATT_KV_HEADS):
        keep = lo if g == 0 else jnp.logical_not(lo)
        qg = jnp.concatenate([jnp.where(keep, q[:, LANE * s:LANE * (s + 1)], jnp.zeros((), BF16))
                              for s in range(r)], axis=0)
        snk = jnp.full((r * n, 1), sink_ref[r * g], F32)
        for j in range(1, r):
            snk = jnp.where(rid >= j * n, sink_ref[r * g + j], snk)
        sc = lax.dot_general(qg, k, (((1,), (1,)), ((), ())), preferred_element_type=F32)
        if masks:
            nblk = sc.shape[1] // LANE
            sc = jnp.concatenate(
                [jnp.where(masks[b], sc[:, b * LANE:(b + 1) * LANE], -1e30) if b in masks
                 else sc[:, b * LANE:(b + 1) * LANE] for b in range(nblk)], axis=1)
        mx = jnp.maximum(jnp.max(sc, axis=-1, keepdims=True), snk)
        p = jnp.exp(sc - mx)
        den = jnp.sum(p, axis=-1, keepdims=True) + jnp.exp(snk - mx)
        outs.append(jnp.dot(p.astype(BF16), v, preferred_element_type=F32) / den)
    return jnp.concatenate([jnp.where(lo, outs[0][s * n:(s + 1) * n], outs[1][s * n:(s + 1) * n])
                            for s in range(r)], axis=1)


def _attn_lat_kernel(sink_ref, q_ref, kp_ref, kc_ref, kn_ref, kx_ref, vp_ref, vc_ref, vn_ref, vx_ref, o_ref):
    nb = pl.num_programs(1)
    n = pl.program_id(1)
    rows = (ATT_HEADS // ATT_KV_HEADS) * BLOCK
    i = lax.broadcasted_iota(jnp.int32, (rows, BLOCK), 0) & (BLOCK - 1)
    j = lax.broadcasted_iota(jnp.int32, (rows, BLOCK), 1)
    m_prev = (j >= i) & (n > 0)
    m_next = (j <= i) & (n < nb - 1)
    k = jnp.concatenate([kp_ref[0], kc_ref[0], kn_ref[0], kx_ref[0]], axis=0)
    v = jnp.concatenate([vp_ref[0], vc_ref[0], vn_ref[0], vx_ref[0]], axis=0)
    o_ref[0] = _attend(q_ref[0], k, v, {0: m_prev, 2: m_next}, sink_ref).astype(BF16)


def _attn_ctx_kernel(sink_ref, q_ref, kx_ref, vx_ref, prev_ref, o_ref):
    o_ref[0] = _attend(q_ref[0], kx_ref[0], vx_ref[0], {}, sink_ref).astype(BF16)


def _attention(q, k, v, sink, with_ctx):
    bsz, s, _ = q.shape
    nb = SEQ // BLOCK
    ctx_blk = SEQ // CTX_LEN
    smem = pl.BlockSpec(memory_space=pltpu.SMEM)
    kv = lambda f: pl.BlockSpec((1, BLOCK, KV_WIDTH), f)
    prev = lambda b, n: (b, jnp.maximum(n - 1, 0), 0)
    cur = lambda b, n: (b, n, 0)
    nxt = lambda b, n: (b, jnp.minimum(n + 1, nb - 1), 0)
    ctx = pl.BlockSpec((1, CTX_LEN, KV_WIDTH), lambda b, n: (b, ctx_blk, 0))
    sink8 = jnp.pad(sink, (0, 8 - ATT_HEADS))
    out = pl.pallas_call(
        _attn_lat_kernel,
        grid=(bsz, nb),
        in_specs=[smem, pl.BlockSpec((1, BLOCK, ATT_WIDTH), cur),
                  kv(prev), kv(cur), kv(nxt), ctx, kv(prev), kv(cur), kv(nxt), ctx],
        out_specs=pl.BlockSpec((1, BLOCK, ATT_WIDTH), cur),
        out_shape=jax.ShapeDtypeStruct((bsz, s, ATT_WIDTH), BF16),
        compiler_params=_cparams(("parallel", "arbitrary")),
        name="window_attention",
    )(sink8, q, k, k, k, k, v, v, v, v)
    if not with_ctx:
        return out
    cq = lambda b: (b, ctx_blk, 0)
    return pl.pallas_call(
        _attn_ctx_kernel,
        grid=(bsz,),
        in_specs=[smem, pl.BlockSpec((1, CTX_LEN, ATT_WIDTH), cq),
                  pl.BlockSpec((1, CTX_LEN, KV_WIDTH), cq), pl.BlockSpec((1, CTX_LEN, KV_WIDTH), cq),
                  pl.BlockSpec(memory_space=pl.ANY)],
        out_specs=pl.BlockSpec((1, CTX_LEN, ATT_WIDTH), cq),
        out_shape=jax.ShapeDtypeStruct((bsz, s, ATT_WIDTH), BF16),
        input_output_aliases={4: 0},
        compiler_params=_cparams(("parallel",)),
        name="context_attention",
    )(sink8, q, k, v, out)


HALO = 16
N_SHIFT = CONV_K - 1


def _shift_table():
    i = jnp.arange(CHUNK)[:, None]
    j = jnp.arange(CHUNK + 2 * HALO)[None, :]
    taps = [k for k in range(CONV_K) if k != CONV_K // 2]
    return jnp.concatenate([(j == i + k - CONV_K // 2 + HALO) for k in taps], axis=0).astype(BF16)


def _expand_table(lane0):
    src = jnp.arange(LANE)[:, None] - lane0
    return (src == (jnp.arange(SSD_WIDTH)[None, :] // HEAD_DIM)).astype(BF16)


def _conv_chunk(c, xp_ref, xc_ref, xn_ref, sh_ref, w_ref, b_ref):
    n_lat = SEQ // CHUNK
    last = S_ALL // CHUNK - 1
    has_prev = (c != 0) & (c != n_lat)
    has_next = (c != n_lat - 1) & (c != last)
    zero = jnp.zeros((), BF16)
    cur = xc_ref[0]
    xx = jnp.concatenate([jnp.where(has_prev, xp_ref[0], zero), cur, jnp.where(has_next, xn_ref[0], zero)], axis=0)
    sh = jnp.dot(sh_ref[...], xx, preferred_element_type=F32)
    mid = CONV_K // 2
    acc = b_ref[...] + w_ref[mid:mid + 1, :] * cur.astype(F32)
    for t in range(N_SHIFT):
        k = t if t < mid else t + 1
        acc = acc + w_ref[k:k + 1, :] * sh[t * CHUNK:(t + 1) * CHUNK]
    return _silu(acc)


def _ssd_chunk(xc, dt_raw, dtb, alog, ex_ref, dsk, state_ref, reverse):
    xs = xc[:, :SSD_WIDTH]
    gs = SSD_GROUPS * SSD_STATE
    bm = xc[:, SSD_WIDTH:SSD_WIDTH + gs].astype(BF16)
    cm = xc[:, SSD_WIDTH + gs:SSD_WIDTH + 2 * gs].astype(BF16)
    raw = dt_raw + dtb
    dt = jnp.maximum(raw, 0.0) + jnp.log(1.0 + jnp.exp(-jnp.abs(raw)))
    dta = dt * (-jnp.exp(alog))
    ri = lax.broadcasted_iota(jnp.int32, (CHUNK, CHUNK), 0)
    ci = lax.broadcasted_iota(jnp.int32, (CHUNK, CHUNK), 1)
    keep = (ri <= ci) if reverse else (ri >= ci)
    tri = keep.astype(BF16)
    tri_t = ((ri >= ci) if reverse else (ri <= ci)).astype(BF16)
    acs = sum(jnp.dot(tri, p, preferred_element_type=F32) for p in _split_bf16(dta, 3))
    acs_t = sum(jnp.dot(p, tri_t, preferred_element_type=F32) for p in _split_bf16(dta.T, 3))
    parts = jnp.concatenate(_split_bf16(dt, 2) + _split_bf16(acs, 3), axis=0)
    wide = jnp.dot(parts, ex_ref[...], preferred_element_type=F32)
    dt_x = wide[:CHUNK] + wide[CHUNK:2 * CHUNK]
    acs_x = wide[2 * CHUNK:3 * CHUNK] + wide[3 * CHUNK:4 * CHUNK] + wide[4 * CHUNK:]
    edge = 0 if reverse else CHUNK - 1
    tot_x = acs_x[edge:edge + 1, :]
    rh = SSD_HEADS // SSD_GROUPS
    gw = rh * HEAD_DIM
    lane0 = SSD_HEADS if reverse else 0
    xd = xs * dt_x
    xdw = (xd * jnp.exp(tot_x - acs_x)).astype(BF16)
    xdb = xd.astype(BF16)
    eacs = jnp.exp(acs_x)
    state = state_ref[...]
    ys = []
    new_state = []
    for g in range(SSD_GROUPS):
        bg = bm[:, g * SSD_STATE:(g + 1) * SSD_STATE]
        cg = cm[:, g * SSD_STATE:(g + 1) * SSD_STATE]
        cb = lax.dot_general(cg, bg, (((1,), (1,)), ((), ())), preferred_element_type=F32)
        sg = state[:, g * gw:(g + 1) * gw]
        y_off = jnp.dot(cg, sg.astype(BF16), preferred_element_type=F32) * eacs[:, g * gw:(g + 1) * gw]
        for r in range(rh):
            h = g * rh + r
            ln = lane0 + h
            dec = jnp.where(keep, jnp.exp(acs[:, ln:ln + 1] - acs_t[ln:ln + 1, :]), 0.0)
            yd = jnp.dot((cb * dec).astype(BF16), xdb[:, h * HEAD_DIM:(h + 1) * HEAD_DIM],
                         preferred_element_type=F32)
            ys.append(yd + y_off[:, r * HEAD_DIM:(r + 1) * HEAD_DIM])
        bt = bg.astype(F32).T.astype(BF16)
        new_state.append(jnp.dot(bt, xdw[:, g * gw:(g + 1) * gw], preferred_element_type=F32))
    state_ref[...] = state * jnp.exp(tot_x) + jnp.concatenate(new_state, axis=1)
    y = jnp.concatenate(ys, axis=1)
    return y if dsk is None else y + dsk * xs


def _ssd_kernel(fp_ref, fc_ref, fn_ref, fdt_ref, bp_ref, bc_ref, bn_ref, bdt_ref,
                sh_ref, w_ref, b_ref, dtb_ref, alog_ref, exf_ref, exb_ref, dsk_ref,
                yf_ref, yb_ref, sf_ref, sb_ref):
    step = pl.program_id(1)
    nc = pl.num_programs(1)

    @pl.when(step == 0)
    def _():
        sf_ref[...] = jnp.zeros_like(sf_ref)
        sb_ref[...] = jnp.zeros_like(sb_ref)

    cf = (step + SEQ // CHUNK) % nc
    cbk = nc - 1 - step
    xf = _conv_chunk(cf, fp_ref, fc_ref, fn_ref, sh_ref, w_ref, b_ref)
    xb = _conv_chunk(cbk, bp_ref, bc_ref, bn_ref, sh_ref, w_ref, b_ref)
    dtb, alog = dtb_ref[...], alog_ref[...]
    yf_ref[0] = _ssd_chunk(xf, fdt_ref[0], dtb, alog, exf_ref, dsk_ref[...], sf_ref, False).astype(BF16)
    yb_ref[0] = _ssd_chunk(xb, bdt_ref[0], dtb, alog, exb_ref, None, sb_ref, True).astype(BF16)


def _ssd_mixer(xbc, dt_raw, conv_w, conv_b, dt_bias, a_log, d_skip):
    bsz, s, cdim = xbc.shape
    nc = s // CHUNK
    n_lat = SEQ // CHUNK
    per = CHUNK // HALO
    nh = s // HALO
    fwd = lambda t: (t + n_lat) % nc
    bwd = lambda t: nc - 1 - t

    def chunk_specs(order):
        return [pl.BlockSpec((1, HALO, cdim), lambda b, t: (b, jnp.maximum(order(t) * per - 1, 0), 0)),
                pl.BlockSpec((1, CHUNK, cdim), lambda b, t: (b, order(t), 0)),
                pl.BlockSpec((1, HALO, cdim), lambda b, t: (b, jnp.minimum(order(t) * per + per, nh - 1), 0)),
                pl.BlockSpec((1, CHUNK, LANE), lambda b, t: (b, order(t), 0))]

    c2 = lambda b, t: (0, 0)
    full = lambda a: pl.BlockSpec(a.shape, c2)
    pad = lambda p: jnp.pad(p.reshape(1, -1), ((0, 0), (0, LANE - 2 * SSD_HEADS)))
    consts = [_shift_table(), jnp.pad(conv_w, ((0, 8 - CONV_K), (0, 0))), conv_b.reshape(1, cdim),
              pad(dt_bias), pad(a_log), _expand_table(0), _expand_table(SSD_HEADS),
              jnp.repeat(d_skip, HEAD_DIM).reshape(1, SSD_WIDTH)]
    return pl.pallas_call(
        _ssd_kernel,
        grid=(bsz, nc),
        in_specs=chunk_specs(fwd) + chunk_specs(bwd) + [full(a) for a in consts],
        out_specs=[pl.BlockSpec((1, CHUNK, SSD_WIDTH), lambda b, t: (b, fwd(t), 0)),
                   pl.BlockSpec((1, CHUNK, SSD_WIDTH), lambda b, t: (b, bwd(t), 0))],
        out_shape=[jax.ShapeDtypeStruct((bsz, s, SSD_WIDTH), BF16)] * 2,
        scratch_shapes=[pltpu.VMEM((SSD_STATE, SSD_WIDTH), F32)] * 2,
        compiler_params=_cparams(("parallel", "arbitrary")),
        name="conv_ssd_scan",
    )(xbc, xbc, xbc, dt_raw, xbc, xbc, xbc, dt_raw, *consts)


def _out_kernel(*refs, n_h):
    (fo_ref, at_ref, yf_ref, yb_ref, z_ref), refs = refs[:5], refs[5:]
    h_refs, refs = refs[:n_h], refs[n_h:]
    mod_ref, gssd_ref, gpost_ref, gpre_ref, wo_ref, wr_ref, hout_ref, u_ref, lg_ref = refs
    y = (yf_ref[0].astype(F32) + yb_ref[0].astype(F32)) * _silu(z_ref[0].astype(F32))
    sm = _rms(y, gssd_ref[...]).astype(BF16)
    m = (jnp.dot(fo_ref[0], wo_ref[0:256, :], preferred_element_type=F32)
         + jnp.dot(at_ref[0], wo_ref[256:640, :], preferred_element_type=F32)
         + jnp.dot(sm, wo_ref[640:1024, :], preferred_element_type=F32))
    mod = mod_ref[0, 0]
    hn = _stream_tile(h_refs, ROW_TILE) + mod[2:3] * _rms(m, gpost_ref[...])
    hout_ref[0] = hn
    u = _rms(hn, gpre_ref[...]) * (1.0 + mod[4:5]) + mod[3:4]
    half = D_MODEL // 2
    lo = pltpu.bitcast(u[:, :half].astype(BF16).astype(F32), jnp.uint32) >> 16
    hi = pltpu.bitcast(u[:, half:].astype(BF16).astype(F32), jnp.uint32) & jnp.uint32(0xFFFF0000)
    u_ref[0] = lo | hi
    uh, ul = _split_bf16(u, 2)
    wh, wl = wr_ref[0], wr_ref[1]
    nt = (((1,), (1,)), ((), ()))
    lg_ref[0] = (lax.dot_general(wh, uh, nt, preferred_element_type=F32)
                 + lax.dot_general(wh, ul, nt, preferred_element_type=F32)
                 + lax.dot_general(wl, uh, nt, preferred_element_type=F32))


def _out_proj(fo, at, yf, yb, z, h, mod_l, g_ssd, g_post, g_pre, w_out_p, w_router, n_rows):
    bsz = fo.shape[0]
    d = D_MODEL
    tm = ROW_TILE
    n_lat = SEQ // tm
    row = lambda b, i: (b, i, 0)
    c2 = lambda b, i: (0, 0)
    wr = jnp.stack(_split_bf16(w_router.T, 2))
    h_specs, h_args = _stream_specs(h, tm)
    return pl.pallas_call(
        functools.partial(_out_kernel, n_h=len(h_args)),
        grid=(bsz, n_rows // tm),
        in_specs=[pl.BlockSpec((1, tm, F_WIDTH), row), pl.BlockSpec((1, tm, ATT_WIDTH), row),
                  pl.BlockSpec((1, tm, SSD_WIDTH), row), pl.BlockSpec((1, tm, SSD_WIDTH), row),
                  pl.BlockSpec((1, tm, SSD_WIDTH), row)] + h_specs + [
                  pl.BlockSpec((1, 1, 6, d), lambda b, i: (jnp.where(i < n_lat, b, bsz), 0, 0, 0)),
                  pl.BlockSpec((1, SSD_WIDTH), c2), pl.BlockSpec((1, d), c2), pl.BlockSpec((1, d), c2),
                  pl.BlockSpec((d, d), c2), pl.BlockSpec((2, N_EXPERTS, d), lambda b, i: (0, 0, 0))],
        out_specs=[pl.BlockSpec((1, tm, d), row), pl.BlockSpec((1, tm, d // 2), row),
                   pl.BlockSpec((1, N_EXPERTS, tm), lambda b, i: (b, 0, i))],
        out_shape=[jax.ShapeDtypeStruct((bsz, n_rows, d), F32),
                   jax.ShapeDtypeStruct((bsz, n_rows, d // 2), jnp.uint32),
                   jax.ShapeDtypeStruct((bsz, N_EXPERTS, n_rows), F32)],
        compiler_params=_cparams(("parallel", "arbitrary"), VMEM_LIMIT),
        name="outproj_norms_router",
    )(fo, at, yf, yb, z, *h_args, mod_l.reshape(mod_l.shape[0], 1, 6, d), g_ssd.reshape(1, SSD_WIDTH),
      g_post.reshape(1, d), g_pre.reshape(1, d), w_out_p, wr)


def _prefix_lanes(mask, upper):
    t = mask.shape[1]
    carry = jnp.zeros((mask.shape[0], 1), F32)
    outs = []
    for j in range(t // LANE):
        pre = jnp.dot(mask[:, j * LANE:(j + 1) * LANE].astype(BF16), upper, preferred_element_type=F32) + carry
        outs.append(pre)
        carry = pre[:, LANE - 1:LANE]
    return jnp.concatenate(outs, axis=1)


def _route_kernel(lg_ref, idx_ref, aff_ref, *, cap):
    l = lg_ref[0]
    n_e, t = l.shape
    e = jnp.exp(l - jnp.max(l, axis=0, keepdims=True))
    aff = e / jnp.sum(e, axis=0, keepdims=True)
    aff_ref[0] = aff
    keys = pltpu.bitcast(aff, jnp.int32)

    def search(it, lo):
        cand = lo | (jnp.int32(1) << (30 - it))
        cnt = jnp.sum((keys >= cand).astype(F32), axis=1, keepdims=True)
        return jnp.where(cnt >= cap, cand, lo)

    thr = lax.fori_loop(0, 31, search, jnp.zeros((n_e, 1), jnp.int32))
    gt = keys > thr
    eq = keys == thr
    need = cap - jnp.sum(gt.astype(F32), axis=1, keepdims=True)
    ri = lax.broadcasted_iota(jnp.int32, (LANE, LANE), 0)
    ci = lax.broadcasted_iota(jnp.int32, (LANE, LANE), 1)
    upper = (ri <= ci).astype(BF16)
    eq_rank = _prefix_lanes(eq.astype(F32), upper)
    sel = gt | (eq & (eq_rank <= need))
    csum = _prefix_lanes(sel.astype(F32), upper)
    csum_t = jnp.concatenate([csum, jnp.zeros((LANE - n_e, t), F32)], axis=0).T
    slot = lax.broadcasted_iota(jnp.int32, (1, cap), 1).astype(F32)
    tc = min(t, 512)
    rows = []
    for x in range(n_e):
        cnt = jnp.zeros((1, cap), F32)
        for j in range(t // tc):
            below = csum_t[j * tc:(j + 1) * tc, x:x + 1] <= slot
            cnt = cnt + jnp.sum(below.astype(F32), axis=0, keepdims=True)
        rows.append(cnt)
    idx_ref[0] = jnp.concatenate(rows, axis=0).astype(jnp.int32)


def _route(logits_t, row0, t, cap):
    bsz, n_e, _ = logits_t.shape
    return pl.pallas_call(
        functools.partial(_route_kernel, cap=cap),
        grid=(bsz,),
        in_specs=[pl.BlockSpec((1, n_e, t), lambda b: (b, 0, row0 // t))],
        out_specs=[pl.BlockSpec((1, n_e, cap), lambda b: (b, 0, 0)),
                   pl.BlockSpec((1, n_e, t), lambda b: (b, 0, 0))],
        out_shape=[jax.ShapeDtypeStruct((bsz, n_e, cap), jnp.int32),
                   jax.ShapeDtypeStruct((bsz, n_e, t), F32)],
        compiler_params=_cparams(("parallel",), VMEM_LIMIT),
        name="expert_choice_route",
    )(logits_t)


def _smem_row(n):
    return pl.BlockSpec((1, 1, 1, n), lambda b, e: (b, e, 0, 0), memory_space=pltpu.SMEM)


def _gather_kernel(idx_ref, u_ref, o_ref, *, cap):
    def body(i, carry):
        t = idx_ref[0, 0, 0, i]
        o_ref[0, pl.ds(i, 1), :] = u_ref[0, pl.ds(t, 1), :]
        return carry

    lax.fori_loop(0, cap, body, 0, unroll=8)


def _gather(u_packed, idx, row0, t, cap):
    bsz, _, half = u_packed.shape
    n_e = idx.shape[1]
    return pl.pallas_call(
        functools.partial(_gather_kernel, cap=cap),
        grid=(bsz, n_e),
        in_specs=[_smem_row(cap), pl.BlockSpec((1, t, half), lambda b, e: (b, row0 // t, 0))],
        out_specs=pl.BlockSpec((1, cap, half), lambda b, e: (e, b, 0)),
        out_shape=jax.ShapeDtypeStruct((n_e, bsz * cap, half), jnp.uint32),
        compiler_params=_cparams(("parallel", "arbitrary"), VMEM_LIMIT),
        name="expert_gather",
    )(idx[:, :, None, :], u_packed)


FF_TILE = 256
OUT_TILE = 256


def _ffn_kernel(x_ref, wg_ref, wu_ref, wd_ref, o_ref, xs_ref, hm_ref, tmp_ref, *, n_up):
    j = pl.program_id(2)

    @pl.when(j == 0)
    def _():
        w = x_ref[0]
        half = w.shape[1]
        xs_ref[:, :half] = pltpu.bitcast(w << 16, F32).astype(BF16)
        xs_ref[:, half:] = pltpu.bitcast(w & jnp.uint32(0xFFFF0000), F32).astype(BF16)

    @pl.when(j < n_up)
    def _():
        x = xs_ref[...]
        g = jnp.dot(x, wg_ref[0, 0].astype(BF16), preferred_element_type=F32)
        u = jnp.dot(x, wu_ref[0, 0].astype(BF16), preferred_element_type=F32)
        tmp_ref[...] = (_silu(g) * u).astype(BF16)

    for f in range(n_up):
        @pl.when(j == f)
        def _(f=f):
            hm_ref[:, f * FF_TILE:(f + 1) * FF_TILE] = tmp_ref[...]

    @pl.when(j >= n_up)
    def _():
        o_ref[0] = jnp.dot(hm_ref[...], wd_ref[0, 0].astype(BF16), preferred_element_type=F32).astype(BF16)


def _expert_ffn(xe, w_gate, w_up, w_down, layer):
    n_e, m, half = xe.shape
    d = 2 * half
    ff = w_gate.shape[-1]
    tm = min(m, 2048)
    n_up = ff // FF_TILE
    up = lambda e, i, j: (layer, e, 0, jnp.minimum(j, n_up - 1))
    down = lambda j: jnp.maximum(j - n_up, 0)
    return pl.pallas_call(
        functools.partial(_ffn_kernel, n_up=n_up),
        grid=(n_e, m // tm, n_up + d // OUT_TILE),
        in_specs=[pl.BlockSpec((1, tm, half), lambda e, i, j: (e, i, 0)),
                  pl.BlockSpec((1, 1, d, FF_TILE), up),
                  pl.BlockSpec((1, 1, d, FF_TILE), up),
                  pl.BlockSpec((1, 1, ff, OUT_TILE), lambda e, i, j: (layer, e, 0, down(j)))],
        out_specs=pl.BlockSpec((1, tm, OUT_TILE), lambda e, i, j: (e, i, down(j))),
        out_shape=jax.ShapeDtypeStruct((n_e, m, d), BF16),
        scratch_shapes=[pltpu.VMEM((tm, d), BF16), pltpu.VMEM((tm, ff), BF16), pltpu.VMEM((tm, FF_TILE), BF16)],
        compiler_params=_cparams(("parallel", "parallel", "arbitrary"), VMEM_LIMIT),
        name="expert_ffn",
    )(xe, w_gate, w_up, w_down)


COMB_ROWS = 16


def _combine_kernel(idx_ref, aff_ref, y_ref, *rest, cap):
    o_ref = rest[-1]

    @pl.when(pl.program_id(1) == 0)
    def _():
        o_ref[...] = jnp.zeros_like(o_ref)

    def body(c, carry):
        base = pl.multiple_of(c * COMB_ROWS, COMB_ROWS)
        tile = y_ref[0, pl.ds(base, COMB_ROWS), :].astype(F32)
        for r in range(COMB_ROWS):
            t = idx_ref[0, 0, 0, base + r]
            o_ref[0, pl.ds(t, 1), :] += aff_ref[0, 0, 0, t] * tile[r:r + 1, :]
        return carry

    lax.fori_loop(0, cap // COMB_ROWS, body, 0)


def _combine(y, idx, aff, row0, t, cap, s_total, out_prev=None):
    n_e, m, d = y.shape
    bsz = idx.shape[0]
    in_specs = [_smem_row(cap), _smem_row(t), pl.BlockSpec((1, cap, d), lambda b, e: (e, b, 0))]
    args = [idx[:, :, None, :], aff[:, :, None, :], y]
    aliases = {}
    if out_prev is not None:
        in_specs.append(pl.BlockSpec(memory_space=pl.ANY))
        args.append(out_prev)
        aliases = {3: 0}
    return pl.pallas_call(
        functools.partial(_combine_kernel, cap=cap),
        grid=(bsz, n_e),
        in_specs=in_specs,
        out_specs=pl.BlockSpec((1, t, d), lambda b, e: (b, row0 // t, 0)),
        out_shape=jax.ShapeDtypeStruct((bsz, s_total, d), F32),
        input_output_aliases=aliases,
        compiler_params=_cparams(("parallel", "arbitrary"), VMEM_LIMIT),
        name="expert_combine",
    )(*args)


def _moe(u_packed, logits_t, w_gate, w_up, w_down, layer, with_ctx):
    bsz, rows, _ = u_packed.shape
    out = None
    sets = [(0, SEQ)] + ([(SEQ, CTX_LEN)] if with_ctx else [])
    for row0, t in sets:
        cap = EC_CAPACITY * t // N_EXPERTS
        idx, aff = _route(logits_t, row0, t, cap)
        xe = _gather(u_packed, idx, row0, t, cap)
        y = _expert_ffn(xe, w_gate, w_up, w_down, layer)
        out = _combine(y, idx, aff, row0, t, cap, rows, out)
    return out


def _final_kernel(h_ref, f_ref, mod_ref, g_ref, o_ref):
    o_ref[0] = h_ref[0] + mod_ref[0, 0][5:6] * _rms(f_ref[0], g_ref[...])


def _final_residual(h, f, mod_l, g_post):
    bsz, n, d = h.shape
    tm = ROW_TILE
    row = lambda b, i: (b, i, 0)
    return pl.pallas_call(
        _final_kernel,
        grid=(bsz, n // tm),
        in_specs=[pl.BlockSpec((1, tm, d), row), pl.BlockSpec((1, tm, d), row),
                  pl.BlockSpec((1, 1, 6, d), lambda b, i: (b, 0, 0, 0)), pl.BlockSpec((1, d), lambda b, i: (0, 0))],
        out_specs=pl.BlockSpec((1, tm, d), row),
        out_shape=jax.ShapeDtypeStruct((bsz, n, d), F32),
        compiler_params=_cparams(("parallel", "arbitrary")),
        name="ffn_residual",
    )(h, f, mod_l.reshape(mod_l.shape[0], 1, 6, d), g_post.reshape(1, d))


def _rope_tables():
    n = jnp.arange(SEQ, dtype=jnp.int32)
    row = (n // GRID_W).astype(F32)
    col = (n % GRID_W).astype(F32)
    quarter = HEAD_DIM // 4
    inv = ROPE_THETA ** (-jnp.arange(quarter, dtype=F32) / quarter)
    lane = jnp.arange(LANE)
    pos = jnp.where((lane % HEAD_DIM < HEAD_DIM // 2)[None, :], row[:, None], col[:, None])
    ang = pos * inv[lane % quarter][None, :]
    first = (lane % (2 * quarter) < quarter)[None, :]
    cos = jnp.cos(ang)
    sin = jnp.sin(ang)
    sa = jnp.where(first, -sin, 0.0)
    sb = jnp.where(first, 0.0, sin)
    padc = lambda t, v: jnp.concatenate([t, jnp.full((CTX_LEN, LANE), v, F32)], axis=0)
    return padc(cos, 1.0), padc(sa, 0.0), padc(sb, 0.0)


def kernel(x, c, ctx, c_ctx, w_ada, b_ada, g_mix_pre, g_mix_post, g_ffn_pre, g_ffn_post, w_in, conv_w, conv_b,
           dt_bias, a_log, d_skip, g_ssd, attn_sink, w_out, w_router, w_gate, w_up, w_down):
    depth = w_ada.shape[0]
    bsz = x.shape[0]
    d = D_MODEL
    cc = jnp.concatenate([c, c_ctx[None, :], jnp.zeros((7, d), F32)], axis=0)
    mod = _modulation(cc, w_ada, b_ada).reshape(depth, bsz + 8, 6, d)
    rope_tabs = _rope_tables()
    chan_tab = _channel_tables()
    cn_lat, sn_lat = _dft_tables(SEQ)
    cn_ctx, sn_ctx = _dft_tables(CTX_LEN)
    perm = _head_perm()
    q0 = F_WIDTH

    h = (x, ctx)
    f_prev = None
    for l in range(depth):
        last = l == depth - 1
        w_q = jnp.take(w_in[l][:, q0:q0 + ATT_WIDTH], perm, axis=1) * HEAD_DIM ** -0.5
        w_in_p = jnp.concatenate([w_in[l][:, :q0], w_q, w_in[l][:, q0 + ATT_WIDTH:],
                                  jnp.zeros((d, IN_PAD - IN_WIDTH), F32)], axis=1).astype(BF16)
        w_out_p = jnp.concatenate([w_out[l][:q0], jnp.take(w_out[l][q0:q0 + ATT_WIDTH], perm, axis=0),
                                   w_out[l][q0 + ATT_WIDTH:]], axis=0).astype(BF16)
        outs = _in_proj(h, mod[l], mod[l - 1] if l else None, g_mix_pre[l], w_in_p, rope_tabs,
                        f_prev, g_ffn_post[l - 1] if l else None)
        if f_prev is not None:
            h, outs = outs[0], outs[1:]
        fo, q, k, v, z, xbc, dt_raw = outs
        fmix = _fourier(fo, chan_tab, cn_lat, sn_lat, 0, SEQ)
        if not last:
            fmix = _fourier(fo, chan_tab, cn_ctx, sn_ctx, SEQ, CTX_LEN, fmix)
        att = _attention(q, k, v, attn_sink[l], not last)
        y_f, y_b = _ssd_mixer(xbc, dt_raw, conv_w[l], conv_b[l], dt_bias[l], a_log[l], d_skip[l])
        n_rows = SEQ if last else S_ALL
        h, u_packed, logits_t = _out_proj(fmix, att, y_f, y_b, z, h, mod[l], g_ssd[l],
                                          g_mix_post[l], g_ffn_pre[l], w_out_p, w_router[l], n_rows)
        f_prev = _moe(u_packed, logits_t, w_gate, w_up, w_down, l, not last)
    return _final_residual(h, f_prev, mod[depth - 1], g_ffn_post[depth - 1])
```

```python
import functools
import math

import jax
import jax.numpy as jnp
from jax import lax
from jax.experimental import pallas as pl
from jax.experimental.pallas import tpu as pltpu

F32 = jnp.float32
BF16 = jnp.bfloat16

D_MODEL = 1024
SEQ = 4096
CTX_LEN = 256
S_ALL = SEQ + CTX_LEN
GRID_W = 64
HEAD_DIM = 64
EPS = 1e-6
F_WIDTH = 256
ATT_HEADS = 6
ATT_KV_HEADS = 2
ATT_WIDTH = 384
KV_WIDTH = 128
BLOCK = 128
ROPE_THETA = 10000.0
SSD_HEADS = 6
SSD_WIDTH = 384
SSD_GROUPS = 2
SSD_STATE = 128
CHUNK = 128
CONV_K = 5
CONV_DIM = 896
IN_WIDTH = 2188
IN_PAD = 2304
N_EXPERTS = 16
EXPERT_FF = 2816
EC_CAPACITY = 2

LANE = 128
ROW_TILE = 256
VMEM_LIMIT = 56 * 1024 * 1024


def _cparams(sem, vmem=None):
    return pltpu.CompilerParams(dimension_semantics=sem, vmem_limit_bytes=vmem)


def _split_bf16(x, n):
    parts, r = [], x
    for _ in range(n):
        p = r.astype(BF16)
        parts.append(p)
        r = r - p.astype(F32)
    return parts


def _rms(x, g):
    return x * lax.rsqrt(jnp.mean(x * x, axis=-1, keepdims=True) + EPS) * g


def _silu(x):
    return x * jax.nn.sigmoid(x)


def _mod_kernel(cc_ref, w_ref, b_ref, o_ref):
    s = _silu(cc_ref[...])
    o_ref[0] = jnp.dot(s.astype(BF16), w_ref[0].astype(BF16), preferred_element_type=F32) + b_ref[0]


def _modulation(cc, w_ada, b_ada):
    n_l, d, d6 = w_ada.shape
    r = cc.shape[0]
    tn = 1024
    return pl.pallas_call(
        _mod_kernel,
        grid=(n_l, d6 // tn),
        in_specs=[pl.BlockSpec((r, d), lambda l, j: (0, 0)),
                  pl.BlockSpec((1, d, tn), lambda l, j: (l, 0, j)),
                  pl.BlockSpec((1, 1, tn), lambda l, j: (l, 0, j))],
        out_specs=pl.BlockSpec((1, r, tn), lambda l, j: (l, 0, j)),
        out_shape=jax.ShapeDtypeStruct((n_l, r, d6), F32),
        compiler_params=_cparams(("arbitrary", "arbitrary")),
        name="adaln_mod",
    )(cc, w_ada, b_ada.reshape(n_l, 1, d6))


def _stream_specs(h, tm):
    n_lat = SEQ // tm
    if isinstance(h, tuple):
        lat, ctx = h
        d = lat.shape[-1]
        return ([pl.BlockSpec((1, tm, d), lambda b, i: (b, jnp.minimum(i, n_lat - 1), 0)),
                 pl.BlockSpec((1, tm, d), lambda b, i: (b, jnp.maximum(i - n_lat, 0), 0))], [lat, ctx])
    return [pl.BlockSpec((1, tm, h.shape[-1]), lambda b, i: (b, i, 0))], [h]


def _stream_tile(refs, tm):
    if len(refs) == 2:
        return jnp.where(pl.program_id(1) < SEQ // tm, refs[0][0], refs[1][0])
    return refs[0][0]


def _in_kernel(*refs, residual, n_h):
    h_refs, refs = refs[:n_h], refs[n_h:]
    if residual:
        (f_ref, gpost_ref, mod_ref, g_ref, w_ref, cos_ref, sa_ref, sb_ref,
         hout_ref, fo_ref, q_ref, k_ref, v_ref, z_ref, xbc_ref, dt_ref) = refs
    else:
        (mod_ref, g_ref, w_ref, cos_ref, sa_ref, sb_ref,
         fo_ref, q_ref, k_ref, v_ref, z_ref, xbc_ref, dt_ref) = refs
    x = _stream_tile(h_refs, ROW_TILE)
    mod = mod_ref[0, 0]
    if residual:
        x = x + mod[5:6] * _rms(f_ref[0], gpost_ref[...])
        hout_ref[0] = x
    u = _rms(x, g_ref[...]) * (1.0 + mod[1:2]) + mod[0:1]
    p = jnp.dot(u.astype(BF16), w_ref[...], preferred_element_type=F32)
    cos, sa, sb = cos_ref[...], sa_ref[...], sb_ref[...]

    def rope(t):
        return t * cos + pltpu.roll(t, LANE - 16, 1) * sa + pltpu.roll(t, 16, 1) * sb

    fo_ref[0] = p[:, 0:256].astype(BF16)
    q_ref[0] = jnp.concatenate([rope(p[:, 256 + LANE * s:256 + LANE * (s + 1)]) for s in range(3)],
                               axis=1).astype(BF16)
    k_ref[0] = rope(p[:, 640:768]).astype(BF16)
    v_ref[0] = p[:, 768:896].astype(BF16)
    z_ref[0] = p[:, 896:1280].astype(BF16)
    xbc_ref[0] = p[:, 1280:2176].astype(BF16)
    dt_ref[0] = p[:, 2176:2304]


def _in_proj(h, mod_l, mod_prev, g_pre, w_in_p, rope_tabs, f_prev=None, g_post_prev=None):
    bsz = mod_l.shape[0] - 8
    s, d = S_ALL, D_MODEL
    tm = ROW_TILE
    n_lat = SEQ // tm
    residual = f_prev is not None
    row = lambda b, i: (b, i, 0)
    mod_map = lambda b, i: (jnp.where(i < n_lat, b, bsz), 0, 0, 0)
    const2 = lambda b, i: (0, 0)
    tab = pl.BlockSpec((tm, LANE), lambda b, i: (i, 0))
    in_specs, args = _stream_specs(h, tm)
    n_h = len(args)
    if residual:
        in_specs += [pl.BlockSpec((1, tm, d), row), pl.BlockSpec((1, d), const2)]
        args += [f_prev, g_post_prev.reshape(1, d)]
    mod_used = mod_l if not residual else jnp.concatenate([mod_l[:, :5], mod_prev[:, 5:6]], axis=1)
    in_specs += [pl.BlockSpec((1, 1, 6, d), mod_map), pl.BlockSpec((1, d), const2),
                 pl.BlockSpec((d, IN_PAD), const2), tab, tab, tab]
    args += [mod_used.reshape(mod_used.shape[0], 1, 6, d), g_pre.reshape(1, d), w_in_p, *rope_tabs]
    widths = [(256, BF16), (384, BF16), (128, BF16), (128, BF16), (384, BF16), (896, BF16), (128, F32)]
    out_specs = [pl.BlockSpec((1, tm, w), row) for w, _ in widths]
    out_shape = [jax.ShapeDtypeStruct((bsz, s, w), dt) for w, dt in widths]
    if residual:
        out_specs = [pl.BlockSpec((1, tm, d), row)] + out_specs
        out_shape = [jax.ShapeDtypeStruct((bsz, s, d), F32)] + out_shape
    return pl.pallas_call(
        functools.partial(_in_kernel, residual=residual, n_h=n_h),
        grid=(bsz, s // tm),
        in_specs=in_specs, out_specs=out_specs, out_shape=out_shape,
        compiler_params=_cparams(("parallel", "arbitrary"), VMEM_LIMIT),
        name="prenorm_inproj",
    )(*args)


def _f1_kernel(u_ref, w_ref, a_ref, b_ref):
    r = jnp.dot(u_ref[0], w_ref[...], preferred_element_type=F32)
    a_ref[...] = r[:, :F_WIDTH].astype(BF16)
    b_ref[...] = r[:, F_WIDTH:].astype(BF16)


def _f2_kernel(c_ref, s_ref, a_ref, b_ref, *rest):
    o_ref = rest[-1]
    y = (jnp.dot(c_ref[...], a_ref[...], preferred_element_type=F32)
         - jnp.dot(s_ref[...], b_ref[...], preferred_element_type=F32))
    o_ref[0] = y.astype(BF16)


def _dft_tables(n):
    k = jnp.arange(n, dtype=jnp.int32)
    m = (k[:, None] * k[None, :]) % n
    ang = m.astype(F32) * (2.0 * math.pi / n)
    sc = n ** -0.5
    return (jnp.cos(ang) * sc).astype(BF16), (jnp.sin(ang) * sc).astype(BF16)


def _channel_tables():
    c64, s64 = _dft_tables(HEAD_DIM)
    eye = jnp.eye(F_WIDTH // HEAD_DIM, dtype=BF16)
    return jnp.concatenate([jnp.kron(eye, c64), jnp.kron(eye, s64)], axis=1)


def _fourier(fo, chan_tab, cn, sn, row0, n, out_prev=None):
    bsz, s, _ = fo.shape
    tm = min(n, 512)
    nt = n // tm
    blk0 = row0 // tm
    a_all, b_all = pl.pallas_call(
        _f1_kernel,
        grid=(bsz, nt),
        in_specs=[pl.BlockSpec((1, tm, F_WIDTH), lambda b, i: (b, blk0 + i, 0)),
                  pl.BlockSpec((F_WIDTH, 2 * F_WIDTH), lambda b, i: (0, 0))],
        out_specs=[pl.BlockSpec((tm, F_WIDTH), lambda b, i: (i, b))] * 2,
        out_shape=[jax.ShapeDtypeStruct((n, bsz * F_WIDTH), BF16)] * 2,
        compiler_params=_cparams(("parallel", "arbitrary")),
        name="fourier_channels",
    )(fo, chan_tab)
    in_specs = [pl.BlockSpec((tm, n), lambda i, b: (i, 0)),
                pl.BlockSpec((tm, n), lambda i, b: (i, 0)),
                pl.BlockSpec((n, F_WIDTH), lambda i, b: (0, b)),
                pl.BlockSpec((n, F_WIDTH), lambda i, b: (0, b))]
    args = [cn, sn, a_all, b_all]
    aliases = {}
    if out_prev is not None:
        in_specs.append(pl.BlockSpec(memory_space=pl.ANY))
        args.append(out_prev)
        aliases = {4: 0}
    return pl.pallas_call(
        _f2_kernel,
        grid=(nt, bsz),
        in_specs=in_specs,
        out_specs=pl.BlockSpec((1, tm, F_WIDTH), lambda i, b: (b, blk0 + i, 0)),
        out_shape=jax.ShapeDtypeStruct((bsz, s, F_WIDTH), BF16),
        input_output_aliases=aliases,
        compiler_params=_cparams(("parallel", "arbitrary"), VMEM_LIMIT),
        name="fourier_positions",
    )(*args)


def _to_slab_order(w, axis):
    r = ATT_HEADS // ATT_KV_HEADS
    shape = w.shape
    w = w.reshape(shape[:axis] + (ATT_KV_HEADS, r, HEAD_DIM) + shape[axis + 1:])
    return jnp.swapaxes(w, axis, axis + 1).reshape(shape)


def _attend(q, k, v, masks, sink_ref):
    n = q.shape[0]
    r = ATT_HEADS // ATT_KV_HEADS
    lo = lax.broadcasted_iota(jnp.int32, (1, LANE), 1) < HEAD_DIM
    rid = lax.broadcasted_iota(jnp.int32, (r * n, 1), 0)
    outs = []
    for g in range(ATT_KV_HEADS):
        keep = lo if g == 0 else jnp.logical_not(lo)
        qg = jnp.concatenate([jnp.where(keep, q[:, LANE * s:LANE * (s + 1)], jnp.zeros((), BF16))
                              for s in range(r)], axis=0)
        snk = jnp.full((r * n, 1), sink_ref[r * g], F32)
        for j in range(1, r):
            snk = jnp.where(rid >= j * n, sink_ref[r * g + j], snk)
        sc = lax.dot_general(qg, k, (((1,), (1,)), ((), ())), preferred_element_type=F32)
        if masks:
            nblk = sc.shape[1] // LANE
            sc = jnp.concatenate(
                [jnp.where(masks[b], sc[:, b * LANE:(b + 1) * LANE], -1e30) if b in masks
                 else sc[:, b * LANE:(b + 1) * LANE] for b in range(nblk)], axis=1)
        mx = jnp.maximum(jnp.max(sc, axis=-1, keepdims=True), snk)
        p = jnp.exp(sc - mx)
        den = jnp.sum(p, axis=-1, keepdims=True) + jnp.exp(snk - mx)
        outs.append(jnp.dot(p.astype(BF16), v, preferred_element_type=F32) / den)
    return jnp.concatenate([jnp.where(lo, outs[0][s * n:(s + 1) * n], outs[1][s * n:(s + 1) * n])
                            for s in range(r)], axis=1)


def _attn_lat_kernel(sink_ref, q_ref, kp_ref, kc_ref, kn_ref, kx_ref, vp_ref, vc_ref, vn_ref, vx_ref, o_ref):
    nb = pl.num_programs(1)
    n = pl.program_id(1)
    rows = (ATT_HEADS // ATT_KV_HEADS) * BLOCK
    i = lax.broadcasted_iota(jnp.int32, (rows, BLOCK), 0) & (BLOCK - 1)
    j = lax.broadcasted_iota(jnp.int32, (rows, BLOCK), 1)
    m_prev = (j >= i) & (n > 0)
    m_next = (j <= i) & (n < nb - 1)
    k = jnp.concatenate([kp_ref[0], kc_ref[0], kn_ref[0], kx_ref[0]], axis=0)
    v = jnp.concatenate([vp_ref[0], vc_ref[0], vn_ref[0], vx_ref[0]], axis=0)
    o_ref[0] = _attend(q_ref[0], k, v, {0: m_prev, 2: m_next}, sink_ref).astype(BF16)


def _attn_ctx_kernel(sink_ref, q_ref, kx_ref, vx_ref, prev_ref, o_ref):
    o_ref[0] = _attend(q_ref[0], kx_ref[0], vx_ref[0], {}, sink_ref).astype(BF16)


def _attention(q, k, v, sink, with_ctx):
    bsz, s, _ = q.shape
    nb = SEQ // BLOCK
    ctx_blk = SEQ // CTX_LEN
    smem = pl.BlockSpec(memory_space=pltpu.SMEM)
    kv = lambda f: pl.BlockSpec((1, BLOCK, KV_WIDTH), f)
    prev = lambda b, n: (b, jnp.maximum(n - 1, 0), 0)
    cur = lambda b, n: (b, n, 0)
    nxt = lambda b, n: (b, jnp.minimum(n + 1, nb - 1), 0)
    ctx = pl.BlockSpec((1, CTX_LEN, KV_WIDTH), lambda b, n: (b, ctx_blk, 0))
    sink8 = jnp.pad(sink, (0, 8 - ATT_HEADS))
    out = pl.pallas_call(
        _attn_lat_kernel,
        grid=(bsz, nb),
        in_specs=[smem, pl.BlockSpec((1, BLOCK, ATT_WIDTH), cur),
                  kv(prev), kv(cur), kv(nxt), ctx, kv(prev), kv(cur), kv(nxt), ctx],
        out_specs=pl.BlockSpec((1, BLOCK, ATT_WIDTH), cur),
        out_shape=jax.ShapeDtypeStruct((bsz, s, ATT_WIDTH), BF16),
        compiler_params=_cparams(("parallel", "arbitrary")),
        name="window_attention",
    )(sink8, q, k, k, k, k, v, v, v, v)
    if not with_ctx:
        return out
    cq = lambda b: (b, ctx_blk, 0)
    return pl.pallas_call(
        _attn_ctx_kernel,
        grid=(bsz,),
        in_specs=[smem, pl.BlockSpec((1, CTX_LEN, ATT_WIDTH), cq),
                  pl.BlockSpec((1, CTX_LEN, KV_WIDTH), cq), pl.BlockSpec((1, CTX_LEN, KV_WIDTH), cq),
                  pl.BlockSpec(memory_space=pl.ANY)],
        out_specs=pl.BlockSpec((1, CTX_LEN, ATT_WIDTH), cq),
        out_shape=jax.ShapeDtypeStruct((bsz, s, ATT_WIDTH), BF16),
        input_output_aliases={4: 0},
        compiler_params=_cparams(("parallel",)),
        name="context_attention",
    )(sink8, q, k, v, out)


HALO = 16
N_SHIFT = CONV_K - 1


def _shift_table():
    i = jnp.arange(CHUNK)[:, None]
    j = jnp.arange(CHUNK + 2 * HALO)[None, :]
    taps = [k for k in range(CONV_K) if k != CONV_K // 2]
    return jnp.concatenate([(j == i + k - CONV_K // 2 + HALO) for k in taps], axis=0).astype(BF16)


def _expand_table(lane0):
    src = jnp.arange(LANE)[:, None] - lane0
    return (src == (jnp.arange(SSD_WIDTH)[None, :] // HEAD_DIM)).astype(BF16)


def _conv_chunk(c, xp_ref, xc_ref, xn_ref, sh_ref, w_ref, b_ref):
    n_lat = SEQ // CHUNK
    last = S_ALL // CHUNK - 1
    has_prev = (c != 0) & (c != n_lat)
    has_next = (c != n_lat - 1) & (c != last)
    zero = jnp.zeros((), BF16)
    cur = xc_ref[0]
    xx = jnp.concatenate([jnp.where(has_prev, xp_ref[0], zero), cur, jnp.where(has_next, xn_ref[0], zero)], axis=0)
    sh = jnp.dot(sh_ref[...], xx, preferred_element_type=F32)
    mid = CONV_K // 2
    acc = b_ref[...] + w_ref[mid:mid + 1, :] * cur.astype(F32)
    for t in range(N_SHIFT):
        k = t if t < mid else t + 1
        acc = acc + w_ref[k:k + 1, :] * sh[t * CHUNK:(t + 1) * CHUNK]
    return _silu(acc)


def _ssd_chunk(xc, dt_raw, dtb, alog, ex_ref, dsk, state_ref, reverse):
    xs = xc[:, :SSD_WIDTH]
    gs = SSD_GROUPS * SSD_STATE
    bm = xc[:, SSD_WIDTH:SSD_WIDTH + gs].astype(BF16)
    cm = xc[:, SSD_WIDTH + gs:SSD_WIDTH + 2 * gs].astype(BF16)
    raw = dt_raw + dtb
    dt = jnp.maximum(raw, 0.0) + jnp.log(1.0 + jnp.exp(-jnp.abs(raw)))
    dta = dt * (-jnp.exp(alog))
    ri = lax.broadcasted_iota(jnp.int32, (CHUNK, CHUNK), 0)
    ci = lax.broadcasted_iota(jnp.int32, (CHUNK, CHUNK), 1)
    keep = (ri <= ci) if reverse else (ri >= ci)
    tri = keep.astype(BF16)
    tri_t = ((ri >= ci) if reverse else (ri <= ci)).astype(BF16)
    acs = sum(jnp.dot(tri, p, preferred_element_type=F32) for p in _split_bf16(dta, 3))
    acs_t = sum(jnp.dot(p, tri_t, preferred_element_type=F32) for p in _split_bf16(dta.T, 3))
    parts = jnp.concatenate(_split_bf16(dt, 2) + _split_bf16(acs, 3), axis=0)
    wide = jnp.dot(parts, ex_ref[...], preferred_element_type=F32)
    dt_x = wide[:CHUNK] + wide[CHUNK:2 * CHUNK]
    acs_x = wide[2 * CHUNK:3 * CHUNK] + wide[3 * CHUNK:4 * CHUNK] + wide[4 * CHUNK:]
    edge = 0 if reverse else CHUNK - 1
    tot_x = acs_x[edge:edge + 1, :]
    rh = SSD_HEADS // SSD_GROUPS
    gw = rh * HEAD_DIM
    lane0 = SSD_HEADS if reverse else 0
    xd = xs * dt_x
    xdw = (xd * jnp.exp(tot_x - acs_x)).astype(BF16)
    xdb = xd.astype(BF16)
    eacs = jnp.exp(acs_x)
    state = state_ref[...]
    ys = []
    new_state = []
    for g in range(SSD_GROUPS):
        bg = bm[:, g * SSD_STATE:(g + 1) * SSD_STATE]
        cg = cm[:, g * SSD_STATE:(g + 1) * SSD_STATE]
        cb = lax.dot_general(cg, bg, (((1,), (1,)), ((), ())), preferred_element_type=F32)
        sg = state[:, g * gw:(g + 1) * gw]
        y_off = jnp.dot(cg, sg.astype(BF16), preferred_element_type=F32) * eacs[:, g * gw:(g + 1) * gw]
        for r in range(rh):
            h = g * rh + r
            ln = lane0 + h
            dec = jnp.where(keep, jnp.exp(acs[:, ln:ln + 1] - acs_t[ln:ln + 1, :]), 0.0)
            yd = jnp.dot((cb * dec).astype(BF16), xdb[:, h * HEAD_DIM:(h + 1) * HEAD_DIM],
                         preferred_element_type=F32)
            ys.append(yd + y_off[:, r * HEAD_DIM:(r + 1) * HEAD_DIM])
        bt = bg.astype(F32).T.astype(BF16)
        new_state.append(jnp.dot(bt, xdw[:, g * gw:(g + 1) * gw], preferred_element_type=F32))
    state_ref[...] = state * jnp.exp(tot_x) + jnp.concatenate(new_state, axis=1)
    y = jnp.concatenate(ys, axis=1)
    return y if dsk is None else y + dsk * xs


def _ssd_kernel(fp_ref, fc_ref, fn_ref, fdt_ref, bp_ref, bc_ref, bn_ref, bdt_ref,
                sh_ref, w_ref, b_ref, dtb_ref, alog_ref, exf_ref, exb_ref, dsk_ref,
                yf_ref, yb_ref, sf_ref, sb_ref):
    step = pl.program_id(1)
    nc = pl.num_programs(1)

    @pl.when(step == 0)
    def _():
        sf_ref[...] = jnp.zeros_like(sf_ref)
        sb_ref[...] = jnp.zeros_like(sb_ref)

    cf = (step + SEQ // CHUNK) % nc
    cbk = nc - 1 - step
    xf = _conv_chunk(cf, fp_ref, fc_ref, fn_ref, sh_ref, w_ref, b_ref)
    xb = _conv_chunk(cbk, bp_ref, bc_ref, bn_ref, sh_ref, w_ref, b_ref)
    dtb, alog = dtb_ref[...], alog_ref[...]
    yf_ref[0] = _ssd_chunk(xf, fdt_ref[0], dtb, alog, exf_ref, dsk_ref[...], sf_ref, False).astype(BF16)
    yb_ref[0] = _ssd_chunk(xb, bdt_ref[0], dtb, alog, exb_ref, None, sb_ref, True).astype(BF16)


def _ssd_mixer(xbc, dt_raw, conv_w, conv_b, dt_bias, a_log, d_skip):
    bsz, s, cdim = xbc.shape
    nc = s // CHUNK
    n_lat = SEQ // CHUNK
    per = CHUNK // HALO
    nh = s // HALO
    fwd = lambda t: (t + n_lat) % nc
    bwd = lambda t: nc - 1 - t

    def chunk_specs(order):
        return [pl.BlockSpec((1, HALO, cdim), lambda b, t: (b, jnp.maximum(order(t) * per - 1, 0), 0)),
                pl.BlockSpec((1, CHUNK, cdim), lambda b, t: (b, order(t), 0)),
                pl.BlockSpec((1, HALO, cdim), lambda b, t: (b, jnp.minimum(order(t) * per + per, nh - 1), 0)),
                pl.BlockSpec((1, CHUNK, LANE), lambda b, t: (b, order(t), 0))]

    c2 = lambda b, t: (0, 0)
    full = lambda a: pl.BlockSpec(a.shape, c2)
    pad = lambda p: jnp.pad(p.reshape(1, -1), ((0, 0), (0, LANE - 2 * SSD_HEADS)))
    consts = [_shift_table(), jnp.pad(conv_w, ((0, 8 - CONV_K), (0, 0))), conv_b.reshape(1, cdim),
              pad(dt_bias), pad(a_log), _expand_table(0), _expand_table(SSD_HEADS),
              jnp.repeat(d_skip, HEAD_DIM).reshape(1, SSD_WIDTH)]
    return pl.pallas_call(
        _ssd_kernel,
        grid=(bsz, nc),
        in_specs=chunk_specs(fwd) + chunk_specs(bwd) + [full(a) for a in consts],
        out_specs=[pl.BlockSpec((1, CHUNK, SSD_WIDTH), lambda b, t: (b, fwd(t), 0)),
                   pl.BlockSpec((1, CHUNK, SSD_WIDTH), lambda b, t: (b, bwd(t), 0))],
        out_shape=[jax.ShapeDtypeStruct((bsz, s, SSD_WIDTH), BF16)] * 2,
        scratch_shapes=[pltpu.VMEM((SSD_STATE, SSD_WIDTH), F32)] * 2,
        compiler_params=_cparams(("parallel", "arbitrary")),
        name="conv_ssd_scan",
    )(xbc, xbc, xbc, dt_raw, xbc, xbc, xbc, dt_raw, *consts)


def _out_kernel(*refs, n_h):
    (fo_ref, at_ref, yf_ref, yb_ref, z_ref), refs = refs[:5], refs[5:]
    h_refs, refs = refs[:n_h], refs[n_h:]
    mod_ref, gssd_ref, gpost_ref, gpre_ref, wo_ref, wr_ref, hout_ref, u_ref, lg_ref = refs
    y = (yf_ref[0].astype(F32) + yb_ref[0].astype(F32)) * _silu(z_ref[0].astype(F32))
    sm = _rms(y, gssd_ref[...]).astype(BF16)
    m = (jnp.dot(fo_ref[0], wo_ref[0:256, :], preferred_element_type=F32)
         + jnp.dot(at_ref[0], wo_ref[256:640, :], preferred_element_type=F32)
         + jnp.dot(sm, wo_ref[640:1024, :], preferred_element_type=F32))
    mod = mod_ref[0, 0]
    hn = _stream_tile(h_refs, ROW_TILE) + mod[2:3] * _rms(m, gpost_ref[...])
    hout_ref[0] = hn
    u = _rms(hn, gpre_ref[...]) * (1.0 + mod[4:5]) + mod[3:4]
    half = D_MODEL // 2
    lo = pltpu.bitcast(u[:, :half].astype(BF16).astype(F32), jnp.uint32) >> 16
    hi = pltpu.bitcast(u[:, half:].astype(BF16).astype(F32), jnp.uint32) & jnp.uint32(0xFFFF0000)
    u_ref[0] = lo | hi
    uh, ul = _split_bf16(u, 2)
    wh, wl = wr_ref[0], wr_ref[1]
    nt = (((1,), (1,)), ((), ()))
    lg_ref[0] = (lax.dot_general(wh, uh, nt, preferred_element_type=F32)
                 + lax.dot_general(wh, ul, nt, preferred_element_type=F32)
                 + lax.dot_general(wl, uh, nt, preferred_element_type=F32))


def _out_proj(fo, at, yf, yb, z, h, mod_l, g_ssd, g_post, g_pre, w_out_p, w_router, n_rows):
    bsz = fo.shape[0]
    d = D_MODEL
    tm = ROW_TILE
    n_lat = SEQ // tm
    row = lambda b, i: (b, i, 0)
    c2 = lambda b, i: (0, 0)
    wr = jnp.stack(_split_bf16(w_router.T, 2))
    h_specs, h_args = _stream_specs(h, tm)
    return pl.pallas_call(
        functools.partial(_out_kernel, n_h=len(h_args)),
        grid=(bsz, n_rows // tm),
        in_specs=[pl.BlockSpec((1, tm, F_WIDTH), row), pl.BlockSpec((1, tm, ATT_WIDTH), row),
                  pl.BlockSpec((1, tm, SSD_WIDTH), row), pl.BlockSpec((1, tm, SSD_WIDTH), row),
                  pl.BlockSpec((1, tm, SSD_WIDTH), row)] + h_specs + [
                  pl.BlockSpec((1, 1, 6, d), lambda b, i: (jnp.where(i < n_lat, b, bsz), 0, 0, 0)),
                  pl.BlockSpec((1, SSD_WIDTH), c2), pl.BlockSpec((1, d), c2), pl.BlockSpec((1, d), c2),
                  pl.BlockSpec((d, d), c2), pl.BlockSpec((2, N_EXPERTS, d), lambda b, i: (0, 0, 0))],
        out_specs=[pl.BlockSpec((1, tm, d), row), pl.BlockSpec((1, tm, d // 2), row),
                   pl.BlockSpec((1, N_EXPERTS, tm), lambda b, i: (b, 0, i))],
        out_shape=[jax.ShapeDtypeStruct((bsz, n_rows, d), F32),
                   jax.ShapeDtypeStruct((bsz, n_rows, d // 2), jnp.uint32),
                   jax.ShapeDtypeStruct((bsz, N_EXPERTS, n_rows), F32)],
        compiler_params=_cparams(("parallel", "arbitrary"), VMEM_LIMIT),
        name="outproj_norms_router",
    )(fo, at, yf, yb, z, *h_args, mod_l.reshape(mod_l.shape[0], 1, 6, d), g_ssd.reshape(1, SSD_WIDTH),
      g_post.reshape(1, d), g_pre.reshape(1, d), w_out_p, wr)


def _prefix_lanes(mask, upper):
    n_r, t = mask.shape
    nb = t // LANE
    if (nb * n_r) % LANE:
        carry = jnp.zeros((n_r, 1), F32)
        outs = []
        for j in range(nb):
            pre = jnp.dot(mask[:, j * LANE:(j + 1) * LANE].astype(BF16), upper, preferred_element_type=F32) + carry
            outs.append(pre)
            carry = pre[:, LANE - 1:LANE]
        return jnp.concatenate(outs, axis=1)
    stacked = jnp.concatenate([mask[:, j * LANE:(j + 1) * LANE] for j in range(nb)], axis=0).astype(BF16)
    pre = jnp.dot(stacked, upper, preferred_element_type=F32)
    ri = lax.broadcasted_iota(jnp.int32, (LANE, LANE), 0)
    last_row = (ri == LANE - 1).astype(BF16)
    tot = jnp.dot(pre.astype(BF16), last_row, preferred_element_type=F32)
    ra = lax.broadcasted_iota(jnp.int32, (nb * n_r, nb * n_r), 0)
    ca = lax.broadcasted_iota(jnp.int32, (nb * n_r, nb * n_r), 1)
    assert n_r & (n_r - 1) == 0
    earlier = (((ca ^ ra) & (n_r - 1)) == 0) & (ca < ra)
    earlier = earlier.astype(BF16)
    full = pre + jnp.dot(earlier, tot.astype(BF16), preferred_element_type=F32)
    return jnp.concatenate([full[j * n_r:(j + 1) * n_r] for j in range(nb)], axis=1)


def _slot_tokens(csum, cap):
    n_r, t = csum.shape
    nb = t // LANE
    slot = lax.broadcasted_iota(jnp.int32, (1, cap), 1).astype(F32)
    rows = []
    if nb < 8:
        csum_t = jnp.concatenate([csum, jnp.zeros((LANE - n_r, t), F32)], axis=0).T
        for x in range(n_r):
            rows.append(jnp.sum((csum_t[:, x:x + 1] <= slot).astype(F32), axis=0, keepdims=True))
        return jnp.concatenate(rows, axis=0)
    blk_id = lax.broadcasted_iota(jnp.int32, (LANE, 1), 0).astype(F32)
    never = jnp.full((LANE - nb, LANE), float(t), F32)
    for x in range(n_r):
        cx = jnp.concatenate([csum[x:x + 1, j * LANE:(j + 1) * LANE] for j in range(nb)] + [never], axis=0)
        blk = jnp.sum((cx[:, LANE - 1:LANE] <= slot).astype(F32), axis=0, keepdims=True)
        pick = (blk_id == blk).astype(BF16)
        g = sum(jnp.dot(p, pick, preferred_element_type=F32) for p in _split_bf16(cx.T, 2))
        within = jnp.sum((g <= slot).astype(F32), axis=0, keepdims=True)
        rows.append(blk * LANE + within)
    return jnp.concatenate(rows, axis=0)


def _route_kernel(lg_ref, idx_ref, aff_ref, *, cap):
    l = lg_ref[0]
    n_e, t = l.shape
    e = jnp.exp(l - jnp.max(l, axis=0, keepdims=True))
    aff = e / jnp.sum(e, axis=0, keepdims=True)
    aff_ref[0] = aff
    keys = pltpu.bitcast(aff, jnp.int32)

    def search(it, lo):
        cand = lo | (jnp.int32(1) << (30 - it))
        cnt = jnp.sum((keys >= cand).astype(F32), axis=1, keepdims=True)
        return jnp.where(cnt >= cap, cand, lo)

    thr = lax.fori_loop(0, 31, search, jnp.zeros((n_e, 1), jnp.int32))
    gt = keys > thr
    eq = keys == thr
    need = cap - jnp.sum(gt.astype(F32), axis=1, keepdims=True)
    ri = lax.broadcasted_iota(jnp.int32, (LANE, LANE), 0)
    ci = lax.broadcasted_iota(jnp.int32, (LANE, LANE), 1)
    upper = (ri <= ci).astype(BF16)
    eq_rank = _prefix_lanes(eq.astype(F32), upper)
    sel = gt | (eq & (eq_rank <= need))
    csum = _prefix_lanes(sel.astype(F32), upper)
    idx_ref[0] = _slot_tokens(csum, cap).astype(jnp.int32)


def _route(logits_t, row0, t, cap):
    bsz, n_e, _ = logits_t.shape
    return pl.pallas_call(
        functools.partial(_route_kernel, cap=cap),
        grid=(bsz,),
        in_specs=[pl.BlockSpec((1, n_e, t), lambda b: (b, 0, row0 // t))],
        out_specs=[pl.BlockSpec((1, n_e, cap), lambda b: (b, 0, 0)),
                   pl.BlockSpec((1, n_e, t), lambda b: (b, 0, 0))],
        out_shape=[jax.ShapeDtypeStruct((bsz, n_e, cap), jnp.int32),
                   jax.ShapeDtypeStruct((bsz, n_e, t), F32)],
        compiler_params=_cparams(("parallel",), VMEM_LIMIT),
        name="expert_choice_route",
    )(logits_t)


def _smem_row(n):
    return pl.BlockSpec((1, 1, 1, n), lambda b, e: (b, e, 0, 0), memory_space=pltpu.SMEM)


def _gather_kernel(idx_ref, u_ref, o_ref, *, cap):
    def body(i, carry):
        t = idx_ref[0, 0, 0, i]
        o_ref[0, pl.ds(i, 1), :] = u_ref[0, pl.ds(t, 1), :]
        return carry

    lax.fori_loop(0, cap, body, 0, unroll=8)


def _gather(u_packed, idx, row0, t, cap):
    bsz, _, half = u_packed.shape
    n_e = idx.shape[1]
    return pl.pallas_call(
        functools.partial(_gather_kernel, cap=cap),
        grid=(bsz, n_e),
        in_specs=[_smem_row(cap), pl.BlockSpec((1, t, half), lambda b, e: (b, row0 // t, 0))],
        out_specs=pl.BlockSpec((1, cap, half), lambda b, e: (e, b, 0)),
        out_shape=jax.ShapeDtypeStruct((n_e, bsz * cap, half), jnp.uint32),
        compiler_params=_cparams(("parallel", "arbitrary"), VMEM_LIMIT),
        name="expert_gather",
    )(idx[:, :, None, :], u_packed)


FF_TILE = 256
OUT_TILE = 256


def _ffn_kernel(x_ref, wg_ref, wu_ref, wd_ref, o_ref, xs_ref, hm_ref, tmp_ref, *, n_up):
    j = pl.program_id(2)

    @pl.when(j == 0)
    def _():
        w = x_ref[0]
        half = w.shape[1]
        xs_ref[:, :half] = pltpu.bitcast(w << 16, F32).astype(BF16)
        xs_ref[:, half:] = pltpu.bitcast(w & jnp.uint32(0xFFFF0000), F32).astype(BF16)

    @pl.when(j < n_up)
    def _():
        x = xs_ref[...]
        g = jnp.dot(x, wg_ref[0, 0], preferred_element_type=F32)
        u = jnp.dot(x, wu_ref[0, 0], preferred_element_type=F32)
        tmp_ref[...] = (_silu(g) * u).astype(BF16)

    for f in range(n_up):
        @pl.when(j == f)
        def _(f=f):
            hm_ref[:, f * FF_TILE:(f + 1) * FF_TILE] = tmp_ref[...]

    @pl.when(j >= n_up)
    def _():
        o_ref[0] = jnp.dot(hm_ref[...], wd_ref[0, 0], preferred_element_type=F32)


def _expert_weights(w_gate, w_up, w_down):
    n_e, d, ff = w_gate.shape
    up = lambda w: w.astype(BF16).reshape(n_e, d, ff // FF_TILE, FF_TILE).transpose(0, 2, 1, 3)
    return up(w_gate), up(w_up), w_down.astype(BF16).reshape(n_e, ff, d // OUT_TILE, OUT_TILE).transpose(0, 2, 1, 3)


def _expert_ffn(xe, wg, wu, wd):
    n_e, m, half = xe.shape
    d = 2 * half
    n_up, ff = wg.shape[1], wd.shape[2]
    tm = min(m, 2048)
    up = lambda e, i, j: (e, jnp.minimum(j, n_up - 1), 0, 0)
    down = lambda j: jnp.maximum(j - n_up, 0)
    return pl.pallas_call(
        functools.partial(_ffn_kernel, n_up=n_up),
        grid=(n_e, m // tm, n_up + d // OUT_TILE),
        in_specs=[pl.BlockSpec((1, tm, half), lambda e, i, j: (e, i, 0)),
                  pl.BlockSpec((1, 1, d, FF_TILE), up),
                  pl.BlockSpec((1, 1, d, FF_TILE), up),
                  pl.BlockSpec((1, 1, ff, OUT_TILE), lambda e, i, j: (e, down(j), 0, 0))],
        out_specs=pl.BlockSpec((1, tm, OUT_TILE), lambda e, i, j: (e, i, down(j))),
        out_shape=jax.ShapeDtypeStruct((n_e, m, d), F32),
        scratch_shapes=[pltpu.VMEM((tm, d), BF16), pltpu.VMEM((tm, ff), BF16), pltpu.VMEM((tm, FF_TILE), BF16)],
        compiler_params=_cparams(("parallel", "parallel", "arbitrary"), VMEM_LIMIT),
        name="expert_ffn",
    )(xe, wg, wu, wd)


COMB_ROWS = 8


def _combine_kernel(idx_ref, aff_ref, y_ref, *rest, cap):
    o_ref = rest[-1]

    @pl.when(pl.program_id(1) == 0)
    def _():
        o_ref[...] = jnp.zeros_like(o_ref)

    def body(c, carry):
        base = pl.multiple_of(c * COMB_ROWS, COMB_ROWS)
        toks = [idx_ref[0, 0, 0, base + r] for r in range(COMB_ROWS)]
        rows = [o_ref[0, pl.ds(t, 1), :] + aff_ref[0, 0, 0, t] * y_ref[0, pl.ds(base + r, 1), :]
                for r, t in enumerate(toks)]
        for t, row in zip(toks, rows):
            o_ref[0, pl.ds(t, 1), :] = row
        return carry

    lax.fori_loop(0, cap // COMB_ROWS, body, 0)


def _combine(y, idx, aff, row0, t, cap, s_total, out_prev=None):
    n_e, m, d = y.shape
    bsz = idx.shape[0]
    in_specs = [_smem_row(cap), _smem_row(t), pl.BlockSpec((1, cap, d), lambda b, e: (e, b, 0))]
    args = [idx[:, :, None, :], aff[:, :, None, :], y]
    aliases = {}
    if out_prev is not None:
        in_specs.append(pl.BlockSpec(memory_space=pl.ANY))
        args.append(out_prev)
        aliases = {3: 0}
    return pl.pallas_call(
        functools.partial(_combine_kernel, cap=cap),
        grid=(bsz, n_e),
        in_specs=in_specs,
        out_specs=pl.BlockSpec((1, t, d), lambda b, e: (b, row0 // t, 0)),
        out_shape=jax.ShapeDtypeStruct((bsz, s_total, d), F32),
        input_output_aliases=aliases,
        compiler_params=_cparams(("parallel", "arbitrary"), VMEM_LIMIT),
        name="expert_combine",
    )(*args)


def _moe(u_packed, logits_t, w_gate, w_up, w_down, with_ctx):
    bsz, rows, _ = u_packed.shape
    wg, wu, wd = _expert_weights(w_gate, w_up, w_down)
    out = None
    sets = [(0, SEQ)] + ([(SEQ, CTX_LEN)] if with_ctx else [])
    for row0, t in sets:
        cap = EC_CAPACITY * t // N_EXPERTS
        idx, aff = _route(logits_t, row0, t, cap)
        xe = _gather(u_packed, idx, row0, t, cap)
        y = _expert_ffn(xe, wg, wu, wd)
        out = _combine(y, idx, aff, row0, t, cap, rows, out)
    return out


def _final_kernel(h_ref, f_ref, mod_ref, g_ref, o_ref):
    o_ref[0] = h_ref[0] + mod_ref[0, 0][5:6] * _rms(f_ref[0], g_ref[...])


def _final_residual(h, f, mod_l, g_post):
    bsz, n, d = h.shape
    tm = ROW_TILE
    row = lambda b, i: (b, i, 0)
    return pl.pallas_call(
        _final_kernel,
        grid=(bsz, n // tm),
        in_specs=[pl.BlockSpec((1, tm, d), row), pl.BlockSpec((1, tm, d), row),
                  pl.BlockSpec((1, 1, 6, d), lambda b, i: (b, 0, 0, 0)), pl.BlockSpec((1, d), lambda b, i: (0, 0))],
        out_specs=pl.BlockSpec((1, tm, d), row),
        out_shape=jax.ShapeDtypeStruct((bsz, n, d), F32),
        compiler_params=_cparams(("parallel", "arbitrary")),
        name="ffn_residual",
    )(h, f, mod_l.reshape(mod_l.shape[0], 1, 6, d), g_post.reshape(1, d))


def _rope_tables():
    n = jnp.arange(SEQ, dtype=jnp.int32)
    row = (n // GRID_W).astype(F32)
    col = (n % GRID_W).astype(F32)
    quarter = HEAD_DIM // 4
    inv = ROPE_THETA ** (-jnp.arange(quarter, dtype=F32) / quarter)
    lane = jnp.arange(LANE)
    pos = jnp.where((lane % HEAD_DIM < HEAD_DIM // 2)[None, :], row[:, None], col[:, None])
    ang = pos * inv[lane % quarter][None, :]
    first = (lane % (2 * quarter) < quarter)[None, :]
    cos = jnp.cos(ang)
    sin = jnp.sin(ang)
    sa = jnp.where(first, -sin, 0.0)
    sb = jnp.where(first, 0.0, sin)
    padc = lambda t, v: jnp.concatenate([t, jnp.full((CTX_LEN, LANE), v, F32)], axis=0)
    return padc(cos, 1.0), padc(sa, 0.0), padc(sb, 0.0)


def kernel(x, c, ctx, c_ctx, w_ada, b_ada, g_mix_pre, g_mix_post, g_ffn_pre, g_ffn_post, w_in, conv_w, conv_b,
           dt_bias, a_log, d_skip, g_ssd, attn_sink, w_out, w_router, w_gate, w_up, w_down):
    depth = w_ada.shape[0]
    bsz = x.shape[0]
    d = D_MODEL
    cc = jnp.concatenate([c, c_ctx[None, :], jnp.zeros((7, d), F32)], axis=0)
    mod = _modulation(cc, w_ada, b_ada).reshape(depth, bsz + 8, 6, d)
    rope_tabs = _rope_tables()
    chan_tab = _channel_tables()
    cn_lat, sn_lat = _dft_tables(SEQ)
    cn_ctx, sn_ctx = _dft_tables(CTX_LEN)
    q0 = F_WIDTH

    h = (x, ctx)
    f_prev = None
    for l in range(depth):
        last = l == depth - 1
        w_q = _to_slab_order(w_in[l][:, q0:q0 + ATT_WIDTH], 1) * HEAD_DIM ** -0.5
        w_in_p = jnp.concatenate([w_in[l][:, :q0], w_q, w_in[l][:, q0 + ATT_WIDTH:],
                                  jnp.zeros((d, IN_PAD - IN_WIDTH), F32)], axis=1).astype(BF16)
        w_out_p = jnp.concatenate([w_out[l][:q0], _to_slab_order(w_out[l][q0:q0 + ATT_WIDTH], 0),
                                   w_out[l][q0 + ATT_WIDTH:]], axis=0).astype(BF16)
        outs = _in_proj(h, mod[l], mod[l - 1] if l else None, g_mix_pre[l], w_in_p, rope_tabs,
                        f_prev, g_ffn_post[l - 1] if l else None)
        if f_prev is not None:
            h, outs = outs[0], outs[1:]
        fo, q, k, v, z, xbc, dt_raw = outs
        fmix = _fourier(fo, chan_tab, cn_lat, sn_lat, 0, SEQ)
        if not last:
            fmix = _fourier(fo, chan_tab, cn_ctx, sn_ctx, SEQ, CTX_LEN, fmix)
        att = _attention(q, k, v, attn_sink[l], not last)
        y_f, y_b = _ssd_mixer(xbc, dt_raw, conv_w[l], conv_b[l], dt_bias[l], a_log[l], d_skip[l])
        n_rows = SEQ if last else S_ALL
        h, u_packed, logits_t = _out_proj(fmix, att, y_f, y_b, z, h, mod[l], g_ssd[l],
                                          g_mix_post[l], g_ffn_pre[l], w_out_p, w_router[l], n_rows)
        f_prev = _moe(u_packed, logits_t, w_gate[l], w_up[l], w_down[l], not last)
    return _final_residual(h, f_prev, mod[depth - 1], g_ffn_post[depth - 1])
```

```python
import functools
import math

import jax
import jax.numpy as jnp
from jax import lax
from jax.experimental import pallas as pl
from jax.experimental.pallas import tpu as pltpu

F32 = jnp.float32
BF16 = jnp.bfloat16

D_MODEL = 1024
SEQ = 4096
CTX_LEN = 256
S_ALL = SEQ + CTX_LEN
GRID_W = 64
HEAD_DIM = 64
EPS = 1e-6
F_WIDTH = 256
ATT_HEADS = 6
ATT_KV_HEADS = 2
ATT_WIDTH = 384
KV_WIDTH = 128
BLOCK = 128
ROPE_THETA = 10000.0
SSD_HEADS = 6
SSD_WIDTH = 384
SSD_GROUPS = 2
SSD_STATE = 128
CHUNK = 128
CONV_K = 5
CONV_DIM = 896
IN_WIDTH = 2188
IN_PAD = 2304
N_EXPERTS = 16
EXPERT_FF = 2816
EC_CAPACITY = 2

LANE = 128
ROW_TILE = 256
VMEM_LIMIT = 56 * 1024 * 1024


def _cparams(sem, vmem=None):
    return pltpu.CompilerParams(dimension_semantics=sem, vmem_limit_bytes=vmem)


def _split_bf16(x, n):
    parts, r = [], x
    for _ in range(n):
        p = r.astype(BF16)
        parts.append(p)
        r = r - p.astype(F32)
    return parts


def _rms(x, g):
    return x * lax.rsqrt(jnp.mean(x * x, axis=-1, keepdims=True) + EPS) * g


def _silu(x):
    return x * jax.nn.sigmoid(x)


def _mod_kernel(cc_ref, w_ref, b_ref, o_ref):
    s = _silu(cc_ref[...])
    o_ref[0] = jnp.dot(s.astype(BF16), w_ref[0].astype(BF16), preferred_element_type=F32) + b_ref[0]


def _modulation(cc, w_ada, b_ada):
    n_l, d, d6 = w_ada.shape
    r = cc.shape[0]
    tn = 1024
    return pl.pallas_call(
        _mod_kernel,
        grid=(n_l, d6 // tn),
        in_specs=[pl.BlockSpec((r, d), lambda l, j: (0, 0)),
                  pl.BlockSpec((1, d, tn), lambda l, j: (l, 0, j)),
                  pl.BlockSpec((1, 1, tn), lambda l, j: (l, 0, j))],
        out_specs=pl.BlockSpec((1, r, tn), lambda l, j: (l, 0, j)),
        out_shape=jax.ShapeDtypeStruct((n_l, r, d6), F32),
        compiler_params=_cparams(("arbitrary", "arbitrary")),
        name="adaln_mod",
    )(cc, w_ada, b_ada.reshape(n_l, 1, d6))


def _stream_specs(h, tm):
    n_lat = SEQ // tm
    if isinstance(h, tuple):
        lat, ctx = h
        d = lat.shape[-1]
        return ([pl.BlockSpec((1, tm, d), lambda b, i: (b, jnp.minimum(i, n_lat - 1), 0)),
                 pl.BlockSpec((1, tm, d), lambda b, i: (b, jnp.maximum(i - n_lat, 0), 0))], [lat, ctx])
    return [pl.BlockSpec((1, tm, h.shape[-1]), lambda b, i: (b, i, 0))], [h]


def _stream_tile(refs, tm):
    if len(refs) == 2:
        return jnp.where(pl.program_id(1) < SEQ // tm, refs[0][0], refs[1][0])
    return refs[0][0]


def _in_kernel(*refs, residual, n_h):
    h_refs, refs = refs[:n_h], refs[n_h:]
    if residual:
        (f_ref, gpost_ref, mod_ref, g_ref, w_ref, cos_ref, sa_ref, sb_ref,
         hout_ref, fo_ref, q_ref, k_ref, v_ref, z_ref, xbc_ref, dt_ref) = refs
    else:
        (mod_ref, g_ref, w_ref, cos_ref, sa_ref, sb_ref,
         fo_ref, q_ref, k_ref, v_ref, z_ref, xbc_ref, dt_ref) = refs
    x = _stream_tile(h_refs, ROW_TILE)
    mod = mod_ref[0, 0]
    if residual:
        x = x + mod[5:6] * _rms(f_ref[0], gpost_ref[...])
        hout_ref[0] = x
    u = _rms(x, g_ref[...]) * (1.0 + mod[1:2]) + mod[0:1]
    p = jnp.dot(u.astype(BF16), w_ref[...], preferred_element_type=F32)
    cos, sa, sb = cos_ref[...], sa_ref[...], sb_ref[...]

    def rope(t):
        return t * cos + pltpu.roll(t, LANE - 16, 1) * sa + pltpu.roll(t, 16, 1) * sb

    fo_ref[0] = p[:, 0:256].astype(BF16)
    q_ref[0] = jnp.concatenate([rope(p[:, 256 + LANE * s:256 + LANE * (s + 1)]) for s in range(3)],
                               axis=1).astype(BF16)
    k_ref[0] = rope(p[:, 640:768]).astype(BF16)
    v_ref[0] = p[:, 768:896].astype(BF16)
    z_ref[0] = p[:, 896:1280].astype(BF16)
    xbc_ref[0] = p[:, 1280:2176].astype(BF16)
    dt_ref[0] = p[:, 2176:2304]


def _in_proj(h, mod_l, mod_prev, g_pre, w_in_p, rope_tabs, f_prev=None, g_post_prev=None):
    bsz = mod_l.shape[0] - 8
    s, d = S_ALL, D_MODEL
    tm = ROW_TILE
    n_lat = SEQ // tm
    residual = f_prev is not None
    row = lambda b, i: (b, i, 0)
    mod_map = lambda b, i: (jnp.where(i < n_lat, b, bsz), 0, 0, 0)
    const2 = lambda b, i: (0, 0)
    tab = pl.BlockSpec((tm, LANE), lambda b, i: (i, 0))
    in_specs, args = _stream_specs(h, tm)
    n_h = len(args)
    if residual:
        in_specs += [pl.BlockSpec((1, tm, d), row), pl.BlockSpec((1, d), const2)]
        args += [f_prev, g_post_prev.reshape(1, d)]
    mod_used = mod_l if not residual else jnp.concatenate([mod_l[:, :5], mod_prev[:, 5:6]], axis=1)
    in_specs += [pl.BlockSpec((1, 1, 6, d), mod_map), pl.BlockSpec((1, d), const2),
                 pl.BlockSpec((d, IN_PAD), const2), tab, tab, tab]
    args += [mod_used.reshape(mod_used.shape[0], 1, 6, d), g_pre.reshape(1, d), w_in_p, *rope_tabs]
    widths = [(256, BF16), (384, BF16), (128, BF16), (128, BF16), (384, BF16), (896, BF16), (128, F32)]
    out_specs = [pl.BlockSpec((1, tm, w), row) for w, _ in widths]
    out_shape = [jax.ShapeDtypeStruct((bsz, s, w), dt) for w, dt in widths]
    if residual:
        out_specs = [pl.BlockSpec((1, tm, d), row)] + out_specs
        out_shape = [jax.ShapeDtypeStruct((bsz, s, d), F32)] + out_shape
    return pl.pallas_call(
        functools.partial(_in_kernel, residual=residual, n_h=n_h),
        grid=(bsz, s // tm),
        in_specs=in_specs, out_specs=out_specs, out_shape=out_shape,
        compiler_params=_cparams(("parallel", "arbitrary"), VMEM_LIMIT),
        name="prenorm_inproj",
    )(*args)


def _f1_kernel(u_ref, w_ref, a_ref, b_ref):
    r = jnp.dot(u_ref[0], w_ref[...], preferred_element_type=F32)
    a_ref[...] = r[:, :F_WIDTH].astype(BF16)
    b_ref[...] = r[:, F_WIDTH:].astype(BF16)


def _f2_kernel(c_ref, s_ref, a_ref, b_ref, *rest):
    o_ref = rest[-1]
    y = (jnp.dot(c_ref[...], a_ref[...], preferred_element_type=F32)
         - jnp.dot(s_ref[...], b_ref[...], preferred_element_type=F32))
    o_ref[0] = y.astype(BF16)


def _dft_tables(n):
    sc = n ** -0.5

    def cs(rows, cols, period):
        m = (rows[:, None] * cols[None, :]) % period
        ang = m.astype(F32) * (2.0 * math.pi / period)
        return jnp.cos(ang), jnp.sin(ang)

    k = jnp.arange(n, dtype=jnp.int32)
    r = 64
    if n <= r * r // 16:
        c, s = cs(k, k, n)
        return (c * sc).astype(BF16), (s * sc).astype(BF16)
    ca, sa = cs(k, jnp.arange(n // r, dtype=jnp.int32), n // r)
    cb, sb = cs(k, jnp.arange(r, dtype=jnp.int32), n)
    c = ca[:, :, None] * cb[:, None, :] - sa[:, :, None] * sb[:, None, :]
    s = sa[:, :, None] * cb[:, None, :] + ca[:, :, None] * sb[:, None, :]
    return (c * sc).reshape(n, n).astype(BF16), (s * sc).reshape(n, n).astype(BF16)


def _channel_tables():
    c64, s64 = _dft_tables(HEAD_DIM)
    eye = jnp.eye(F_WIDTH // HEAD_DIM, dtype=BF16)
    return jnp.concatenate([jnp.kron(eye, c64), jnp.kron(eye, s64)], axis=1)


def _fourier(fo, chan_tab, cn, sn, row0, n, out_prev=None, out_rows=S_ALL):
    bsz, s, _ = fo.shape
    tm = min(n, 512)
    nt = n // tm
    blk0 = row0 // tm
    a_all, b_all = pl.pallas_call(
        _f1_kernel,
        grid=(bsz, nt),
        in_specs=[pl.BlockSpec((1, tm, F_WIDTH), lambda b, i: (b, blk0 + i, 0)),
                  pl.BlockSpec((F_WIDTH, 2 * F_WIDTH), lambda b, i: (0, 0))],
        out_specs=[pl.BlockSpec((tm, F_WIDTH), lambda b, i: (i, b))] * 2,
        out_shape=[jax.ShapeDtypeStruct((n, bsz * F_WIDTH), BF16)] * 2,
        compiler_params=_cparams(("parallel", "arbitrary")),
        name="fourier_channels",
    )(fo, chan_tab)
    in_specs = [pl.BlockSpec((tm, n), lambda i, b: (i, 0)),
                pl.BlockSpec((tm, n), lambda i, b: (i, 0)),
                pl.BlockSpec((n, F_WIDTH), lambda i, b: (0, b)),
                pl.BlockSpec((n, F_WIDTH), lambda i, b: (0, b))]
    args = [cn, sn, a_all, b_all]
    aliases = {}
    if out_prev is not None:
        in_specs.append(pl.BlockSpec(memory_space=pl.ANY))
        args.append(out_prev)
        aliases = {4: 0}
    return pl.pallas_call(
        _f2_kernel,
        grid=(nt, bsz),
        in_specs=in_specs,
        out_specs=pl.BlockSpec((1, tm, F_WIDTH), lambda i, b: (b, blk0 + i, 0)),
        out_shape=jax.ShapeDtypeStruct((bsz, out_rows, F_WIDTH), BF16),
        input_output_aliases=aliases,
        compiler_params=_cparams(("parallel", "arbitrary"), VMEM_LIMIT),
        name="fourier_positions",
    )(*args)


def _to_slab_order(w, axis):
    r = ATT_HEADS // ATT_KV_HEADS
    shape = w.shape
    w = w.reshape(shape[:axis] + (ATT_KV_HEADS, r, HEAD_DIM) + shape[axis + 1:])
    return jnp.swapaxes(w, axis, axis + 1).reshape(shape)


def _attend(q, k, v, masks, sink_ref):
    n = q.shape[0]
    r = ATT_HEADS // ATT_KV_HEADS
    lo = lax.broadcasted_iota(jnp.int32, (1, LANE), 1) < HEAD_DIM
    rid = lax.broadcasted_iota(jnp.int32, (r * n, 1), 0)
    outs = []
    for g in range(ATT_KV_HEADS):
        keep = lo if g == 0 else jnp.logical_not(lo)
        qg = jnp.concatenate([jnp.where(keep, q[:, LANE * s:LANE * (s + 1)], jnp.zeros((), BF16))
                              for s in range(r)], axis=0)
        snk = jnp.full((r * n, 1), sink_ref[r * g], F32)
        for j in range(1, r):
            snk = jnp.where(rid >= j * n, sink_ref[r * g + j], snk)
        sc = lax.dot_general(qg, k, (((1,), (1,)), ((), ())), preferred_element_type=F32)
        if masks:
            nblk = sc.shape[1] // LANE
            sc = jnp.concatenate(
                [jnp.where(masks[b], sc[:, b * LANE:(b + 1) * LANE], -1e30) if b in masks
                 else sc[:, b * LANE:(b + 1) * LANE] for b in range(nblk)], axis=1)
        mx = jnp.maximum(jnp.max(sc, axis=-1, keepdims=True), snk)
        p = jnp.exp(sc - mx)
        den = jnp.sum(p, axis=-1, keepdims=True) + jnp.exp(snk - mx)
        outs.append(jnp.dot(p.astype(BF16), v, preferred_element_type=F32) / den)
    return jnp.concatenate([jnp.where(lo, outs[0][s * n:(s + 1) * n], outs[1][s * n:(s + 1) * n])
                            for s in range(r)], axis=1)


def _attn_lat_kernel(sink_ref, q_ref, kp_ref, kc_ref, kn_ref, kx_ref, vp_ref, vc_ref, vn_ref, vx_ref, o_ref):
    nb = pl.num_programs(1)
    n = pl.program_id(1)
    rows = (ATT_HEADS // ATT_KV_HEADS) * BLOCK
    i = lax.broadcasted_iota(jnp.int32, (rows, BLOCK), 0) & (BLOCK - 1)
    j = lax.broadcasted_iota(jnp.int32, (rows, BLOCK), 1)
    m_prev = (j >= i) & (n > 0)
    m_next = (j <= i) & (n < nb - 1)
    k = jnp.concatenate([kp_ref[0], kc_ref[0], kn_ref[0], kx_ref[0]], axis=0)
    v = jnp.concatenate([vp_ref[0], vc_ref[0], vn_ref[0], vx_ref[0]], axis=0)
    o_ref[0] = _attend(q_ref[0], k, v, {0: m_prev, 2: m_next}, sink_ref).astype(BF16)


def _attn_ctx_kernel(sink_ref, q_ref, kx_ref, vx_ref, prev_ref, o_ref):
    o_ref[0] = _attend(q_ref[0], kx_ref[0], vx_ref[0], {}, sink_ref).astype(BF16)


def _attention(q, k, v, sink, with_ctx):
    bsz = q.shape[0]
    s = S_ALL if with_ctx else SEQ
    nb = SEQ // BLOCK
    ctx_blk = SEQ // CTX_LEN
    smem = pl.BlockSpec(memory_space=pltpu.SMEM)
    kv = lambda f: pl.BlockSpec((1, BLOCK, KV_WIDTH), f)
    prev = lambda b, n: (b, jnp.maximum(n - 1, 0), 0)
    cur = lambda b, n: (b, n, 0)
    nxt = lambda b, n: (b, jnp.minimum(n + 1, nb - 1), 0)
    ctx = pl.BlockSpec((1, CTX_LEN, KV_WIDTH), lambda b, n: (b, ctx_blk, 0))
    sink8 = jnp.pad(sink, (0, 8 - ATT_HEADS))
    out = pl.pallas_call(
        _attn_lat_kernel,
        grid=(bsz, nb),
        in_specs=[smem, pl.BlockSpec((1, BLOCK, ATT_WIDTH), cur),
                  kv(prev), kv(cur), kv(nxt), ctx, kv(prev), kv(cur), kv(nxt), ctx],
        out_specs=pl.BlockSpec((1, BLOCK, ATT_WIDTH), cur),
        out_shape=jax.ShapeDtypeStruct((bsz, s, ATT_WIDTH), BF16),
        compiler_params=_cparams(("parallel", "arbitrary")),
        name="window_attention",
    )(sink8, q, k, k, k, k, v, v, v, v)
    if not with_ctx:
        return out
    cq = lambda b: (b, ctx_blk, 0)
    return pl.pallas_call(
        _attn_ctx_kernel,
        grid=(bsz,),
        in_specs=[smem, pl.BlockSpec((1, CTX_LEN, ATT_WIDTH), cq),
                  pl.BlockSpec((1, CTX_LEN, KV_WIDTH), cq), pl.BlockSpec((1, CTX_LEN, KV_WIDTH), cq),
                  pl.BlockSpec(memory_space=pl.ANY)],
        out_specs=pl.BlockSpec((1, CTX_LEN, ATT_WIDTH), cq),
        out_shape=jax.ShapeDtypeStruct((bsz, s, ATT_WIDTH), BF16),
        input_output_aliases={4: 0},
        compiler_params=_cparams(("parallel",)),
        name="context_attention",
    )(sink8, q, k, v, out)


HALO = 16
N_SHIFT = CONV_K - 1


def _shift_table():
    i = jnp.arange(CHUNK)[:, None]
    j = jnp.arange(CHUNK + 2 * HALO)[None, :]
    taps = [k for k in range(CONV_K) if k != CONV_K // 2]
    return jnp.concatenate([(j == i + k - CONV_K // 2 + HALO) for k in taps], axis=0).astype(BF16)


def _expand_table(lane0):
    src = jnp.arange(LANE)[:, None] - lane0
    return (src == (jnp.arange(SSD_WIDTH)[None, :] // HEAD_DIM)).astype(BF16)


def _conv_chunk(c, xp_ref, xc_ref, xn_ref, sh_ref, w_ref, b_ref):
    n_lat = SEQ // CHUNK
    last = S_ALL // CHUNK - 1
    has_prev = (c != 0) & (c != n_lat)
    has_next = (c != n_lat - 1) & (c != last)
    zero = jnp.zeros((), BF16)
    cur = xc_ref[0]
    xx = jnp.concatenate([jnp.where(has_prev, xp_ref[0], zero), cur, jnp.where(has_next, xn_ref[0], zero)], axis=0)
    sh = jnp.dot(sh_ref[...], xx, preferred_element_type=F32)
    mid = CONV_K // 2
    acc = b_ref[...] + w_ref[mid:mid + 1, :] * cur.astype(F32)
    for t in range(N_SHIFT):
        k = t if t < mid else t + 1
        acc = acc + w_ref[k:k + 1, :] * sh[t * CHUNK:(t + 1) * CHUNK]
    return _silu(acc)


def _ssd_chunk(xc, dt_raw, dtb, alog, ex_ref, dsk, state_ref, reverse):
    xs = xc[:, :SSD_WIDTH].astype(F32)
    gs = SSD_GROUPS * SSD_STATE
    bm = xc[:, SSD_WIDTH:SSD_WIDTH + gs]
    cm = xc[:, SSD_WIDTH + gs:SSD_WIDTH + 2 * gs]
    raw = dt_raw + dtb
    dt = jnp.maximum(raw, 0.0) + jnp.log(1.0 + jnp.exp(-jnp.abs(raw)))
    dta = dt * (-jnp.exp(alog))
    ri = lax.broadcasted_iota(jnp.int32, (CHUNK, CHUNK), 0)
    ci = lax.broadcasted_iota(jnp.int32, (CHUNK, CHUNK), 1)
    keep = (ri <= ci) if reverse else (ri >= ci)
    tri = keep.astype(BF16)
    tri_t = ((ri >= ci) if reverse else (ri <= ci)).astype(BF16)
    acs = sum(jnp.dot(tri, p, preferred_element_type=F32) for p in _split_bf16(dta, 3))
    acs_t = sum(jnp.dot(p, tri_t, preferred_element_type=F32) for p in _split_bf16(dta.T, 3))
    parts = jnp.concatenate(_split_bf16(dt, 2) + _split_bf16(acs, 3), axis=0)
    wide = jnp.dot(parts, ex_ref[...], preferred_element_type=F32)
    dt_x = wide[:CHUNK] + wide[CHUNK:2 * CHUNK]
    acs_x = wide[2 * CHUNK:3 * CHUNK] + wide[3 * CHUNK:4 * CHUNK] + wide[4 * CHUNK:]
    edge = 0 if reverse else CHUNK - 1
    tot_x = acs_x[edge:edge + 1, :]
    rh = SSD_HEADS // SSD_GROUPS
    gw = rh * HEAD_DIM
    lane0 = SSD_HEADS if reverse else 0
    xd = xs * dt_x
    xdw = (xd * jnp.exp(tot_x - acs_x)).astype(BF16)
    xdb = xd.astype(BF16)
    eacs = jnp.exp(acs_x)
    state = state_ref[...]
    ys = []
    new_state = []
    for g in range(SSD_GROUPS):
        bg = bm[:, g * SSD_STATE:(g + 1) * SSD_STATE]
        cg = cm[:, g * SSD_STATE:(g + 1) * SSD_STATE]
        cb = lax.dot_general(cg, bg, (((1,), (1,)), ((), ())), preferred_element_type=F32)
        sg = state[:, g * gw:(g + 1) * gw]
        y_off = jnp.dot(cg, sg.astype(BF16), preferred_element_type=F32) * eacs[:, g * gw:(g + 1) * gw]
        for r in range(rh):
            h = g * rh + r
            ln = lane0 + h
            dec = jnp.where(keep, jnp.exp(acs[:, ln:ln + 1] - acs_t[ln:ln + 1, :]), 0.0)
            yd = jnp.dot((cb * dec).astype(BF16), xdb[:, h * HEAD_DIM:(h + 1) * HEAD_DIM],
                         preferred_element_type=F32)
            ys.append(yd + y_off[:, r * HEAD_DIM:(r + 1) * HEAD_DIM])
        bt = bg.astype(F32).T.astype(BF16)
        new_state.append(jnp.dot(bt, xdw[:, g * gw:(g + 1) * gw], preferred_element_type=F32))
    state_ref[...] = state * jnp.exp(tot_x) + jnp.concatenate(new_state, axis=1)
    y = jnp.concatenate(ys, axis=1)
    return y if dsk is None else y + dsk * xs


def _ssd_kernel(fp_ref, fc_ref, fn_ref, fdt_ref, bp_ref, bc_ref, bn_ref, bdt_ref,
                sh_ref, w_ref, b_ref, dtb_ref, alog_ref, exf_ref, exb_ref, dsk_ref,
                yf_ref, yb_ref, sf_ref, sb_ref, xcache_ref):
    step = pl.program_id(1)
    nc = pl.num_programs(1)
    n_lat = SEQ // CHUNK

    @pl.when(step == 0)
    def _():
        sf_ref[...] = jnp.zeros_like(sf_ref)
        sb_ref[...] = jnp.zeros_like(sb_ref)

    cf = (step + n_lat) % nc
    cbk = nc - 1 - step

    @pl.when((step != 1) & (step < n_lat // 2 + 2))
    def _():
        xcache_ref[cf] = _conv_chunk(cf, fp_ref, fc_ref, fn_ref, sh_ref, w_ref, b_ref).astype(BF16)
        xcache_ref[cbk] = _conv_chunk(cbk, bp_ref, bc_ref, bn_ref, sh_ref, w_ref, b_ref).astype(BF16)

    dtb, alog = dtb_ref[...], alog_ref[...]
    yf_ref[0] = _ssd_chunk(xcache_ref[cf], fdt_ref[0], dtb, alog, exf_ref, dsk_ref[...], sf_ref,
                           False).astype(BF16)
    yb_ref[0] = _ssd_chunk(xcache_ref[cbk], bdt_ref[0], dtb, alog, exb_ref, None, sb_ref, True).astype(BF16)


def _ssd_mixer(xbc, dt_raw, conv_w, conv_b, dt_bias, a_log, d_skip):
    bsz, s, cdim = xbc.shape
    nc = s // CHUNK
    n_lat = SEQ // CHUNK
    per = CHUNK // HALO
    nh = s // HALO
    assert nc == n_lat + 2 and n_lat % 2 == 0
    fwd = lambda t: (t + n_lat) % nc
    bwd = lambda t: nc - 1 - t

    def chunk_specs(order):
        return [pl.BlockSpec((1, HALO, cdim), lambda b, t: (b, jnp.maximum(order(t) * per - 1, 0), 0)),
                pl.BlockSpec((1, CHUNK, cdim), lambda b, t: (b, order(t), 0)),
                pl.BlockSpec((1, HALO, cdim), lambda b, t: (b, jnp.minimum(order(t) * per + per, nh - 1), 0)),
                pl.BlockSpec((1, CHUNK, LANE), lambda b, t: (b, order(t), 0))]

    c2 = lambda b, t: (0, 0)
    full = lambda a: pl.BlockSpec(a.shape, c2)
    pad = lambda p: jnp.pad(p.reshape(1, -1), ((0, 0), (0, LANE - 2 * SSD_HEADS)))
    consts = [_shift_table(), jnp.pad(conv_w, ((0, 8 - CONV_K), (0, 0))), conv_b.reshape(1, cdim),
              pad(dt_bias), pad(a_log), _expand_table(0), _expand_table(SSD_HEADS),
              jnp.repeat(d_skip, HEAD_DIM).reshape(1, SSD_WIDTH)]
    return pl.pallas_call(
        _ssd_kernel,
        grid=(bsz, nc),
        in_specs=chunk_specs(fwd) + chunk_specs(bwd) + [full(a) for a in consts],
        out_specs=[pl.BlockSpec((1, CHUNK, SSD_WIDTH), lambda b, t: (b, fwd(t), 0)),
                   pl.BlockSpec((1, CHUNK, SSD_WIDTH), lambda b, t: (b, bwd(t), 0))],
        out_shape=[jax.ShapeDtypeStruct((bsz, s, SSD_WIDTH), BF16)] * 2,
        scratch_shapes=[pltpu.VMEM((SSD_STATE, SSD_WIDTH), F32)] * 2 + [pltpu.VMEM((nc, CHUNK, cdim), BF16)],
        compiler_params=_cparams(("parallel", "arbitrary"), VMEM_LIMIT),
        name="conv_ssd_scan",
    )(xbc, xbc, xbc, dt_raw, xbc, xbc, xbc, dt_raw, *consts)


def _out_kernel(*refs, n_h):
    (fo_ref, at_ref, yf_ref, yb_ref, z_ref), refs = refs[:5], refs[5:]
    h_refs, refs = refs[:n_h], refs[n_h:]
    mod_ref, gssd_ref, gpost_ref, gpre_ref, wo_ref, wr_ref, hout_ref, u_ref, lg_ref = refs
    y = (yf_ref[0].astype(F32) + yb_ref[0].astype(F32)) * _silu(z_ref[0].astype(F32))
    sm = _rms(y, gssd_ref[...]).astype(BF16)
    m = (jnp.dot(fo_ref[0], wo_ref[0:256, :], preferred_element_type=F32)
         + jnp.dot(at_ref[0], wo_ref[256:640, :], preferred_element_type=F32)
         + jnp.dot(sm, wo_ref[640:1024, :], preferred_element_type=F32))
    mod = mod_ref[0, 0]
    hn = _stream_tile(h_refs, ROW_TILE) + mod[2:3] * _rms(m, gpost_ref[...])
    hout_ref[0] = hn
    u = _rms(hn, gpre_ref[...]) * (1.0 + mod[4:5]) + mod[3:4]
    half = D_MODEL // 2
    lo = pltpu.bitcast(u[:, :half].astype(BF16).astype(F32), jnp.uint32) >> 16
    hi = pltpu.bitcast(u[:, half:].astype(BF16).astype(F32), jnp.uint32) & jnp.uint32(0xFFFF0000)
    u_ref[0] = lo | hi
    uh, ul = _split_bf16(u, 2)
    wh, wl = wr_ref[0], wr_ref[1]
    nt = (((1,), (1,)), ((), ()))
    lg_ref[0] = (lax.dot_general(wh, uh, nt, preferred_element_type=F32)
                 + lax.dot_general(wh, ul, nt, preferred_element_type=F32)
                 + lax.dot_general(wl, uh, nt, preferred_element_type=F32))


def _out_proj(fo, at, yf, yb, z, h, mod_l, g_ssd, g_post, g_pre, w_out_p, w_router, n_rows):
    bsz = fo.shape[0]
    d = D_MODEL
    tm = ROW_TILE
    n_lat = SEQ // tm
    row = lambda b, i: (b, i, 0)
    c2 = lambda b, i: (0, 0)
    wr = jnp.stack(_split_bf16(w_router.T, 2))
    h_specs, h_args = _stream_specs(h, tm)
    return pl.pallas_call(
        functools.partial(_out_kernel, n_h=len(h_args)),
        grid=(bsz, n_rows // tm),
        in_specs=[pl.BlockSpec((1, tm, F_WIDTH), row), pl.BlockSpec((1, tm, ATT_WIDTH), row),
                  pl.BlockSpec((1, tm, SSD_WIDTH), row), pl.BlockSpec((1, tm, SSD_WIDTH), row),
                  pl.BlockSpec((1, tm, SSD_WIDTH), row)] + h_specs + [
                  pl.BlockSpec((1, 1, 6, d), lambda b, i: (jnp.where(i < n_lat, b, bsz), 0, 0, 0)),
                  pl.BlockSpec((1, SSD_WIDTH), c2), pl.BlockSpec((1, d), c2), pl.BlockSpec((1, d), c2),
                  pl.BlockSpec((d, d), c2), pl.BlockSpec((2, N_EXPERTS, d), lambda b, i: (0, 0, 0))],
        out_specs=[pl.BlockSpec((1, tm, d), row), pl.BlockSpec((1, tm, d // 2), row),
                   pl.BlockSpec((1, N_EXPERTS, tm), lambda b, i: (b, 0, i))],
        out_shape=[jax.ShapeDtypeStruct((bsz, n_rows, d), F32),
                   jax.ShapeDtypeStruct((bsz, n_rows, d // 2), jnp.uint32),
                   jax.ShapeDtypeStruct((bsz, N_EXPERTS, n_rows), F32)],
        compiler_params=_cparams(("parallel", "arbitrary"), VMEM_LIMIT),
        name="outproj_norms_router",
    )(fo, at, yf, yb, z, *h_args, mod_l.reshape(mod_l.shape[0], 1, 6, d), g_ssd.reshape(1, SSD_WIDTH),
      g_post.reshape(1, d), g_pre.reshape(1, d), w_out_p, wr)


def _prefix_lanes(mask, upper):
    n_r, t = mask.shape
    nb = t // LANE
    if (nb * n_r) % LANE:
        carry = jnp.zeros((n_r, 1), F32)
        outs = []
        for j in range(nb):
            pre = jnp.dot(mask[:, j * LANE:(j + 1) * LANE].astype(BF16), upper, preferred_element_type=F32) + carry
            outs.append(pre)
            carry = pre[:, LANE - 1:LANE]
        return jnp.concatenate(outs, axis=1)
    stacked = jnp.concatenate([mask[:, j * LANE:(j + 1) * LANE] for j in range(nb)], axis=0).astype(BF16)
    pre = jnp.dot(stacked, upper, preferred_element_type=F32)
    ri = lax.broadcasted_iota(jnp.int32, (LANE, LANE), 0)
    last_row = (ri == LANE - 1).astype(BF16)
    tot = jnp.dot(pre.astype(BF16), last_row, preferred_element_type=F32)
    ra = lax.broadcasted_iota(jnp.int32, (nb * n_r, nb * n_r), 0)
    ca = lax.broadcasted_iota(jnp.int32, (nb * n_r, nb * n_r), 1)
    assert n_r & (n_r - 1) == 0
    earlier = (((ca ^ ra) & (n_r - 1)) == 0) & (ca < ra)
    earlier = earlier.astype(BF16)
    full = pre + jnp.dot(earlier, tot.astype(BF16), preferred_element_type=F32)
    return jnp.concatenate([full[j * n_r:(j + 1) * n_r] for j in range(nb)], axis=1)


def _slot_tokens(csum, cap):
    n_r, t = csum.shape
    nb = t // LANE
    slot = lax.broadcasted_iota(jnp.int32, (1, cap), 1).astype(F32)
    rows = []
    if nb < 8:
        csum_t = jnp.concatenate([csum, jnp.zeros((LANE - n_r, t), F32)], axis=0).T
        for x in range(n_r):
            rows.append(jnp.sum((csum_t[:, x:x + 1] <= slot).astype(F32), axis=0, keepdims=True))
        return jnp.concatenate(rows, axis=0)
    blk_id = lax.broadcasted_iota(jnp.int32, (LANE, 1), 0).astype(F32)
    never = jnp.full((LANE - nb, LANE), float(t), F32)
    for x in range(n_r):
        cx = jnp.concatenate([csum[x:x + 1, j * LANE:(j + 1) * LANE] for j in range(nb)] + [never], axis=0)
        blk = jnp.sum((cx[:, LANE - 1:LANE] <= slot).astype(F32), axis=0, keepdims=True)
        pick = (blk_id == blk).astype(BF16)
        g = sum(jnp.dot(p, pick, preferred_element_type=F32) for p in _split_bf16(cx.T, 2))
        within = jnp.sum((g <= slot).astype(F32), axis=0, keepdims=True)
        rows.append(blk * LANE + within)
    return jnp.concatenate(rows, axis=0)


def _route_kernel(lg_ref, idx_ref, aff_ref, *, cap):
    l = lg_ref[0]
    n_e, t = l.shape
    e = jnp.exp(l - jnp.max(l, axis=0, keepdims=True))
    aff = e / jnp.sum(e, axis=0, keepdims=True)
    aff_ref[0] = aff
    keys = pltpu.bitcast(aff, jnp.int32)

    def search(it, lo):
        cand = lo | (jnp.int32(1) << (30 - it))
        cnt = jnp.sum((keys >= cand).astype(F32), axis=1, keepdims=True)
        return jnp.where(cnt >= cap, cand, lo)

    thr = lax.fori_loop(0, 31, search, jnp.zeros((n_e, 1), jnp.int32))
    gt = keys > thr
    eq = keys == thr
    need = cap - jnp.sum(gt.astype(F32), axis=1, keepdims=True)
    ri = lax.broadcasted_iota(jnp.int32, (LANE, LANE), 0)
    ci = lax.broadcasted_iota(jnp.int32, (LANE, LANE), 1)
    upper = (ri <= ci).astype(BF16)
    eq_rank = _prefix_lanes(eq.astype(F32), upper)
    sel = gt | (eq & (eq_rank <= need))
    csum = _prefix_lanes(sel.astype(F32), upper)
    idx_ref[0] = _slot_tokens(csum, cap).astype(jnp.int32)


def _route(logits_t, row0, t, cap):
    bsz, n_e, _ = logits_t.shape
    return pl.pallas_call(
        functools.partial(_route_kernel, cap=cap),
        grid=(bsz,),
        in_specs=[pl.BlockSpec((1, n_e, t), lambda b: (b, 0, row0 // t))],
        out_specs=[pl.BlockSpec((1, n_e, cap), lambda b: (b, 0, 0)),
                   pl.BlockSpec((1, n_e, t), lambda b: (b, 0, 0))],
        out_shape=[jax.ShapeDtypeStruct((bsz, n_e, cap), jnp.int32),
                   jax.ShapeDtypeStruct((bsz, n_e, t), F32)],
        compiler_params=_cparams(("parallel",), VMEM_LIMIT),
        name="expert_choice_route",
    )(logits_t)


def _smem_row(n):
    return pl.BlockSpec((1, 1, 1, n), lambda b, e: (b, e, 0, 0), memory_space=pltpu.SMEM)


def _gather_kernel(idx_ref, u_ref, o_ref, *, cap):
    def body(i, carry):
        t = idx_ref[0, 0, 0, i]
        o_ref[0, pl.ds(i, 1), :] = u_ref[0, pl.ds(t, 1), :]
        return carry

    lax.fori_loop(0, cap, body, 0, unroll=8)


def _gather(u_packed, idx, row0, t, cap):
    bsz, _, half = u_packed.shape
    n_e = idx.shape[1]
    return pl.pallas_call(
        functools.partial(_gather_kernel, cap=cap),
        grid=(bsz, n_e),
        in_specs=[_smem_row(cap), pl.BlockSpec((1, t, half), lambda b, e: (b, row0 // t, 0))],
        out_specs=pl.BlockSpec((1, cap, half), lambda b, e: (e, b, 0)),
        out_shape=jax.ShapeDtypeStruct((n_e, bsz * cap, half), jnp.uint32),
        compiler_params=_cparams(("parallel", "arbitrary"), VMEM_LIMIT),
        name="expert_gather",
    )(idx[:, :, None, :], u_packed)


FF_TILE = 256


def _ffn_kernel(x_ref, wg_ref, wu_ref, wd_ref, o_ref, xs_ref):
    f = pl.program_id(2)

    @pl.when(f == 0)
    def _():
        w = x_ref[0]
        half = w.shape[1]
        xs_ref[:, :half] = pltpu.bitcast(w << 16, F32).astype(BF16)
        xs_ref[:, half:] = pltpu.bitcast(w & jnp.uint32(0xFFFF0000), F32).astype(BF16)

    x = xs_ref[...]
    g = jnp.dot(x, wg_ref[0, 0].astype(BF16), preferred_element_type=F32)
    u = jnp.dot(x, wu_ref[0, 0].astype(BF16), preferred_element_type=F32)
    y = jnp.dot((_silu(g) * u).astype(BF16), wd_ref[0, 0].astype(BF16), preferred_element_type=F32)

    @pl.when(f == 0)
    def _():
        o_ref[0] = y

    @pl.when(f > 0)
    def _():
        o_ref[0] += y


def _expert_ffn(xe, w_gate, w_up, w_down, layer):
    n_e, m, half = xe.shape
    d = 2 * half
    ff = w_gate.shape[-1]
    tm = min(m, 2048)
    return pl.pallas_call(
        _ffn_kernel,
        grid=(n_e, m // tm, ff // FF_TILE),
        in_specs=[pl.BlockSpec((1, tm, half), lambda e, i, f: (e, i, 0)),
                  pl.BlockSpec((1, 1, d, FF_TILE), lambda e, i, f: (layer, e, 0, f)),
                  pl.BlockSpec((1, 1, d, FF_TILE), lambda e, i, f: (layer, e, 0, f)),
                  pl.BlockSpec((1, 1, FF_TILE, d), lambda e, i, f: (layer, e, f, 0))],
        out_specs=pl.BlockSpec((1, tm, d), lambda e, i, f: (e, i, 0)),
        out_shape=jax.ShapeDtypeStruct((n_e, m, d), F32),
        scratch_shapes=[pltpu.VMEM((tm, d), BF16)],
        compiler_params=_cparams(("parallel", "parallel", "arbitrary"), VMEM_LIMIT),
        name="expert_ffn",
    )(xe, w_gate, w_up, w_down)


COMB_ROWS = 8


def _combine_kernel(idx_ref, aff_ref, y_ref, *rest, cap):
    o_ref = rest[-1]

    @pl.when(pl.program_id(1) == 0)
    def _():
        o_ref[...] = jnp.zeros_like(o_ref)

    def body(c, carry):
        base = pl.multiple_of(c * COMB_ROWS, COMB_ROWS)
        toks = [idx_ref[0, 0, 0, base + r] for r in range(COMB_ROWS)]
        rows = [o_ref[0, pl.ds(t, 1), :] + aff_ref[0, 0, 0, t] * y_ref[0, pl.ds(base + r, 1), :]
                for r, t in enumerate(toks)]
        for t, row in zip(toks, rows):
            o_ref[0, pl.ds(t, 1), :] = row
        return carry

    lax.fori_loop(0, cap // COMB_ROWS, body, 0)


def _combine(y, idx, aff, row0, t, cap, s_total, out_prev=None):
    n_e, m, d = y.shape
    bsz = idx.shape[0]
    in_specs = [_smem_row(cap), _smem_row(t), pl.BlockSpec((1, cap, d), lambda b, e: (e, b, 0))]
    args = [idx[:, :, None, :], aff[:, :, None, :], y]
    aliases = {}
    if out_prev is not None:
        in_specs.append(pl.BlockSpec(memory_space=pl.ANY))
        args.append(out_prev)
        aliases = {3: 0}
    return pl.pallas_call(
        functools.partial(_combine_kernel, cap=cap),
        grid=(bsz, n_e),
        in_specs=in_specs,
        out_specs=pl.BlockSpec((1, t, d), lambda b, e: (b, row0 // t, 0)),
        out_shape=jax.ShapeDtypeStruct((bsz, s_total, d), F32),
        input_output_aliases=aliases,
        compiler_params=_cparams(("parallel", "arbitrary"), VMEM_LIMIT),
        name="expert_combine",
    )(*args)


def _moe(u_packed, logits_t, w_gate, w_up, w_down, layer, with_ctx):
    bsz, rows, _ = u_packed.shape
    out = None
    sets = [(0, SEQ)] + ([(SEQ, CTX_LEN)] if with_ctx else [])
    for row0, t in sets:
        cap = EC_CAPACITY * t // N_EXPERTS
        idx, aff = _route(logits_t, row0, t, cap)
        xe = _gather(u_packed, idx, row0, t, cap)
        y = _expert_ffn(xe, w_gate, w_up, w_down, layer)
        out = _combine(y, idx, aff, row0, t, cap, rows, out)
    return out


def _final_kernel(h_ref, f_ref, mod_ref, g_ref, o_ref):
    o_ref[0] = h_ref[0] + mod_ref[0, 0][5:6] * _rms(f_ref[0], g_ref[...])


def _final_residual(h, f, mod_l, g_post):
    bsz, n, d = h.shape
    tm = ROW_TILE
    row = lambda b, i: (b, i, 0)
    return pl.pallas_call(
        _final_kernel,
        grid=(bsz, n // tm),
        in_specs=[pl.BlockSpec((1, tm, d), row), pl.BlockSpec((1, tm, d), row),
                  pl.BlockSpec((1, 1, 6, d), lambda b, i: (b, 0, 0, 0)), pl.BlockSpec((1, d), lambda b, i: (0, 0))],
        out_specs=pl.BlockSpec((1, tm, d), row),
        out_shape=jax.ShapeDtypeStruct((bsz, n, d), F32),
        compiler_params=_cparams(("parallel", "arbitrary")),
        name="ffn_residual",
    )(h, f, mod_l.reshape(mod_l.shape[0], 1, 6, d), g_post.reshape(1, d))


def _rope_tables():
    n = jnp.arange(SEQ, dtype=jnp.int32)
    row = (n // GRID_W).astype(F32)
    col = (n % GRID_W).astype(F32)
    quarter = HEAD_DIM // 4
    inv = ROPE_THETA ** (-jnp.arange(quarter, dtype=F32) / quarter)
    lane = jnp.arange(LANE)
    pos = jnp.where((lane % HEAD_DIM < HEAD_DIM // 2)[None, :], row[:, None], col[:, None])
    ang = pos * inv[lane % quarter][None, :]
    first = (lane % (2 * quarter) < quarter)[None, :]
    cos = jnp.cos(ang)
    sin = jnp.sin(ang)
    sa = jnp.where(first, -sin, 0.0)
    sb = jnp.where(first, 0.0, sin)
    padc = lambda t, v: jnp.concatenate([t, jnp.full((CTX_LEN, LANE), v, F32)], axis=0)
    return padc(cos, 1.0), padc(sa, 0.0), padc(sb, 0.0)


def kernel(x, c, ctx, c_ctx, w_ada, b_ada, g_mix_pre, g_mix_post, g_ffn_pre, g_ffn_post, w_in, conv_w, conv_b,
           dt_bias, a_log, d_skip, g_ssd, attn_sink, w_out, w_router, w_gate, w_up, w_down):
    depth = w_ada.shape[0]
    bsz = x.shape[0]
    d = D_MODEL
    cc = jnp.concatenate([c, c_ctx[None, :], jnp.zeros((7, d), F32)], axis=0)
    mod = _modulation(cc, w_ada, b_ada).reshape(depth, bsz + 8, 6, d)
    rope_tabs = _rope_tables()
    chan_tab = _channel_tables()
    cn_lat, sn_lat = _dft_tables(SEQ)
    cn_ctx, sn_ctx = _dft_tables(CTX_LEN)
    q0 = F_WIDTH

    h = (x, ctx)
    f_prev = None
    for l in range(depth):
        last = l == depth - 1
        w_q = _to_slab_order(w_in[l][:, q0:q0 + ATT_WIDTH], 1) * HEAD_DIM ** -0.5
        w_in_p = jnp.concatenate([w_in[l][:, :q0], w_q, w_in[l][:, q0 + ATT_WIDTH:],
                                  jnp.zeros((d, IN_PAD - IN_WIDTH), F32)], axis=1).astype(BF16)
        w_out_p = jnp.concatenate([w_out[l][:q0], _to_slab_order(w_out[l][q0:q0 + ATT_WIDTH], 0),
                                   w_out[l][q0 + ATT_WIDTH:]], axis=0).astype(BF16)
        outs = _in_proj(h, mod[l], mod[l - 1] if l else None, g_mix_pre[l], w_in_p, rope_tabs,
                        f_prev, g_ffn_post[l - 1] if l else None)
        if f_prev is not None:
            h, outs = outs[0], outs[1:]
        fo, q, k, v, z, xbc, dt_raw = outs
        fmix = _fourier(fo, chan_tab, cn_lat, sn_lat, 0, SEQ, out_rows=SEQ if last else S_ALL)
        if not last:
            fmix = _fourier(fo, chan_tab, cn_ctx, sn_ctx, SEQ, CTX_LEN, fmix)
        att = _attention(q, k, v, attn_sink[l], not last)
        y_f, y_b = _ssd_mixer(xbc, dt_raw, conv_w[l], conv_b[l], dt_bias[l], a_log[l], d_skip[l])
        n_rows = SEQ if last else S_ALL
        h, u_packed, logits_t = _out_proj(fmix, att, y_f, y_b, z, h, mod[l], g_ssd[l],
                                          g_mix_post[l], g_ffn_pre[l], w_out_p, w_router[l], n_rows)
        f_prev = _moe(u_packed, logits_t, w_gate, w_up, w_down, l, not last)
    return _final_residual(h, f_prev, mod[depth - 1], g_ffn_post[depth - 1])
```

```python
import functools
import math

import jax
import jax.numpy as jnp
from jax import lax
from jax.experimental import pallas as pl
from jax.experimental.pallas import tpu as pltpu

F32 = jnp.float32
BF16 = jnp.bfloat16

D_MODEL = 1024
SEQ = 4096
CTX_LEN = 256
S_ALL = SEQ + CTX_LEN
GRID_W = 64
HEAD_DIM = 64
EPS = 1e-6
F_WIDTH = 256
ATT_HEADS = 6
ATT_KV_HEADS = 2
ATT_WIDTH = 384
KV_WIDTH = 128
BLOCK = 128
ROPE_THETA = 10000.0
SSD_HEADS = 6
SSD_WIDTH = 384
SSD_GROUPS = 2
SSD_STATE = 128
CHUNK = 128
CONV_K = 5
CONV_DIM = 896
IN_WIDTH = 2188
IN_PAD = 2304
N_EXPERTS = 16
EXPERT_FF = 2816
EC_CAPACITY = 2

LANE = 128
ROW_TILE = 512
VMEM_LIMIT = 56 * 1024 * 1024


def _cparams(sem, vmem=None):
    return pltpu.CompilerParams(dimension_semantics=sem, vmem_limit_bytes=vmem)


def _split_bf16(x, n):
    parts, r = [], x
    for _ in range(n):
        p = r.astype(BF16)
        parts.append(p)
        r = r - p.astype(F32)
    return parts


def _rms(x, g):
    return x * lax.rsqrt(jnp.mean(x * x, axis=-1, keepdims=True) + EPS) * g


def _silu(x):
    return x * jax.nn.sigmoid(x)


def _mod_kernel(cc_ref, w_ref, b_ref, o_ref):
    s = _silu(cc_ref[...])
    o_ref[0] = jnp.dot(s.astype(BF16), w_ref[0].astype(BF16), preferred_element_type=F32) + b_ref[0]


def _modulation(cc, w_ada, b_ada):
    n_l, d, d6 = w_ada.shape
    r = cc.shape[0]
    tn = 1024
    return pl.pallas_call(
        _mod_kernel,
        grid=(n_l, d6 // tn),
        in_specs=[pl.BlockSpec((r, d), lambda l, j: (0, 0)),
                  pl.BlockSpec((1, d, tn), lambda l, j: (l, 0, j)),
                  pl.BlockSpec((1, 1, tn), lambda l, j: (l, 0, j))],
        out_specs=pl.BlockSpec((1, r, tn), lambda l, j: (l, 0, j)),
        out_shape=jax.ShapeDtypeStruct((n_l, r, d6), F32),
        compiler_params=_cparams(("arbitrary", "arbitrary")),
        name="adaln_mod",
    )(cc, w_ada, b_ada.reshape(n_l, 1, d6))


def _stream_specs(h, tm):
    n_lat = SEQ // tm
    if isinstance(h, tuple):
        lat, ctx = h
        d = lat.shape[-1]
        return ([pl.BlockSpec((1, tm, d), lambda b, i: (b, jnp.minimum(i, n_lat - 1), 0)),
                 pl.BlockSpec((1, CTX_LEN, d), lambda b, i: (b, 0, 0))], [lat, ctx])
    return [pl.BlockSpec((1, tm, h.shape[-1]), lambda b, i: (b, i, 0))], [h]


def _stream_tile(refs, tm):
    if len(refs) == 2:
        ctx = jnp.concatenate([refs[1][0]] * (tm // CTX_LEN), axis=0)
        return jnp.where(pl.program_id(1) < SEQ // tm, refs[0][0], ctx)
    return refs[0][0]


def _in_kernel(*refs, residual, n_h):
    h_refs, refs = refs[:n_h], refs[n_h:]
    if residual:
        (f_ref, gpost_ref, mod_ref, g_ref, w_ref, cos_ref, sa_ref, sb_ref,
         hout_ref, fo_ref, q_ref, k_ref, v_ref, z_ref, xbc_ref, dt_ref) = refs
    else:
        (mod_ref, g_ref, w_ref, cos_ref, sa_ref, sb_ref,
         fo_ref, q_ref, k_ref, v_ref, z_ref, xbc_ref, dt_ref) = refs
    x = _stream_tile(h_refs, ROW_TILE)
    mod = mod_ref[0, 0]
    if residual:
        x = x + mod[5:6] * _rms(f_ref[0], gpost_ref[...])
        hout_ref[0] = x
    u = _rms(x, g_ref[...]) * (1.0 + mod[1:2]) + mod[0:1]
    p = jnp.dot(u.astype(BF16), w_ref[...], preferred_element_type=F32)
    cos, sa, sb = cos_ref[...], sa_ref[...], sb_ref[...]

    def rope(t):
        return t * cos + pltpu.roll(t, LANE - 16, 1) * sa + pltpu.roll(t, 16, 1) * sb

    fo_ref[0] = p[:, 0:256].astype(BF16)
    q_ref[0] = jnp.concatenate([rope(p[:, 256 + LANE * s:256 + LANE * (s + 1)]) for s in range(3)],
                               axis=1).astype(BF16)
    k_ref[0] = rope(p[:, 640:768]).astype(BF16)
    v_ref[0] = p[:, 768:896].astype(BF16)
    z_ref[0] = p[:, 896:1280].astype(BF16)
    xbc_ref[0] = p[:, 1280:2176].astype(BF16)
    dt_ref[0] = p[:, 2176:2304]


def _in_proj(h, mod_l, mod_prev, g_pre, w_in_p, rope_tabs, f_prev=None, g_post_prev=None):
    bsz = mod_l.shape[0] - 8
    s, d = S_ALL, D_MODEL
    tm = ROW_TILE
    n_lat = SEQ // tm
    residual = f_prev is not None
    row = lambda b, i: (b, i, 0)
    mod_map = lambda b, i: (jnp.where(i < n_lat, b, bsz), 0, 0, 0)
    const2 = lambda b, i: (0, 0)
    tab = pl.BlockSpec((tm, LANE), lambda b, i: (i, 0))
    in_specs, args = _stream_specs(h, tm)
    n_h = len(args)
    if residual:
        in_specs += [pl.BlockSpec((1, tm, d), row), pl.BlockSpec((1, d), const2)]
        args += [f_prev, g_post_prev.reshape(1, d)]
    mod_used = mod_l if not residual else jnp.concatenate([mod_l[:, :5], mod_prev[:, 5:6]], axis=1)
    in_specs += [pl.BlockSpec((1, 1, 6, d), mod_map), pl.BlockSpec((1, d), const2),
                 pl.BlockSpec((d, IN_PAD), const2), tab, tab, tab]
    args += [mod_used.reshape(mod_used.shape[0], 1, 6, d), g_pre.reshape(1, d), w_in_p, *rope_tabs]
    widths = [(256, BF16), (384, BF16), (128, BF16), (128, BF16), (384, BF16), (896, BF16), (128, F32)]
    out_specs = [pl.BlockSpec((1, tm, w), row) for w, _ in widths]
    out_shape = [jax.ShapeDtypeStruct((bsz, s, w), dt) for w, dt in widths]
    if residual:
        out_specs = [pl.BlockSpec((1, tm, d), row)] + out_specs
        out_shape = [jax.ShapeDtypeStruct((bsz, s, d), F32)] + out_shape
    return pl.pallas_call(
        functools.partial(_in_kernel, residual=residual, n_h=n_h),
        grid=(bsz, pl.cdiv(s, tm)),
        in_specs=in_specs, out_specs=out_specs, out_shape=out_shape,
        compiler_params=_cparams(("parallel", "arbitrary"), VMEM_LIMIT),
        name="prenorm_inproj",
    )(*args)


def _f1_kernel(u_ref, w_ref, a_ref, b_ref):
    r = jnp.dot(u_ref[0], w_ref[...], preferred_element_type=F32)
    a_ref[...] = r[:, :F_WIDTH].astype(BF16)
    b_ref[...] = r[:, F_WIDTH:].astype(BF16)


def _f2_kernel(c_ref, s_ref, a_ref, b_ref, *rest):
    o_ref = rest[-1]
    y = (jnp.dot(c_ref[...], a_ref[...], preferred_element_type=F32)
         - jnp.dot(s_ref[...], b_ref[...], preferred_element_type=F32))
    o_ref[0] = y.astype(BF16)


def _dft_tables(n):
    sc = n ** -0.5

    def cs(rows, cols, period):
        m = (rows[:, None] * cols[None, :]) % period
        ang = m.astype(F32) * (2.0 * math.pi / period)
        return jnp.cos(ang), jnp.sin(ang)

    k = jnp.arange(n, dtype=jnp.int32)
    r = 64
    if n <= r * r // 16:
        c, s = cs(k, k, n)
        return (c * sc).astype(BF16), (s * sc).astype(BF16)
    ca, sa = cs(k, jnp.arange(n // r, dtype=jnp.int32), n // r)
    cb, sb = cs(k, jnp.arange(r, dtype=jnp.int32), n)
    c = ca[:, :, None] * cb[:, None, :] - sa[:, :, None] * sb[:, None, :]
    s = sa[:, :, None] * cb[:, None, :] + ca[:, :, None] * sb[:, None, :]
    return (c * sc).reshape(n, n).astype(BF16), (s * sc).reshape(n, n).astype(BF16)


def _channel_tables():
    c64, s64 = _dft_tables(HEAD_DIM)
    eye = jnp.eye(F_WIDTH // HEAD_DIM, dtype=BF16)
    return jnp.concatenate([jnp.kron(eye, c64), jnp.kron(eye, s64)], axis=1)


def _fourier(fo, chan_tab, cn, sn, row0, n, out_prev=None, out_rows=S_ALL):
    bsz, s, _ = fo.shape
    tm = min(n, 512)
    nt = n // tm
    blk0 = row0 // tm
    a_all, b_all = pl.pallas_call(
        _f1_kernel,
        grid=(bsz, nt),
        in_specs=[pl.BlockSpec((1, tm, F_WIDTH), lambda b, i: (b, blk0 + i, 0)),
                  pl.BlockSpec((F_WIDTH, 2 * F_WIDTH), lambda b, i: (0, 0))],
        out_specs=[pl.BlockSpec((tm, F_WIDTH), lambda b, i: (i, b))] * 2,
        out_shape=[jax.ShapeDtypeStruct((n, bsz * F_WIDTH), BF16)] * 2,
        compiler_params=_cparams(("parallel", "arbitrary")),
        name="fourier_channels",
    )(fo, chan_tab)
    in_specs = [pl.BlockSpec((tm, n), lambda i, b: (i, 0)),
                pl.BlockSpec((tm, n), lambda i, b: (i, 0)),
                pl.BlockSpec((n, F_WIDTH), lambda i, b: (0, b)),
                pl.BlockSpec((n, F_WIDTH), lambda i, b: (0, b))]
    args = [cn, sn, a_all, b_all]
    aliases = {}
    if out_prev is not None:
        in_specs.append(pl.BlockSpec(memory_space=pl.ANY))
        args.append(out_prev)
        aliases = {4: 0}
    return pl.pallas_call(
        _f2_kernel,
        grid=(nt, bsz),
        in_specs=in_specs,
        out_specs=pl.BlockSpec((1, tm, F_WIDTH), lambda i, b: (b, blk0 + i, 0)),
        out_shape=jax.ShapeDtypeStruct((bsz, out_rows, F_WIDTH), BF16),
        input_output_aliases=aliases,
        compiler_params=_cparams(("parallel", "arbitrary"), VMEM_LIMIT),
        name="fourier_positions",
    )(*args)


def _to_slab_order(w, axis):
    r = ATT_HEADS // ATT_KV_HEADS
    shape = w.shape
    w = w.reshape(shape[:axis] + (ATT_KV_HEADS, r, HEAD_DIM) + shape[axis + 1:])
    return jnp.swapaxes(w, axis, axis + 1).reshape(shape)


def _attend(q, k, v, masks, sink_ref):
    n = q.shape[0]
    r = ATT_HEADS // ATT_KV_HEADS
    lo = lax.broadcasted_iota(jnp.int32, (1, LANE), 1) < HEAD_DIM
    rid = lax.broadcasted_iota(jnp.int32, (r * n, 1), 0)
    outs = []
    for g in range(ATT_KV_HEADS):
        keep = lo if g == 0 else jnp.logical_not(lo)
        qg = jnp.concatenate([jnp.where(keep, q[:, LANE * s:LANE * (s + 1)], jnp.zeros((), BF16))
                              for s in range(r)], axis=0)
        snk = jnp.full((r * n, 1), sink_ref[r * g], F32)
        for j in range(1, r):
            snk = jnp.where(rid >= j * n, sink_ref[r * g + j], snk)
        sc = lax.dot_general(qg, k, (((1,), (1,)), ((), ())), preferred_element_type=F32)
        if masks:
            nblk = sc.shape[1] // LANE
            sc = jnp.concatenate(
                [jnp.where(masks[b], sc[:, b * LANE:(b + 1) * LANE], -1e30) if b in masks
                 else sc[:, b * LANE:(b + 1) * LANE] for b in range(nblk)], axis=1)
        mx = jnp.maximum(jnp.max(sc, axis=-1, keepdims=True), snk)
        p = jnp.exp(sc - mx)
        den = jnp.sum(p, axis=-1, keepdims=True) + jnp.exp(snk - mx)
        outs.append(jnp.dot(p.astype(BF16), v, preferred_element_type=F32) / den)
    return jnp.concatenate([jnp.where(lo, outs[0][s * n:(s + 1) * n], outs[1][s * n:(s + 1) * n])
                            for s in range(r)], axis=1)


def _attn_lat_kernel(sink_ref, q_ref, kp_ref, kc_ref, kn_ref, kx_ref, vp_ref, vc_ref, vn_ref, vx_ref, o_ref):
    nb = pl.num_programs(1)
    n = pl.program_id(1)
    rows = (ATT_HEADS // ATT_KV_HEADS) * BLOCK
    i = lax.broadcasted_iota(jnp.int32, (rows, BLOCK), 0) & (BLOCK - 1)
    j = lax.broadcasted_iota(jnp.int32, (rows, BLOCK), 1)
    m_prev = (j >= i) & (n > 0)
    m_next = (j <= i) & (n < nb - 1)
    k = jnp.concatenate([kp_ref[0], kc_ref[0], kn_ref[0], kx_ref[0]], axis=0)
    v = jnp.concatenate([vp_ref[0], vc_ref[0], vn_ref[0], vx_ref[0]], axis=0)
    o_ref[0] = _attend(q_ref[0], k, v, {0: m_prev, 2: m_next}, sink_ref).astype(BF16)


def _attn_ctx_kernel(sink_ref, q_ref, kx_ref, vx_ref, prev_ref, o_ref):
    o_ref[0] = _attend(q_ref[0], kx_ref[0], vx_ref[0], {}, sink_ref).astype(BF16)


def _attention(q, k, v, sink, with_ctx):
    bsz = q.shape[0]
    s = S_ALL if with_ctx else SEQ
    nb = SEQ // BLOCK
    ctx_blk = SEQ // CTX_LEN
    smem = pl.BlockSpec(memory_space=pltpu.SMEM)
    kv = lambda f: pl.BlockSpec((1, BLOCK, KV_WIDTH), f)
    prev = lambda b, n: (b, jnp.maximum(n - 1, 0), 0)
    cur = lambda b, n: (b, n, 0)
    nxt = lambda b, n: (b, jnp.minimum(n + 1, nb - 1), 0)
    ctx = pl.BlockSpec((1, CTX_LEN, KV_WIDTH), lambda b, n: (b, ctx_blk, 0))
    sink8 = jnp.pad(sink, (0, 8 - ATT_HEADS))
    out = pl.pallas_call(
        _attn_lat_kernel,
        grid=(bsz, nb),
        in_specs=[smem, pl.BlockSpec((1, BLOCK, ATT_WIDTH), cur),
                  kv(prev), kv(cur), kv(nxt), ctx, kv(prev), kv(cur), kv(nxt), ctx],
        out_specs=pl.BlockSpec((1, BLOCK, ATT_WIDTH), cur),
        out_shape=jax.ShapeDtypeStruct((bsz, s, ATT_WIDTH), BF16),
        compiler_params=_cparams(("parallel", "arbitrary")),
        name="window_attention",
    )(sink8, q, k, k, k, k, v, v, v, v)
    if not with_ctx:
        return out
    cq = lambda b: (b, ctx_blk, 0)
    return pl.pallas_call(
        _attn_ctx_kernel,
        grid=(bsz,),
        in_specs=[smem, pl.BlockSpec((1, CTX_LEN, ATT_WIDTH), cq),
                  pl.BlockSpec((1, CTX_LEN, KV_WIDTH), cq), pl.BlockSpec((1, CTX_LEN, KV_WIDTH), cq),
                  pl.BlockSpec(memory_space=pl.ANY)],
        out_specs=pl.BlockSpec((1, CTX_LEN, ATT_WIDTH), cq),
        out_shape=jax.ShapeDtypeStruct((bsz, s, ATT_WIDTH), BF16),
        input_output_aliases={4: 0},
        compiler_params=_cparams(("parallel",)),
        name="context_attention",
    )(sink8, q, k, v, out)


HALO = 16
N_SHIFT = CONV_K - 1


def _shift_table():
    i = jnp.arange(CHUNK)[:, None]
    j = jnp.arange(CHUNK + 2 * HALO)[None, :]
    taps = [k for k in range(CONV_K) if k != CONV_K // 2]
    return jnp.concatenate([(j == i + k - CONV_K // 2 + HALO) for k in taps], axis=0).astype(BF16)


def _expand_table(lane0):
    src = jnp.arange(LANE)[:, None] - lane0
    return (src == (jnp.arange(SSD_WIDTH)[None, :] // HEAD_DIM)).astype(BF16)


def _conv_chunk(c, xp_ref, xc_ref, xn_ref, sh_ref, w_ref, b_ref):
    n_lat = SEQ // CHUNK
    last = S_ALL // CHUNK - 1
    has_prev = (c != 0) & (c != n_lat)
    has_next = (c != n_lat - 1) & (c != last)
    zero = jnp.zeros((), BF16)
    cur = xc_ref[0]
    xx = jnp.concatenate([jnp.where(has_prev, xp_ref[0], zero), cur, jnp.where(has_next, xn_ref[0], zero)], axis=0)
    sh = jnp.dot(sh_ref[...], xx, preferred_element_type=F32)
    mid = CONV_K // 2
    acc = b_ref[...] + w_ref[mid:mid + 1, :] * cur.astype(F32)
    for t in range(N_SHIFT):
        k = t if t < mid else t + 1
        acc = acc + w_ref[k:k + 1, :] * sh[t * CHUNK:(t + 1) * CHUNK]
    return _silu(acc)


def _ssd_chunk(xc, dt_raw, dtb, alog, ex_ref, dsk, state_ref, reverse):
    xs = xc[:, :SSD_WIDTH].astype(F32)
    gs = SSD_GROUPS * SSD_STATE
    bm = xc[:, SSD_WIDTH:SSD_WIDTH + gs]
    cm = xc[:, SSD_WIDTH + gs:SSD_WIDTH + 2 * gs]
    raw = dt_raw + dtb
    dt = jnp.maximum(raw, 0.0) + jnp.log(1.0 + jnp.exp(-jnp.abs(raw)))
    dta = dt * (-jnp.exp(alog))
    ri = lax.broadcasted_iota(jnp.int32, (CHUNK, CHUNK), 0)
    ci = lax.broadcasted_iota(jnp.int32, (CHUNK, CHUNK), 1)
    keep = (ri <= ci) if reverse else (ri >= ci)
    tri = keep.astype(BF16)
    tri_t = ((ri >= ci) if reverse else (ri <= ci)).astype(BF16)
    acs = sum(jnp.dot(tri, p, preferred_element_type=F32) for p in _split_bf16(dta, 3))
    acs_t = sum(jnp.dot(p, tri_t, preferred_element_type=F32) for p in _split_bf16(dta.T, 3))
    parts = jnp.concatenate(_split_bf16(dt, 2) + _split_bf16(acs, 3), axis=0)
    wide = jnp.dot(parts, ex_ref[...], preferred_element_type=F32)
    dt_x = wide[:CHUNK] + wide[CHUNK:2 * CHUNK]
    acs_x = wide[2 * CHUNK:3 * CHUNK] + wide[3 * CHUNK:4 * CHUNK] + wide[4 * CHUNK:]
    edge = 0 if reverse else CHUNK - 1
    tot_x = acs_x[edge:edge + 1, :]
    rh = SSD_HEADS // SSD_GROUPS
    gw = rh * HEAD_DIM
    lane0 = SSD_HEADS if reverse else 0
    xd = xs * dt_x
    xdw = (xd * jnp.exp(tot_x - acs_x)).astype(BF16)
    xdb = xd.astype(BF16)
    eacs = jnp.exp(acs_x)
    state = state_ref[...]
    ys = []
    new_state = []
    for g in range(SSD_GROUPS):
        bg = bm[:, g * SSD_STATE:(g + 1) * SSD_STATE]
        cg = cm[:, g * SSD_STATE:(g + 1) * SSD_STATE]
        cb = lax.dot_general(cg, bg, (((1,), (1,)), ((), ())), preferred_element_type=F32)
        sg = state[:, g * gw:(g + 1) * gw]
        y_off = jnp.dot(cg, sg.astype(BF16), preferred_element_type=F32) * eacs[:, g * gw:(g + 1) * gw]
        for r in range(rh):
            h = g * rh + r
            ln = lane0 + h
            dec = jnp.where(keep, jnp.exp(acs[:, ln:ln + 1] - acs_t[ln:ln + 1, :]), 0.0)
            yd = jnp.dot((cb * dec).astype(BF16), xdb[:, h * HEAD_DIM:(h + 1) * HEAD_DIM],
                         preferred_element_type=F32)
            ys.append(yd + y_off[:, r * HEAD_DIM:(r + 1) * HEAD_DIM])
        bt = bg.astype(F32).T.astype(BF16)
        new_state.append(jnp.dot(bt, xdw[:, g * gw:(g + 1) * gw], preferred_element_type=F32))
    state_ref[...] = state * jnp.exp(tot_x) + jnp.concatenate(new_state, axis=1)
    y = jnp.concatenate(ys, axis=1)
    return y if dsk is None else y + dsk * xs


def _ssd_kernel(fp_ref, fc_ref, fn_ref, fdt_ref, bp_ref, bc_ref, bn_ref, bdt_ref,
                sh_ref, w_ref, b_ref, dtb_ref, alog_ref, exf_ref, exb_ref, dsk_ref,
                yf_ref, yb_ref, sf_ref, sb_ref, xcache_ref):
    step = pl.program_id(1)
    nc = pl.num_programs(1)
    n_lat = SEQ // CHUNK

    @pl.when(step == 0)
    def _():
        sf_ref[...] = jnp.zeros_like(sf_ref)
        sb_ref[...] = jnp.zeros_like(sb_ref)

    cf = (step + n_lat) % nc
    cbk = nc - 1 - step

    @pl.when((step != 1) & (step < n_lat // 2 + 2))
    def _():
        xcache_ref[cf] = _conv_chunk(cf, fp_ref, fc_ref, fn_ref, sh_ref, w_ref, b_ref).astype(BF16)
        xcache_ref[cbk] = _conv_chunk(cbk, bp_ref, bc_ref, bn_ref, sh_ref, w_ref, b_ref).astype(BF16)

    dtb, alog = dtb_ref[...], alog_ref[...]
    yf_ref[0] = _ssd_chunk(xcache_ref[cf], fdt_ref[0], dtb, alog, exf_ref, dsk_ref[...], sf_ref,
                           False).astype(BF16)
    yb_ref[0] = _ssd_chunk(xcache_ref[cbk], bdt_ref[0], dtb, alog, exb_ref, None, sb_ref, True).astype(BF16)


def _ssd_mixer(xbc, dt_raw, conv_w, conv_b, dt_bias, a_log, d_skip):
    bsz, s, cdim = xbc.shape
    nc = s // CHUNK
    n_lat = SEQ // CHUNK
    per = CHUNK // HALO
    nh = s // HALO
    assert nc == n_lat + 2 and n_lat % 2 == 0
    fwd = lambda t: (t + n_lat) % nc
    bwd = lambda t: nc - 1 - t

    def chunk_specs(order):
        return [pl.BlockSpec((1, HALO, cdim), lambda b, t: (b, jnp.maximum(order(t) * per - 1, 0), 0)),
                pl.BlockSpec((1, CHUNK, cdim), lambda b, t: (b, order(t), 0)),
                pl.BlockSpec((1, HALO, cdim), lambda b, t: (b, jnp.minimum(order(t) * per + per, nh - 1), 0)),
                pl.BlockSpec((1, CHUNK, LANE), lambda b, t: (b, order(t), 0))]

    c2 = lambda b, t: (0, 0)
    full = lambda a: pl.BlockSpec(a.shape, c2)
    pad = lambda p: jnp.pad(p.reshape(1, -1), ((0, 0), (0, LANE - 2 * SSD_HEADS)))
    consts = [_shift_table(), jnp.pad(conv_w, ((0, 8 - CONV_K), (0, 0))), conv_b.reshape(1, cdim),
              pad(dt_bias), pad(a_log), _expand_table(0), _expand_table(SSD_HEADS),
              jnp.repeat(d_skip, HEAD_DIM).reshape(1, SSD_WIDTH)]
    return pl.pallas_call(
        _ssd_kernel,
        grid=(bsz, nc),
        in_specs=chunk_specs(fwd) + chunk_specs(bwd) + [full(a) for a in consts],
        out_specs=[pl.BlockSpec((1, CHUNK, SSD_WIDTH), lambda b, t: (b, fwd(t), 0)),
                   pl.BlockSpec((1, CHUNK, SSD_WIDTH), lambda b, t: (b, bwd(t), 0))],
        out_shape=[jax.ShapeDtypeStruct((bsz, s, SSD_WIDTH), BF16)] * 2,
        scratch_shapes=[pltpu.VMEM((SSD_STATE, SSD_WIDTH), F32)] * 2 + [pltpu.VMEM((nc, CHUNK, cdim), BF16)],
        compiler_params=_cparams(("parallel", "arbitrary"), VMEM_LIMIT),
        name="conv_ssd_scan",
    )(xbc, xbc, xbc, dt_raw, xbc, xbc, xbc, dt_raw, *consts)


def _out_kernel(*refs, n_h):
    (fo_ref, at_ref, yf_ref, yb_ref, z_ref), refs = refs[:5], refs[5:]
    h_refs, refs = refs[:n_h], refs[n_h:]
    mod_ref, gssd_ref, gpost_ref, gpre_ref, wo_ref, wr_ref, hout_ref, u_ref, lg_ref = refs
    y = (yf_ref[0].astype(F32) + yb_ref[0].astype(F32)) * _silu(z_ref[0].astype(F32))
    sm = _rms(y, gssd_ref[...]).astype(BF16)
    m = (jnp.dot(fo_ref[0], wo_ref[0:256, :], preferred_element_type=F32)
         + jnp.dot(at_ref[0], wo_ref[256:640, :], preferred_element_type=F32)
         + jnp.dot(sm, wo_ref[640:1024, :], preferred_element_type=F32))
    mod = mod_ref[0, 0]
    hn = _stream_tile(h_refs, ROW_TILE) + mod[2:3] * _rms(m, gpost_ref[...])
    hout_ref[0] = hn
    u = _rms(hn, gpre_ref[...]) * (1.0 + mod[4:5]) + mod[3:4]
    half = D_MODEL // 2
    lo = pltpu.bitcast(u[:, :half].astype(BF16).astype(F32), jnp.uint32) >> 16
    hi = pltpu.bitcast(u[:, half:].astype(BF16).astype(F32), jnp.uint32) & jnp.uint32(0xFFFF0000)
    u_ref[0] = lo | hi
    uh, ul = _split_bf16(u, 2)
    wh, wl = wr_ref[0], wr_ref[1]
    nt = (((1,), (1,)), ((), ()))
    lg_ref[0] = (lax.dot_general(wh, uh, nt, preferred_element_type=F32)
                 + lax.dot_general(wh, ul, nt, preferred_element_type=F32)
                 + lax.dot_general(wl, uh, nt, preferred_element_type=F32))


def _out_proj(fo, at, yf, yb, z, h, mod_l, g_ssd, g_post, g_pre, w_out_p, w_router, n_rows):
    bsz = fo.shape[0]
    d = D_MODEL
    tm = ROW_TILE
    n_lat = SEQ // tm
    row = lambda b, i: (b, i, 0)
    c2 = lambda b, i: (0, 0)
    wr = jnp.stack(_split_bf16(w_router.T, 2))
    h_specs, h_args = _stream_specs(h, tm)
    return pl.pallas_call(
        functools.partial(_out_kernel, n_h=len(h_args)),
        grid=(bsz, pl.cdiv(n_rows, tm)),
        in_specs=[pl.BlockSpec((1, tm, F_WIDTH), row), pl.BlockSpec((1, tm, ATT_WIDTH), row),
                  pl.BlockSpec((1, tm, SSD_WIDTH), row), pl.BlockSpec((1, tm, SSD_WIDTH), row),
                  pl.BlockSpec((1, tm, SSD_WIDTH), row)] + h_specs + [
                  pl.BlockSpec((1, 1, 6, d), lambda b, i: (jnp.where(i < n_lat, b, bsz), 0, 0, 0)),
                  pl.BlockSpec((1, SSD_WIDTH), c2), pl.BlockSpec((1, d), c2), pl.BlockSpec((1, d), c2),
                  pl.BlockSpec((d, d), c2), pl.BlockSpec((2, N_EXPERTS, d), lambda b, i: (0, 0, 0))],
        out_specs=[pl.BlockSpec((1, tm, d), row), pl.BlockSpec((1, tm, d // 2), row),
                   pl.BlockSpec((1, N_EXPERTS, tm), lambda b, i: (b, 0, i))],
        out_shape=[jax.ShapeDtypeStruct((bsz, n_rows, d), F32),
                   jax.ShapeDtypeStruct((bsz, n_rows, d // 2), jnp.uint32),
                   jax.ShapeDtypeStruct((bsz, N_EXPERTS, n_rows), F32)],
        compiler_params=_cparams(("parallel", "arbitrary"), VMEM_LIMIT),
        name="outproj_norms_router",
    )(fo, at, yf, yb, z, *h_args, mod_l.reshape(mod_l.shape[0], 1, 6, d), g_ssd.reshape(1, SSD_WIDTH),
      g_post.reshape(1, d), g_pre.reshape(1, d), w_out_p, wr)


def _prefix_lanes(mask, upper):
    n_r, t = mask.shape
    nb = t // LANE
    if (nb * n_r) % LANE:
        carry = jnp.zeros((n_r, 1), F32)
        outs = []
        for j in range(nb):
            pre = jnp.dot(mask[:, j * LANE:(j + 1) * LANE].astype(BF16), upper, preferred_element_type=F32) + carry
            outs.append(pre)
            carry = pre[:, LANE - 1:LANE]
        return jnp.concatenate(outs, axis=1)
    stacked = jnp.concatenate([mask[:, j * LANE:(j + 1) * LANE] for j in range(nb)], axis=0).astype(BF16)
    pre = jnp.dot(stacked, upper, preferred_element_type=F32)
    ri = lax.broadcasted_iota(jnp.int32, (LANE, LANE), 0)
    last_row = (ri == LANE - 1).astype(BF16)
    tot = jnp.dot(pre.astype(BF16), last_row, preferred_element_type=F32)
    ra = lax.broadcasted_iota(jnp.int32, (nb * n_r, nb * n_r), 0)
    ca = lax.broadcasted_iota(jnp.int32, (nb * n_r, nb * n_r), 1)
    assert n_r & (n_r - 1) == 0
    earlier = (((ca ^ ra) & (n_r - 1)) == 0) & (ca < ra)
    earlier = earlier.astype(BF16)
    full = pre + jnp.dot(earlier, tot.astype(BF16), preferred_element_type=F32)
    return jnp.concatenate([full[j * n_r:(j + 1) * n_r] for j in range(nb)], axis=1)


def _slot_tokens(csum, cap):
    n_r, t = csum.shape
    nb = t // LANE
    slot = lax.broadcasted_iota(jnp.int32, (1, cap), 1).astype(F32)
    rows = []
    if nb < 8:
        csum_t = jnp.concatenate([csum, jnp.zeros((LANE - n_r, t), F32)], axis=0).T
        for x in range(n_r):
            rows.append(jnp.sum((csum_t[:, x:x + 1] <= slot).astype(F32), axis=0, keepdims=True))
        return jnp.concatenate(rows, axis=0)
    blk_id = lax.broadcasted_iota(jnp.int32, (LANE, 1), 0).astype(F32)
    never = jnp.full((LANE - nb, LANE), float(t), F32)
    for x in range(n_r):
        cx = jnp.concatenate([csum[x:x + 1, j * LANE:(j + 1) * LANE] for j in range(nb)] + [never], axis=0)
        blk = jnp.sum((cx[:, LANE - 1:LANE] <= slot).astype(F32), axis=0, keepdims=True)
        pick = (blk_id == blk).astype(BF16)
        g = sum(jnp.dot(p, pick, preferred_element_type=F32) for p in _split_bf16(cx.T, 2))
        within = jnp.sum((g <= slot).astype(F32), axis=0, keepdims=True)
        rows.append(blk * LANE + within)
    return jnp.concatenate(rows, axis=0)


def _route_kernel(lg_ref, idx_ref, aff_ref, *, cap):
    l = lg_ref[0]
    n_e, t = l.shape
    e = jnp.exp(l - jnp.max(l, axis=0, keepdims=True))
    aff = e / jnp.sum(e, axis=0, keepdims=True)
    aff_ref[0] = aff
    keys = pltpu.bitcast(aff, jnp.int32)

    def search(it, lo):
        cand = lo | (jnp.int32(1) << (30 - it))
        cnt = jnp.sum((keys >= cand).astype(F32), axis=1, keepdims=True)
        return jnp.where(cnt >= cap, cand, lo)

    thr = lax.fori_loop(0, 31, search, jnp.zeros((n_e, 1), jnp.int32))
    gt = keys > thr
    eq = keys == thr
    need = cap - jnp.sum(gt.astype(F32), axis=1, keepdims=True)
    ri = lax.broadcasted_iota(jnp.int32, (LANE, LANE), 0)
    ci = lax.broadcasted_iota(jnp.int32, (LANE, LANE), 1)
    upper = (ri <= ci).astype(BF16)
    eq_rank = _prefix_lanes(eq.astype(F32), upper)
    sel = gt | (eq & (eq_rank <= need))
    csum = _prefix_lanes(sel.astype(F32), upper)
    idx_ref[0] = _slot_tokens(csum, cap).astype(jnp.int32)


def _route(logits_t, row0, t, cap):
    bsz, n_e, _ = logits_t.shape
    return pl.pallas_call(
        functools.partial(_route_kernel, cap=cap),
        grid=(bsz,),
        in_specs=[pl.BlockSpec((1, n_e, t), lambda b: (b, 0, row0 // t))],
        out_specs=[pl.BlockSpec((1, n_e, cap), lambda b: (b, 0, 0)),
                   pl.BlockSpec((1, n_e, t), lambda b: (b, 0, 0))],
        out_shape=[jax.ShapeDtypeStruct((bsz, n_e, cap), jnp.int32),
                   jax.ShapeDtypeStruct((bsz, n_e, t), F32)],
        compiler_params=_cparams(("parallel",), VMEM_LIMIT),
        name="expert_choice_route",
    )(logits_t)


def _smem_row(n):
    return pl.BlockSpec((1, 1, 1, n), lambda b, e: (b, e, 0, 0), memory_space=pltpu.SMEM)


def _gather_kernel(idx_ref, u_ref, o_ref, *, cap):
    def body(i, carry):
        t = idx_ref[0, 0, 0, i]
        o_ref[0, pl.ds(i, 1), :] = u_ref[0, pl.ds(t, 1), :]
        return carry

    lax.fori_loop(0, cap, body, 0, unroll=8)


def _gather(u_packed, idx, row0, t, cap):
    bsz, _, half = u_packed.shape
    n_e = idx.shape[1]
    return pl.pallas_call(
        functools.partial(_gather_kernel, cap=cap),
        grid=(bsz, n_e),
        in_specs=[_smem_row(cap), pl.BlockSpec((1, t, half), lambda b, e: (b, row0 // t, 0))],
        out_specs=pl.BlockSpec((1, cap, half), lambda b, e: (e, b, 0)),
        out_shape=jax.ShapeDtypeStruct((n_e, bsz * cap, half), jnp.uint32),
        compiler_params=_cparams(("parallel", "arbitrary"), VMEM_LIMIT),
        name="expert_gather",
    )(idx[:, :, None, :], u_packed)


FF_TILE = 256


def _ffn_kernel(x_ref, wg_ref, wu_ref, wd_ref, o_ref, xs_ref):
    f = pl.program_id(2)

    @pl.when(f == 0)
    def _():
        w = x_ref[0]
        half = w.shape[1]
        xs_ref[:, :half] = pltpu.bitcast(w << 16, F32).astype(BF16)
        xs_ref[:, half:] = pltpu.bitcast(w & jnp.uint32(0xFFFF0000), F32).astype(BF16)
        o_ref[...] = jnp.zeros_like(o_ref)

    x = xs_ref[...]
    g = jnp.dot(x, wg_ref[0, 0].astype(BF16), preferred_element_type=F32)
    u = jnp.dot(x, wu_ref[0, 0].astype(BF16), preferred_element_type=F32)
    o_ref[0] += jnp.dot((_silu(g) * u).astype(BF16), wd_ref[0, 0].astype(BF16), preferred_element_type=F32)


def _expert_ffn(xe, w_gate, w_up, w_down, layer):
    n_e, m, half = xe.shape
    d = 2 * half
    ff = w_gate.shape[-1]
    tm = min(m, 2048)
    return pl.pallas_call(
        _ffn_kernel,
        grid=(n_e, m // tm, ff // FF_TILE),
        in_specs=[pl.BlockSpec((1, tm, half), lambda e, i, f: (e, i, 0)),
                  pl.BlockSpec((1, 1, d, FF_TILE), lambda e, i, f: (layer, e, 0, f)),
                  pl.BlockSpec((1, 1, d, FF_TILE), lambda e, i, f: (layer, e, 0, f)),
                  pl.BlockSpec((1, 1, FF_TILE, d), lambda e, i, f: (layer, e, f, 0))],
        out_specs=pl.BlockSpec((1, tm, d), lambda e, i, f: (e, i, 0)),
        out_shape=jax.ShapeDtypeStruct((n_e, m, d), F32),
        scratch_shapes=[pltpu.VMEM((tm, d), BF16)],
        compiler_params=_cparams(("parallel", "parallel", "arbitrary"), VMEM_LIMIT),
        name="expert_ffn",
    )(xe, w_gate, w_up, w_down)


COMB_ROWS = 8


def _combine_kernel(idx_ref, aff_ref, y_ref, *rest, cap):
    o_ref = rest[-1]

    @pl.when(pl.program_id(1) == 0)
    def _():
        o_ref[...] = jnp.zeros_like(o_ref)

    def body(c, carry):
        base = pl.multiple_of(c * COMB_ROWS, COMB_ROWS)
        toks = [idx_ref[0, 0, 0, base + r] for r in range(COMB_ROWS)]
        rows = [o_ref[0, pl.ds(t, 1), :] + aff_ref[0, 0, 0, t] * y_ref[0, pl.ds(base + r, 1), :]
                for r, t in enumerate(toks)]
        for t, row in zip(toks, rows):
            o_ref[0, pl.ds(t, 1), :] = row
        return carry

    lax.fori_loop(0, cap // COMB_ROWS, body, 0)


def _combine(y, idx, aff, row0, t, cap, s_total, out_prev=None):
    n_e, m, d = y.shape
    bsz = idx.shape[0]
    in_specs = [_smem_row(cap), _smem_row(t), pl.BlockSpec((1, cap, d), lambda b, e: (e, b, 0))]
    args = [idx[:, :, None, :], aff[:, :, None, :], y]
    aliases = {}
    if out_prev is not None:
        in_specs.append(pl.BlockSpec(memory_space=pl.ANY))
        args.append(out_prev)
        aliases = {3: 0}
    return pl.pallas_call(
        functools.partial(_combine_kernel, cap=cap),
        grid=(bsz, n_e),
        in_specs=in_specs,
        out_specs=pl.BlockSpec((1, t, d), lambda b, e: (b, row0 // t, 0)),
        out_shape=jax.ShapeDtypeStruct((bsz, s_total, d), F32),
        input_output_aliases=aliases,
        compiler_params=_cparams(("parallel", "arbitrary"), VMEM_LIMIT),
        name="expert_combine",
    )(*args)


def _moe(u_packed, logits_t, w_gate, w_up, w_down, layer, with_ctx):
    bsz, rows, _ = u_packed.shape
    out = None
    sets = [(0, SEQ)] + ([(SEQ, CTX_LEN)] if with_ctx else [])
    for row0, t in sets:
        cap = EC_CAPACITY * t // N_EXPERTS
        idx, aff = _route(logits_t, row0, t, cap)
        xe = _gather(u_packed, idx, row0, t, cap)
        y = _expert_ffn(xe, w_gate, w_up, w_down, layer)
        out = _combine(y, idx, aff, row0, t, cap, rows, out)
    return out


def _final_kernel(h_ref, f_ref, mod_ref, g_ref, o_ref):
    o_ref[0] = h_ref[0] + mod_ref[0, 0][5:6] * _rms(f_ref[0], g_ref[...])


def _final_residual(h, f, mod_l, g_post):
    bsz, n, d = h.shape
    tm = ROW_TILE
    row = lambda b, i: (b, i, 0)
    return pl.pallas_call(
        _final_kernel,
        grid=(bsz, n // tm),
        in_specs=[pl.BlockSpec((1, tm, d), row), pl.BlockSpec((1, tm, d), row),
                  pl.BlockSpec((1, 1, 6, d), lambda b, i: (b, 0, 0, 0)), pl.BlockSpec((1, d), lambda b, i: (0, 0))],
        out_specs=pl.BlockSpec((1, tm, d), row),
        out_shape=jax.ShapeDtypeStruct((bsz, n, d), F32),
        compiler_params=_cparams(("parallel", "arbitrary")),
        name="ffn_residual",
    )(h, f, mod_l.reshape(mod_l.shape[0], 1, 6, d), g_post.reshape(1, d))


def _rope_tables():
    n = jnp.arange(SEQ, dtype=jnp.int32)
    row = (n // GRID_W).astype(F32)
    col = (n % GRID_W).astype(F32)
    quarter = HEAD_DIM // 4
    inv = ROPE_THETA ** (-jnp.arange(quarter, dtype=F32) / quarter)
    lane = jnp.arange(LANE)
    pos = jnp.where((lane % HEAD_DIM < HEAD_DIM // 2)[None, :], row[:, None], col[:, None])
    ang = pos * inv[lane % quarter][None, :]
    first = (lane % (2 * quarter) < quarter)[None, :]
    cos = jnp.cos(ang)
    sin = jnp.sin(ang)
    sa = jnp.where(first, -sin, 0.0)
    sb = jnp.where(first, 0.0, sin)
    padc = lambda t, v: jnp.concatenate([t, jnp.full((CTX_LEN, LANE), v, F32)], axis=0)
    return padc(cos, 1.0), padc(sa, 0.0), padc(sb, 0.0)


def kernel(x, c, ctx, c_ctx, w_ada, b_ada, g_mix_pre, g_mix_post, g_ffn_pre, g_ffn_post, w_in, conv_w, conv_b,
           dt_bias, a_log, d_skip, g_ssd, attn_sink, w_out, w_router, w_gate, w_up, w_down):
    depth = w_ada.shape[0]
    bsz = x.shape[0]
    d = D_MODEL
    cc = jnp.concatenate([c, c_ctx[None, :], jnp.zeros((7, d), F32)], axis=0)
    mod = _modulation(cc, w_ada, b_ada).reshape(depth, bsz + 8, 6, d)
    rope_tabs = _rope_tables()
    chan_tab = _channel_tables()
    cn_lat, sn_lat = _dft_tables(SEQ)
    cn_ctx, sn_ctx = _dft_tables(CTX_LEN)
    q0 = F_WIDTH

    h = (x, ctx)
    f_prev = None
    for l in range(depth):
        last = l == depth - 1
        w_q = _to_slab_order(w_in[l][:, q0:q0 + ATT_WIDTH], 1) * HEAD_DIM ** -0.5
        w_in_p = jnp.concatenate([w_in[l][:, :q0], w_q, w_in[l][:, q0 + ATT_WIDTH:],
                                  jnp.zeros((d, IN_PAD - IN_WIDTH), F32)], axis=1).astype(BF16)
        w_out_p = jnp.concatenate([w_out[l][:q0], _to_slab_order(w_out[l][q0:q0 + ATT_WIDTH], 0),
                                   w_out[l][q0 + ATT_WIDTH:]], axis=0).astype(BF16)
        outs = _in_proj(h, mod[l], mod[l - 1] if l else None, g_mix_pre[l], w_in_p, rope_tabs,
                        f_prev, g_ffn_post[l - 1] if l else None)
        if f_prev is not None:
            h, outs = outs[0], outs[1:]
        fo, q, k, v, z, xbc, dt_raw = outs
        fmix = _fourier(fo, chan_tab, cn_lat, sn_lat, 0, SEQ, out_rows=SEQ if last else S_ALL)
        if not last:
            fmix = _fourier(fo, chan_tab, cn_ctx, sn_ctx, SEQ, CTX_LEN, fmix)
        att = _attention(q, k, v, attn_sink[l], not last)
        y_f, y_b = _ssd_mixer(xbc, dt_raw, conv_w[l], conv_b[l], dt_bias[l], a_log[l], d_skip[l])
        n_rows = SEQ if last else S_ALL
        h, u_packed, logits_t = _out_proj(fmix, att, y_f, y_b, z, h, mod[l], g_ssd[l],
                                          g_mix_post[l], g_ffn_pre[l], w_out_p, w_router[l], n_rows)
        f_prev = _moe(u_packed, logits_t, w_gate, w_up, w_down, l, not last)
    return _final_residual(h, f_prev, mod[depth - 1], g_ffn_post[depth - 1])
```

```python
import functools
import math

import jax
import jax.numpy as jnp
from jax import lax
from jax.experimental import pallas as pl
from jax.experimental.pallas import tpu as pltpu

F32 = jnp.float32
BF16 = jnp.bfloat16

D_MODEL = 1024
SEQ = 4096
CTX_LEN = 256
S_ALL = SEQ + CTX_LEN
GRID_W = 64
HEAD_DIM = 64
EPS = 1e-6
F_WIDTH = 256
ATT_HEADS = 6
ATT_KV_HEADS = 2
ATT_WIDTH = 384
KV_WIDTH = 128
BLOCK = 128
ROPE_THETA = 10000.0
SSD_HEADS = 6
SSD_WIDTH = 384
SSD_GROUPS = 2
SSD_STATE = 128
CHUNK = 128
CONV_K = 5
CONV_DIM = 896
IN_WIDTH = 2188
IN_PAD = 2304
N_EXPERTS = 16
EXPERT_FF = 2816
EC_CAPACITY = 2

LANE = 128
ROW_TILE = 512
VMEM_LIMIT = 56 * 1024 * 1024


def _cparams(sem, vmem=None):
    return pltpu.CompilerParams(dimension_semantics=sem, vmem_limit_bytes=vmem)


def _split_bf16(x, n):
    parts, r = [], x
    for _ in range(n):
        p = r.astype(BF16)
        parts.append(p)
        r = r - p.astype(F32)
    return parts


def _rms(x, g):
    return x * lax.rsqrt(jnp.mean(x * x, axis=-1, keepdims=True) + EPS) * g


def _silu(x):
    return x * jax.nn.sigmoid(x)


def _mod_kernel(cc_ref, w_ref, b_ref, o_ref):
    s = _silu(cc_ref[...])
    o_ref[0] = jnp.dot(s.astype(BF16), w_ref[0].astype(BF16), preferred_element_type=F32) + b_ref[0]


def _modulation(cc, w_ada, b_ada):
    n_l, d, d6 = w_ada.shape
    r = cc.shape[0]
    tn = 1024
    return pl.pallas_call(
        _mod_kernel,
        grid=(n_l, d6 // tn),
        in_specs=[pl.BlockSpec((r, d), lambda l, j: (0, 0)),
                  pl.BlockSpec((1, d, tn), lambda l, j: (l, 0, j)),
                  pl.BlockSpec((1, 1, tn), lambda l, j: (l, 0, j))],
        out_specs=pl.BlockSpec((1, r, tn), lambda l, j: (l, 0, j)),
        out_shape=jax.ShapeDtypeStruct((n_l, r, d6), F32),
        compiler_params=_cparams(("arbitrary", "arbitrary")),
        name="adaln_mod",
    )(cc, w_ada, b_ada.reshape(n_l, 1, d6))


def _stream_specs(h, tm):
    n_lat = SEQ // tm
    if isinstance(h, tuple):
        lat, ctx = h
        d = lat.shape[-1]
        return ([pl.BlockSpec((1, tm, d), lambda b, i: (b, jnp.minimum(i, n_lat - 1), 0)),
                 pl.BlockSpec((1, CTX_LEN, d), lambda b, i: (b, 0, 0))], [lat, ctx])
    return [pl.BlockSpec((1, tm, h.shape[-1]), lambda b, i: (b, i, 0))], [h]


def _stream_tile(refs, tm):
    if len(refs) == 2:
        ctx = jnp.concatenate([refs[1][0]] * (tm // CTX_LEN), axis=0)
        return jnp.where(pl.program_id(1) < SEQ // tm, refs[0][0], ctx)
    return refs[0][0]


def _in_kernel(*refs, residual, n_h):
    h_refs, refs = refs[:n_h], refs[n_h:]
    if residual:
        (f_ref, gpost_ref, mod_ref, g_ref, w_ref, cos_ref, sa_ref, sb_ref,
         hout_ref, fo_ref, q_ref, k_ref, v_ref, z_ref, xbc_ref, dt_ref) = refs
    else:
        (mod_ref, g_ref, w_ref, cos_ref, sa_ref, sb_ref,
         fo_ref, q_ref, k_ref, v_ref, z_ref, xbc_ref, dt_ref) = refs
    x = _stream_tile(h_refs, ROW_TILE)
    mod = mod_ref[0, 0]
    if residual:
        x = x + mod[5:6] * _rms(f_ref[0], gpost_ref[...])
        hout_ref[0] = x
    u = _rms(x, g_ref[...]) * (1.0 + mod[1:2]) + mod[0:1]
    p = jnp.dot(u.astype(BF16), w_ref[...], preferred_element_type=F32)
    cos, sa, sb = cos_ref[...], sa_ref[...], sb_ref[...]

    def rope(t):
        return t * cos + pltpu.roll(t, LANE - 16, 1) * sa + pltpu.roll(t, 16, 1) * sb

    fo_ref[0] = p[:, 0:256].astype(BF16)
    q_ref[0] = jnp.concatenate([rope(p[:, 256 + LANE * s:256 + LANE * (s + 1)]) for s in range(3)],
                               axis=1).astype(BF16)
    k_ref[0] = rope(p[:, 640:768]).astype(BF16)
    v_ref[0] = p[:, 768:896].astype(BF16)
    z_ref[0] = p[:, 896:1280].astype(BF16)
    xbc_ref[0] = p[:, 1280:2176].astype(BF16)
    dt_ref[0] = p[:, 2176:2304]


def _in_proj(h, mod_l, mod_prev, g_pre, w_in_p, rope_tabs, f_prev=None, g_post_prev=None):
    bsz = mod_l.shape[0] - 8
    s, d = S_ALL, D_MODEL
    tm = ROW_TILE
    n_lat = SEQ // tm
    residual = f_prev is not None
    row = lambda b, i: (b, i, 0)
    mod_map = lambda b, i: (jnp.where(i < n_lat, b, bsz), 0, 0, 0)
    const2 = lambda b, i: (0, 0)
    tab = pl.BlockSpec((tm, LANE), lambda b, i: (i, 0))
    in_specs, args = _stream_specs(h, tm)
    n_h = len(args)
    if residual:
        in_specs += [pl.BlockSpec((1, tm, d), row), pl.BlockSpec((1, d), const2)]
        args += [f_prev, g_post_prev.reshape(1, d)]
    mod_used = mod_l if not residual else jnp.concatenate([mod_l[:, :5], mod_prev[:, 5:6]], axis=1)
    in_specs += [pl.BlockSpec((1, 1, 6, d), mod_map), pl.BlockSpec((1, d), const2),
                 pl.BlockSpec((d, IN_PAD), const2), tab, tab, tab]
    args += [mod_used.reshape(mod_used.shape[0], 1, 6, d), g_pre.reshape(1, d), w_in_p, *rope_tabs]
    widths = [(256, BF16), (384, BF16), (128, BF16), (128, BF16), (384, BF16), (896, BF16), (128, F32)]
    out_specs = [pl.BlockSpec((1, tm, w), row) for w, _ in widths]
    out_shape = [jax.ShapeDtypeStruct((bsz, s, w), dt) for w, dt in widths]
    if residual:
        out_specs = [pl.BlockSpec((1, tm, d), row)] + out_specs
        out_shape = [jax.ShapeDtypeStruct((bsz, s, d), F32)] + out_shape
    return pl.pallas_call(
        functools.partial(_in_kernel, residual=residual, n_h=n_h),
        grid=(bsz, pl.cdiv(s, tm)),
        in_specs=in_specs, out_specs=out_specs, out_shape=out_shape,
        compiler_params=_cparams(("parallel", "arbitrary"), VMEM_LIMIT),
        name="prenorm_inproj",
    )(*args)


def _f1_kernel(u_ref, w_ref, a_ref, b_ref):
    r = jnp.dot(u_ref[0], w_ref[...], preferred_element_type=F32)
    a_ref[...] = r[:, :F_WIDTH].astype(BF16)
    b_ref[...] = r[:, F_WIDTH:].astype(BF16)


def _f2_kernel(c_ref, s_ref, a_ref, b_ref, *rest):
    o_ref = rest[-1]
    y = (jnp.dot(c_ref[...], a_ref[...], preferred_element_type=F32)
         - jnp.dot(s_ref[...], b_ref[...], preferred_element_type=F32))
    o_ref[0] = y.astype(BF16)


def _dft_tables(n):
    sc = n ** -0.5

    def cs(rows, cols, period):
        m = (rows[:, None] * cols[None, :]) % period
        ang = m.astype(F32) * (2.0 * math.pi / period)
        return jnp.cos(ang), jnp.sin(ang)

    k = jnp.arange(n, dtype=jnp.int32)
    r = 64
    if n <= r * r // 16:
        c, s = cs(k, k, n)
        return (c * sc).astype(BF16), (s * sc).astype(BF16)
    ca, sa = cs(k, jnp.arange(n // r, dtype=jnp.int32), n // r)
    cb, sb = cs(k, jnp.arange(r, dtype=jnp.int32), n)
    c = ca[:, :, None] * cb[:, None, :] - sa[:, :, None] * sb[:, None, :]
    s = sa[:, :, None] * cb[:, None, :] + ca[:, :, None] * sb[:, None, :]
    return (c * sc).reshape(n, n).astype(BF16), (s * sc).reshape(n, n).astype(BF16)


def _channel_tables():
    c64, s64 = _dft_tables(HEAD_DIM)
    eye = jnp.eye(F_WIDTH // HEAD_DIM, dtype=BF16)
    return jnp.concatenate([jnp.kron(eye, c64), jnp.kron(eye, s64)], axis=1)


F2_EXT = 16


def _f2_sym_kernel(cm_ref, ce_ref, sm_ref, se_ref, a_ref, b_ref, jr_ref, o_ref, p_ref, q_ref, *, tm, nt):
    i = pl.program_id(1)
    half = nt // 2

    @pl.when(i == 0)
    def _():
        tail = half * tm + F2_EXT
        p_ref[tail:, :] = jnp.zeros((LANE - F2_EXT, F_WIDTH), F32)
        q_ref[tail:, :] = jnp.zeros((LANE - F2_EXT, F_WIDTH), F32)

    @pl.when(i < half)
    def _():
        a, b = a_ref[...], b_ref[...]
        row = pl.multiple_of(i * tm, tm)
        p = jnp.dot(cm_ref[...], a, preferred_element_type=F32)
        q = jnp.dot(sm_ref[...], b, preferred_element_type=F32)
        p_ref[pl.ds(row, tm), :] = p
        q_ref[pl.ds(row, tm), :] = q
        p_ref[pl.ds(row + tm, F2_EXT), :] = jnp.dot(ce_ref[...], a, preferred_element_type=F32)
        q_ref[pl.ds(row + tm, F2_EXT), :] = jnp.dot(se_ref[...], b, preferred_element_type=F32)
        o_ref[0] = (p - q).astype(BF16)

    @pl.when(i >= half)
    def _():
        row = pl.multiple_of((nt - 1 - i) * tm, tm)
        s = (p_ref[pl.ds(row, tm + LANE), :] + q_ref[pl.ds(row, tm + LANE), :]).astype(BF16)
        o_ref[0] = jnp.dot(jr_ref[...], s, preferred_element_type=F32).astype(BF16)


def _fourier_positions_sym(cn, sn, a_all, b_all, bsz, n, tm, blk0, out_rows):
    nt = n // tm
    half = nt // 2
    per = tm // F2_EXT
    main = lambda b, i: (jnp.minimum(i, half - 1), 0)
    ext = lambda b, i: ((jnp.minimum(i, half - 1) + 1) * per, 0)
    r = jnp.arange(tm)[:, None]
    flip = (jnp.arange(tm + LANE)[None, :] == tm - r).astype(BF16)
    return pl.pallas_call(
        functools.partial(_f2_sym_kernel, tm=tm, nt=nt),
        grid=(bsz, nt),
        in_specs=[pl.BlockSpec((tm, n), main), pl.BlockSpec((F2_EXT, n), ext),
                  pl.BlockSpec((tm, n), main), pl.BlockSpec((F2_EXT, n), ext),
                  pl.BlockSpec((n, F_WIDTH), lambda b, i: (0, b)),
                  pl.BlockSpec((n, F_WIDTH), lambda b, i: (0, b)),
                  pl.BlockSpec((tm, tm + LANE), lambda b, i: (0, 0))],
        out_specs=pl.BlockSpec((1, tm, F_WIDTH), lambda b, i: (b, blk0 + i, 0)),
        out_shape=jax.ShapeDtypeStruct((bsz, out_rows, F_WIDTH), BF16),
        scratch_shapes=[pltpu.VMEM((n // 2 + LANE, F_WIDTH), F32)] * 2,
        compiler_params=_cparams(("parallel", "arbitrary"), VMEM_LIMIT),
        name="fourier_positions",
    )(cn, cn, sn, sn, a_all, b_all, flip)


def _fourier(fo, chan_tab, cn, sn, row0, n, out_prev=None, out_rows=S_ALL):
    bsz, s, _ = fo.shape
    tm = min(n, 512)
    nt = n // tm
    blk0 = row0 // tm
    a_all, b_all = pl.pallas_call(
        _f1_kernel,
        grid=(bsz, nt),
        in_specs=[pl.BlockSpec((1, tm, F_WIDTH), lambda b, i: (b, blk0 + i, 0)),
                  pl.BlockSpec((F_WIDTH, 2 * F_WIDTH), lambda b, i: (0, 0))],
        out_specs=[pl.BlockSpec((tm, F_WIDTH), lambda b, i: (i, b))] * 2,
        out_shape=[jax.ShapeDtypeStruct((n, bsz * F_WIDTH), BF16)] * 2,
        compiler_params=_cparams(("parallel", "arbitrary")),
        name="fourier_channels",
    )(fo, chan_tab)
    if nt >= 4 and out_prev is None:
        return _fourier_positions_sym(cn, sn, a_all, b_all, bsz, n, tm, blk0, out_rows)
    in_specs = [pl.BlockSpec((tm, n), lambda i, b: (i, 0)),
                pl.BlockSpec((tm, n), lambda i, b: (i, 0)),
                pl.BlockSpec((n, F_WIDTH), lambda i, b: (0, b)),
                pl.BlockSpec((n, F_WIDTH), lambda i, b: (0, b))]
    args = [cn, sn, a_all, b_all]
    aliases = {}
    if out_prev is not None:
        in_specs.append(pl.BlockSpec(memory_space=pl.ANY))
        args.append(out_prev)
        aliases = {4: 0}
    return pl.pallas_call(
        _f2_kernel,
        grid=(nt, bsz),
        in_specs=in_specs,
        out_specs=pl.BlockSpec((1, tm, F_WIDTH), lambda i, b: (b, blk0 + i, 0)),
        out_shape=jax.ShapeDtypeStruct((bsz, out_rows, F_WIDTH), BF16),
        input_output_aliases=aliases,
        compiler_params=_cparams(("parallel", "arbitrary"), VMEM_LIMIT),
        name="fourier_positions",
    )(*args)


def _to_slab_order(w, axis):
    r = ATT_HEADS // ATT_KV_HEADS
    shape = w.shape
    w = w.reshape(shape[:axis] + (ATT_KV_HEADS, r, HEAD_DIM) + shape[axis + 1:])
    return jnp.swapaxes(w, axis, axis + 1).reshape(shape)


def _attend(q, k, v, masks, sink_ref):
    n = q.shape[0]
    r = ATT_HEADS // ATT_KV_HEADS
    lo = lax.broadcasted_iota(jnp.int32, (1, LANE), 1) < HEAD_DIM
    rid = lax.broadcasted_iota(jnp.int32, (r * n, 1), 0)
    outs = []
    for g in range(ATT_KV_HEADS):
        keep = lo if g == 0 else jnp.logical_not(lo)
        qg = jnp.concatenate([jnp.where(keep, q[:, LANE * s:LANE * (s + 1)], jnp.zeros((), BF16))
                              for s in range(r)], axis=0)
        snk = jnp.full((r * n, 1), sink_ref[r * g], F32)
        for j in range(1, r):
            snk = jnp.where(rid >= j * n, sink_ref[r * g + j], snk)
        sc = lax.dot_general(qg, k, (((1,), (1,)), ((), ())), preferred_element_type=F32)
        if masks:
            nblk = sc.shape[1] // LANE
            sc = jnp.concatenate(
                [jnp.where(masks[b], sc[:, b * LANE:(b + 1) * LANE], -1e30) if b in masks
                 else sc[:, b * LANE:(b + 1) * LANE] for b in range(nblk)], axis=1)
        mx = jnp.maximum(jnp.max(sc, axis=-1, keepdims=True), snk)
        p = jnp.exp(sc - mx)
        den = jnp.sum(p, axis=-1, keepdims=True) + jnp.exp(snk - mx)
        outs.append(jnp.dot(p.astype(BF16), v, preferred_element_type=F32) / den)
    return jnp.concatenate([jnp.where(lo, outs[0][s * n:(s + 1) * n], outs[1][s * n:(s + 1) * n])
                            for s in range(r)], axis=1)


def _attn_lat_kernel(sink_ref, q_ref, kp_ref, kc_ref, kn_ref, kx_ref, vp_ref, vc_ref, vn_ref, vx_ref, o_ref):
    nb = pl.num_programs(1)
    n = pl.program_id(1)
    rows = (ATT_HEADS // ATT_KV_HEADS) * BLOCK
    i = lax.broadcasted_iota(jnp.int32, (rows, BLOCK), 0) & (BLOCK - 1)
    j = lax.broadcasted_iota(jnp.int32, (rows, BLOCK), 1)
    m_prev = (j >= i) & (n > 0)
    m_next = (j <= i) & (n < nb - 1)
    k = jnp.concatenate([kp_ref[0], kc_ref[0], kn_ref[0], kx_ref[0]], axis=0)
    v = jnp.concatenate([vp_ref[0], vc_ref[0], vn_ref[0], vx_ref[0]], axis=0)
    o_ref[0] = _attend(q_ref[0], k, v, {0: m_prev, 2: m_next}, sink_ref).astype(BF16)


def _attn_ctx_kernel(sink_ref, q_ref, kx_ref, vx_ref, prev_ref, o_ref):
    o_ref[0] = _attend(q_ref[0], kx_ref[0], vx_ref[0], {}, sink_ref).astype(BF16)


def _attention(q, k, v, sink, with_ctx):
    bsz = q.shape[0]
    s = S_ALL if with_ctx else SEQ
    nb = SEQ // BLOCK
    ctx_blk = SEQ // CTX_LEN
    smem = pl.BlockSpec(memory_space=pltpu.SMEM)
    kv = lambda f: pl.BlockSpec((1, BLOCK, KV_WIDTH), f)
    prev = lambda b, n: (b, jnp.maximum(n - 1, 0), 0)
    cur = lambda b, n: (b, n, 0)
    nxt = lambda b, n: (b, jnp.minimum(n + 1, nb - 1), 0)
    ctx = pl.BlockSpec((1, CTX_LEN, KV_WIDTH), lambda b, n: (b, ctx_blk, 0))
    sink8 = jnp.pad(sink, (0, 8 - ATT_HEADS))
    out = pl.pallas_call(
        _attn_lat_kernel,
        grid=(bsz, nb),
        in_specs=[smem, pl.BlockSpec((1, BLOCK, ATT_WIDTH), cur),
                  kv(prev), kv(cur), kv(nxt), ctx, kv(prev), kv(cur), kv(nxt), ctx],
        out_specs=pl.BlockSpec((1, BLOCK, ATT_WIDTH), cur),
        out_shape=jax.ShapeDtypeStruct((bsz, s, ATT_WIDTH), BF16),
        compiler_params=_cparams(("parallel", "arbitrary")),
        name="window_attention",
    )(sink8, q, k, k, k, k, v, v, v, v)
    if not with_ctx:
        return out
    cq = lambda b: (b, ctx_blk, 0)
    return pl.pallas_call(
        _attn_ctx_kernel,
        grid=(bsz,),
        in_specs=[smem, pl.BlockSpec((1, CTX_LEN, ATT_WIDTH), cq),
                  pl.BlockSpec((1, CTX_LEN, KV_WIDTH), cq), pl.BlockSpec((1, CTX_LEN, KV_WIDTH), cq),
                  pl.BlockSpec(memory_space=pl.ANY)],
        out_specs=pl.BlockSpec((1, CTX_LEN, ATT_WIDTH), cq),
        out_shape=jax.ShapeDtypeStruct((bsz, s, ATT_WIDTH), BF16),
        input_output_aliases={4: 0},
        compiler_params=_cparams(("parallel",)),
        name="context_attention",
    )(sink8, q, k, v, out)


HALO = 16
N_SHIFT = CONV_K - 1


def _shift_table():
    i = jnp.arange(CHUNK)[:, None]
    j = jnp.arange(CHUNK + 2 * HALO)[None, :]
    taps = [k for k in range(CONV_K) if k != CONV_K // 2]
    return jnp.concatenate([(j == i + k - CONV_K // 2 + HALO) for k in taps], axis=0).astype(BF16)


def _expand_table(lane0):
    src = jnp.arange(LANE)[:, None] - lane0
    return (src == (jnp.arange(SSD_WIDTH)[None, :] // HEAD_DIM)).astype(BF16)


def _conv_chunk(c, xp_ref, xc_ref, xn_ref, sh_ref, w_ref, b_ref):
    n_lat = SEQ // CHUNK
    last = S_ALL // CHUNK - 1
    has_prev = (c != 0) & (c != n_lat)
    has_next = (c != n_lat - 1) & (c != last)
    zero = jnp.zeros((), BF16)
    cur = xc_ref[0]
    xx = jnp.concatenate([jnp.where(has_prev, xp_ref[0], zero), cur, jnp.where(has_next, xn_ref[0], zero)], axis=0)
    sh = jnp.dot(sh_ref[...], xx, preferred_element_type=F32)
    mid = CONV_K // 2
    acc = b_ref[...] + w_ref[mid:mid + 1, :] * cur.astype(F32)
    for t in range(N_SHIFT):
        k = t if t < mid else t + 1
        acc = acc + w_ref[k:k + 1, :] * sh[t * CHUNK:(t + 1) * CHUNK]
    return _silu(acc)


def _ssd_chunk(xc, dt_raw, dtb, alog, ex_ref, dsk, state_ref, reverse):
    xs = xc[:, :SSD_WIDTH].astype(F32)
    gs = SSD_GROUPS * SSD_STATE
    bm = xc[:, SSD_WIDTH:SSD_WIDTH + gs]
    cm = xc[:, SSD_WIDTH + gs:SSD_WIDTH + 2 * gs]
    raw = dt_raw + dtb
    dt = jnp.maximum(raw, 0.0) + jnp.log(1.0 + jnp.exp(-jnp.abs(raw)))
    dta = dt * (-jnp.exp(alog))
    ri = lax.broadcasted_iota(jnp.int32, (CHUNK, CHUNK), 0)
    ci = lax.broadcasted_iota(jnp.int32, (CHUNK, CHUNK), 1)
    keep = (ri <= ci) if reverse else (ri >= ci)
    tri = keep.astype(BF16)
    acs = sum(jnp.dot(tri, p, preferred_element_type=F32) for p in _split_bf16(dta, 3))
    acs_t = acs.T
    parts = jnp.concatenate(_split_bf16(dt, 2) + _split_bf16(acs, 3), axis=0)
    wide = jnp.dot(parts, ex_ref[...], preferred_element_type=F32)
    dt_x = wide[:CHUNK] + wide[CHUNK:2 * CHUNK]
    acs_x = wide[2 * CHUNK:3 * CHUNK] + wide[3 * CHUNK:4 * CHUNK] + wide[4 * CHUNK:]
    edge = 0 if reverse else CHUNK - 1
    tot_x = acs_x[edge:edge + 1, :]
    rh = SSD_HEADS // SSD_GROUPS
    gw = rh * HEAD_DIM
    lane0 = SSD_HEADS if reverse else 0
    xd = xs * dt_x
    xdw = (xd * jnp.exp(tot_x - acs_x)).astype(BF16)
    xdb = xd.astype(BF16)
    eacs = jnp.exp(acs_x)
    state = state_ref[...]
    ys = []
    new_state = []
    for g in range(SSD_GROUPS):
        bg = bm[:, g * SSD_STATE:(g + 1) * SSD_STATE]
        cg = cm[:, g * SSD_STATE:(g + 1) * SSD_STATE]
        cb = lax.dot_general(cg, bg, (((1,), (1,)), ((), ())), preferred_element_type=F32)
        sg = state[:, g * gw:(g + 1) * gw]
        y_off = jnp.dot(cg, sg.astype(BF16), preferred_element_type=F32) * eacs[:, g * gw:(g + 1) * gw]
        for r in range(rh):
            h = g * rh + r
            ln = lane0 + h
            dec = jnp.where(keep, jnp.exp(acs[:, ln:ln + 1] - acs_t[ln:ln + 1, :]), 0.0)
            yd = jnp.dot((cb * dec).astype(BF16), xdb[:, h * HEAD_DIM:(h + 1) * HEAD_DIM],
                         preferred_element_type=F32)
            ys.append(yd + y_off[:, r * HEAD_DIM:(r + 1) * HEAD_DIM])
        bt = bg.astype(F32).T.astype(BF16)
        new_state.append(jnp.dot(bt, xdw[:, g * gw:(g + 1) * gw], preferred_element_type=F32))
    state_ref[...] = state * jnp.exp(tot_x) + jnp.concatenate(new_state, axis=1)
    y = jnp.concatenate(ys, axis=1)
    return y if dsk is None else y + dsk * xs


def _ssd_kernel(fp_ref, fc_ref, fn_ref, fdt_ref, bp_ref, bc_ref, bn_ref, bdt_ref,
                sh_ref, w_ref, b_ref, dtb_ref, alog_ref, exf_ref, exb_ref, dsk_ref,
                yf_ref, yb_ref, sf_ref, sb_ref, xcache_ref):
    step = pl.program_id(1)
    nc = pl.num_programs(1)
    n_lat = SEQ // CHUNK

    @pl.when(step == 0)
    def _():
        sf_ref[...] = jnp.zeros_like(sf_ref)
        sb_ref[...] = jnp.zeros_like(sb_ref)

    cf = (step + n_lat) % nc
    cbk = nc - 1 - step

    @pl.when((step != 1) & (step < n_lat // 2 + 2))
    def _():
        xcache_ref[cf] = _conv_chunk(cf, fp_ref, fc_ref, fn_ref, sh_ref, w_ref, b_ref).astype(BF16)
        xcache_ref[cbk] = _conv_chunk(cbk, bp_ref, bc_ref, bn_ref, sh_ref, w_ref, b_ref).astype(BF16)

    dtb, alog = dtb_ref[...], alog_ref[...]
    yf_ref[0] = _ssd_chunk(xcache_ref[cf], fdt_ref[0], dtb, alog, exf_ref, dsk_ref[...], sf_ref,
                           False).astype(BF16)
    yb_ref[0] = _ssd_chunk(xcache_ref[cbk], bdt_ref[0], dtb, alog, exb_ref, None, sb_ref, True).astype(BF16)


def _ssd_mixer(xbc, dt_raw, conv_w, conv_b, dt_bias, a_log, d_skip):
    bsz, s, cdim = xbc.shape
    nc = s // CHUNK
    n_lat = SEQ // CHUNK
    per = CHUNK // HALO
    nh = s // HALO
    assert nc == n_lat + 2 and n_lat % 2 == 0
    fwd = lambda t: (t + n_lat) % nc
    bwd = lambda t: nc - 1 - t

    def chunk_specs(order):
        return [pl.BlockSpec((1, HALO, cdim), lambda b, t: (b, jnp.maximum(order(t) * per - 1, 0), 0)),
                pl.BlockSpec((1, CHUNK, cdim), lambda b, t: (b, order(t), 0)),
                pl.BlockSpec((1, HALO, cdim), lambda b, t: (b, jnp.minimum(order(t) * per + per, nh - 1), 0)),
                pl.BlockSpec((1, CHUNK, LANE), lambda b, t: (b, order(t), 0))]

    c2 = lambda b, t: (0, 0)
    full = lambda a: pl.BlockSpec(a.shape, c2)
    pad = lambda p: jnp.pad(p.reshape(1, -1), ((0, 0), (0, LANE - 2 * SSD_HEADS)))
    consts = [_shift_table(), jnp.pad(conv_w, ((0, 8 - CONV_K), (0, 0))), conv_b.reshape(1, cdim),
              pad(dt_bias), pad(a_log), _expand_table(0), _expand_table(SSD_HEADS),
              jnp.repeat(d_skip, HEAD_DIM).reshape(1, SSD_WIDTH)]
    return pl.pallas_call(
        _ssd_kernel,
        grid=(bsz, nc),
        in_specs=chunk_specs(fwd) + chunk_specs(bwd) + [full(a) for a in consts],
        out_specs=[pl.BlockSpec((1, CHUNK, SSD_WIDTH), lambda b, t: (b, fwd(t), 0)),
                   pl.BlockSpec((1, CHUNK, SSD_WIDTH), lambda b, t: (b, bwd(t), 0))],
        out_shape=[jax.ShapeDtypeStruct((bsz, s, SSD_WIDTH), BF16)] * 2,
        scratch_shapes=[pltpu.VMEM((SSD_STATE, SSD_WIDTH), F32)] * 2 + [pltpu.VMEM((nc, CHUNK, cdim), BF16)],
        compiler_params=_cparams(("parallel", "arbitrary"), VMEM_LIMIT),
        name="conv_ssd_scan",
    )(xbc, xbc, xbc, dt_raw, xbc, xbc, xbc, dt_raw, *consts)


def _out_kernel(*refs, n_h):
    (fo_ref, at_ref, yf_ref, yb_ref, z_ref), refs = refs[:5], refs[5:]
    h_refs, refs = refs[:n_h], refs[n_h:]
    mod_ref, gssd_ref, gpost_ref, gpre_ref, wo_ref, wr_ref, hout_ref, u_ref, lg_ref = refs
    y = (yf_ref[0].astype(F32) + yb_ref[0].astype(F32)) * _silu(z_ref[0].astype(F32))
    sm = _rms(y, gssd_ref[...]).astype(BF16)
    m = (jnp.dot(fo_ref[0], wo_ref[0:256, :], preferred_element_type=F32)
         + jnp.dot(at_ref[0], wo_ref[256:640, :], preferred_element_type=F32)
         + jnp.dot(sm, wo_ref[640:1024, :], preferred_element_type=F32))
    mod = mod_ref[0, 0]
    hn = _stream_tile(h_refs, ROW_TILE) + mod[2:3] * _rms(m, gpost_ref[...])
    hout_ref[0] = hn
    u = _rms(hn, gpre_ref[...]) * (1.0 + mod[4:5]) + mod[3:4]
    half = D_MODEL // 2
    lo = pltpu.bitcast(u[:, :half].astype(BF16).astype(F32), jnp.uint32) >> 16
    hi = pltpu.bitcast(u[:, half:].astype(BF16).astype(F32), jnp.uint32) & jnp.uint32(0xFFFF0000)
    u_ref[0] = lo | hi
    uh, ul = _split_bf16(u, 2)
    wh, wl = wr_ref[0], wr_ref[1]
    nt = (((1,), (1,)), ((), ()))
    lg_ref[0] = (lax.dot_general(wh, uh, nt, preferred_element_type=F32)
                 + lax.dot_general(wh, ul, nt, preferred_element_type=F32)
                 + lax.dot_general(wl, uh, nt, preferred_element_type=F32))


def _out_proj(fo, at, yf, yb, z, h, mod_l, g_ssd, g_post, g_pre, w_out_p, w_router, n_rows):
    bsz = fo.shape[0]
    d = D_MODEL
    tm = ROW_TILE
    n_lat = SEQ // tm
    row = lambda b, i: (b, i, 0)
    c2 = lambda b, i: (0, 0)
    wr = jnp.stack(_split_bf16(w_router.T, 2))
    h_specs, h_args = _stream_specs(h, tm)
    return pl.pallas_call(
        functools.partial(_out_kernel, n_h=len(h_args)),
        grid=(bsz, pl.cdiv(n_rows, tm)),
        in_specs=[pl.BlockSpec((1, tm, F_WIDTH), row), pl.BlockSpec((1, tm, ATT_WIDTH), row),
                  pl.BlockSpec((1, tm, SSD_WIDTH), row), pl.BlockSpec((1, tm, SSD_WIDTH), row),
                  pl.BlockSpec((1, tm, SSD_WIDTH), row)] + h_specs + [
                  pl.BlockSpec((1, 1, 6, d), lambda b, i: (jnp.where(i < n_lat, b, bsz), 0, 0, 0)),
                  pl.BlockSpec((1, SSD_WIDTH), c2), pl.BlockSpec((1, d), c2), pl.BlockSpec((1, d), c2),
                  pl.BlockSpec((d, d), c2), pl.BlockSpec((2, N_EXPERTS, d), lambda b, i: (0, 0, 0))],
        out_specs=[pl.BlockSpec((1, tm, d), row), pl.BlockSpec((1, tm, d // 2), row),
                   pl.BlockSpec((1, N_EXPERTS, tm), lambda b, i: (b, 0, i))],
        out_shape=[jax.ShapeDtypeStruct((bsz, n_rows, d), F32),
                   jax.ShapeDtypeStruct((bsz, n_rows, d // 2), jnp.uint32),
                   jax.ShapeDtypeStruct((bsz, N_EXPERTS, n_rows), F32)],
        compiler_params=_cparams(("parallel", "arbitrary"), VMEM_LIMIT),
        name="outproj_norms_router",
    )(fo, at, yf, yb, z, *h_args, mod_l.reshape(mod_l.shape[0], 1, 6, d), g_ssd.reshape(1, SSD_WIDTH),
      g_post.reshape(1, d), g_pre.reshape(1, d), w_out_p, wr)


def _prefix_lanes(mask, upper):
    n_r, t = mask.shape
    nb = t // LANE
    if (nb * n_r) % LANE:
        carry = jnp.zeros((n_r, 1), F32)
        outs = []
        for j in range(nb):
            pre = jnp.dot(mask[:, j * LANE:(j + 1) * LANE].astype(BF16), upper, preferred_element_type=F32) + carry
            outs.append(pre)
            carry = pre[:, LANE - 1:LANE]
        return jnp.concatenate(outs, axis=1)
    stacked = jnp.concatenate([mask[:, j * LANE:(j + 1) * LANE] for j in range(nb)], axis=0).astype(BF16)
    pre = jnp.dot(stacked, upper, preferred_element_type=F32)
    ri = lax.broadcasted_iota(jnp.int32, (LANE, LANE), 0)
    last_row = (ri == LANE - 1).astype(BF16)
    tot = jnp.dot(pre.astype(BF16), last_row, preferred_element_type=F32)
    ra = lax.broadcasted_iota(jnp.int32, (nb * n_r, nb * n_r), 0)
    ca = lax.broadcasted_iota(jnp.int32, (nb * n_r, nb * n_r), 1)
    assert n_r & (n_r - 1) == 0
    earlier = (((ca ^ ra) & (n_r - 1)) == 0) & (ca < ra)
    earlier = earlier.astype(BF16)
    full = pre + jnp.dot(earlier, tot.astype(BF16), preferred_element_type=F32)
    return jnp.concatenate([full[j * n_r:(j + 1) * n_r] for j in range(nb)], axis=1)


def _slot_tokens(csum, cap):
    n_r, t = csum.shape
    nb = t // LANE
    slot = lax.broadcasted_iota(jnp.int32, (1, cap), 1).astype(F32)
    rows = []
    if nb < 8:
        csum_t = jnp.concatenate([csum, jnp.zeros((LANE - n_r, t), F32)], axis=0).T
        for x in range(n_r):
            rows.append(jnp.sum((csum_t[:, x:x + 1] <= slot).astype(F32), axis=0, keepdims=True))
        return jnp.concatenate(rows, axis=0)
    blk_id = lax.broadcasted_iota(jnp.int32, (LANE, 1), 0).astype(F32)
    never = jnp.full((LANE - nb, LANE), float(t), F32)
    for x in range(n_r):
        cx = jnp.concatenate([csum[x:x + 1, j * LANE:(j + 1) * LANE] for j in range(nb)] + [never], axis=0)
        blk = jnp.sum((cx[:, LANE - 1:LANE] <= slot).astype(F32), axis=0, keepdims=True)
        pick = (blk_id == blk).astype(BF16)
        g = sum(jnp.dot(p, pick, preferred_element_type=F32) for p in _split_bf16(cx.T, 2))
        within = jnp.sum((g <= slot).astype(F32), axis=0, keepdims=True)
        rows.append(blk * LANE + within)
    return jnp.concatenate(rows, axis=0)


def _route_kernel(lg_ref, idx_ref, aff_ref, *, cap):
    l = lg_ref[0]
    n_e, t = l.shape
    e = jnp.exp(l - jnp.max(l, axis=0, keepdims=True))
    aff = e / jnp.sum(e, axis=0, keepdims=True)
    aff_ref[0] = aff
    keys = pltpu.bitcast(aff, jnp.int32)

    def search(it, lo):
        cand = lo | (jnp.int32(1) << (30 - it))
        cnt = jnp.sum((keys >= cand).astype(F32), axis=1, keepdims=True)
        return jnp.where(cnt >= cap, cand, lo)

    thr = lax.fori_loop(0, 31, search, jnp.zeros((n_e, 1), jnp.int32))
    gt = keys > thr
    eq = keys == thr
    need = cap - jnp.sum(gt.astype(F32), axis=1, keepdims=True)
    ri = lax.broadcasted_iota(jnp.int32, (LANE, LANE), 0)
    ci = lax.broadcasted_iota(jnp.int32, (LANE, LANE), 1)
    upper = (ri <= ci).astype(BF16)
    eq_rank = _prefix_lanes(eq.astype(F32), upper)
    sel = gt | (eq & (eq_rank <= need))
    csum = _prefix_lanes(sel.astype(F32), upper)
    idx_ref[0] = _slot_tokens(csum, cap).astype(jnp.int32)


def _route(logits_t, row0, t, cap):
    bsz, n_e, _ = logits_t.shape
    return pl.pallas_call(
        functools.partial(_route_kernel, cap=cap),
        grid=(bsz,),
        in_specs=[pl.BlockSpec((1, n_e, t), lambda b: (b, 0, row0 // t))],
        out_specs=[pl.BlockSpec((1, n_e, cap), lambda b: (b, 0, 0)),
                   pl.BlockSpec((1, n_e, t), lambda b: (b, 0, 0))],
        out_shape=[jax.ShapeDtypeStruct((bsz, n_e, cap), jnp.int32),
                   jax.ShapeDtypeStruct((bsz, n_e, t), F32)],
        compiler_params=_cparams(("parallel",), VMEM_LIMIT),
        name="expert_choice_route",
    )(logits_t)


def _smem_row(n):
    return pl.BlockSpec((1, 1, 1, n), lambda b, e: (b, e, 0, 0), memory_space=pltpu.SMEM)


def _gather_kernel(idx_ref, u_ref, o_ref, *, cap):
    def body(i, carry):
        t = idx_ref[0, 0, 0, i]
        o_ref[0, pl.ds(i, 1), :] = u_ref[0, pl.ds(t, 1), :]
        return carry

    lax.fori_loop(0, cap, body, 0, unroll=8)


def _gather(u_packed, idx, row0, t, cap):
    bsz, _, half = u_packed.shape
    n_e = idx.shape[1]
    return pl.pallas_call(
        functools.partial(_gather_kernel, cap=cap),
        grid=(bsz, n_e),
        in_specs=[_smem_row(cap), pl.BlockSpec((1, t, half), lambda b, e: (b, row0 // t, 0))],
        out_specs=pl.BlockSpec((1, cap, half), lambda b, e: (e, b, 0)),
        out_shape=jax.ShapeDtypeStruct((n_e, bsz * cap, half), jnp.uint32),
        compiler_params=_cparams(("parallel", "arbitrary"), VMEM_LIMIT),
        name="expert_gather",
    )(idx[:, :, None, :], u_packed)


FF_TILE = 256


def _ffn_kernel(x_ref, wg_ref, wu_ref, wd_ref, o_ref, xs_ref):
    f = pl.program_id(2)

    @pl.when(f == 0)
    def _():
        w = x_ref[0]
        half = w.shape[1]
        xs_ref[:, :half] = pltpu.bitcast(w << 16, F32).astype(BF16)
        xs_ref[:, half:] = pltpu.bitcast(w & jnp.uint32(0xFFFF0000), F32).astype(BF16)
        o_ref[...] = jnp.zeros_like(o_ref)

    x = xs_ref[...]
    g = jnp.dot(x, wg_ref[0, 0].astype(BF16), preferred_element_type=F32)
    u = jnp.dot(x, wu_ref[0, 0].astype(BF16), preferred_element_type=F32)
    o_ref[0] += jnp.dot((_silu(g) * u).astype(BF16), wd_ref[0, 0].astype(BF16), preferred_element_type=F32)


def _expert_ffn(xe, w_gate, w_up, w_down, layer):
    n_e, m, half = xe.shape
    d = 2 * half
    ff = w_gate.shape[-1]
    tm = min(m, 2048)
    return pl.pallas_call(
        _ffn_kernel,
        grid=(n_e, m // tm, ff // FF_TILE),
        in_specs=[pl.BlockSpec((1, tm, half), lambda e, i, f: (e, i, 0)),
                  pl.BlockSpec((1, 1, d, FF_TILE), lambda e, i, f: (layer, e, 0, f)),
                  pl.BlockSpec((1, 1, d, FF_TILE), lambda e, i, f: (layer, e, 0, f)),
                  pl.BlockSpec((1, 1, FF_TILE, d), lambda e, i, f: (layer, e, f, 0))],
        out_specs=pl.BlockSpec((1, tm, d), lambda e, i, f: (e, i, 0)),
        out_shape=jax.ShapeDtypeStruct((n_e, m, d), F32),
        scratch_shapes=[pltpu.VMEM((tm, d), BF16)],
        compiler_params=_cparams(("parallel", "parallel", "arbitrary"), VMEM_LIMIT),
        name="expert_ffn",
    )(xe, w_gate, w_up, w_down)


COMB_ROWS = 8


def _combine_kernel(idx_ref, aff_ref, y_ref, *rest, cap):
    o_ref = rest[-1]

    @pl.when(pl.program_id(1) == 0)
    def _():
        o_ref[...] = jnp.zeros_like(o_ref)

    def body(c, carry):
        base = pl.multiple_of(c * COMB_ROWS, COMB_ROWS)
        toks = [idx_ref[0, 0, 0, base + r] for r in range(COMB_ROWS)]
        rows = [o_ref[0, pl.ds(t, 1), :] + aff_ref[0, 0, 0, t] * y_ref[0, pl.ds(base + r, 1), :]
                for r, t in enumerate(toks)]
        for t, row in zip(toks, rows):
            o_ref[0, pl.ds(t, 1), :] = row
        return carry

    lax.fori_loop(0, cap // COMB_ROWS, body, 0)


def _combine(y, idx, aff, row0, t, cap, s_total, out_prev=None):
    n_e, m, d = y.shape
    bsz = idx.shape[0]
    in_specs = [_smem_row(cap), _smem_row(t), pl.BlockSpec((1, cap, d), lambda b, e: (e, b, 0))]
    args = [idx[:, :, None, :], aff[:, :, None, :], y]
    aliases = {}
    if out_prev is not None:
        in_specs.append(pl.BlockSpec(memory_space=pl.ANY))
        args.append(out_prev)
        aliases = {3: 0}
    return pl.pallas_call(
        functools.partial(_combine_kernel, cap=cap),
        grid=(bsz, n_e),
        in_specs=in_specs,
        out_specs=pl.BlockSpec((1, t, d), lambda b, e: (b, row0 // t, 0)),
        out_shape=jax.ShapeDtypeStruct((bsz, s_total, d), F32),
        input_output_aliases=aliases,
        compiler_params=_cparams(("parallel", "arbitrary"), VMEM_LIMIT),
        name="expert_combine",
    )(*args)


def _moe(u_packed, logits_t, w_gate, w_up, w_down, layer, with_ctx):
    bsz, rows, _ = u_packed.shape
    out = None
    sets = [(0, SEQ)] + ([(SEQ, CTX_LEN)] if with_ctx else [])
    for row0, t in sets:
        cap = EC_CAPACITY * t // N_EXPERTS
        idx, aff = _route(logits_t, row0, t, cap)
        xe = _gather(u_packed, idx, row0, t, cap)
        y = _expert_ffn(xe, w_gate, w_up, w_down, layer)
        out = _combine(y, idx, aff, row0, t, cap, rows, out)
    return out


def _final_kernel(h_ref, f_ref, mod_ref, g_ref, o_ref):
    o_ref[0] = h_ref[0] + mod_ref[0, 0][5:6] * _rms(f_ref[0], g_ref[...])


def _final_residual(h, f, mod_l, g_post):
    bsz, n, d = h.shape
    tm = ROW_TILE
    row = lambda b, i: (b, i, 0)
    return pl.pallas_call(
        _final_kernel,
        grid=(bsz, n // tm),
        in_specs=[pl.BlockSpec((1, tm, d), row), pl.BlockSpec((1, tm, d), row),
                  pl.BlockSpec((1, 1, 6, d), lambda b, i: (b, 0, 0, 0)), pl.BlockSpec((1, d), lambda b, i: (0, 0))],
        out_specs=pl.BlockSpec((1, tm, d), row),
        out_shape=jax.ShapeDtypeStruct((bsz, n, d), F32),
        compiler_params=_cparams(("parallel", "arbitrary")),
        name="ffn_residual",
    )(h, f, mod_l.reshape(mod_l.shape[0], 1, 6, d), g_post.reshape(1, d))


def _rope_tables():
    n = jnp.arange(SEQ, dtype=jnp.int32)
    row = (n // GRID_W).astype(F32)
    col = (n % GRID_W).astype(F32)
    quarter = HEAD_DIM // 4
    inv = ROPE_THETA ** (-jnp.arange(quarter, dtype=F32) / quarter)
    lane = jnp.arange(LANE)
    pos = jnp.where((lane % HEAD_DIM < HEAD_DIM // 2)[None, :], row[:, None], col[:, None])
    ang = pos * inv[lane % quarter][None, :]
    first = (lane % (2 * quarter) < quarter)[None, :]
    cos = jnp.cos(ang)
    sin = jnp.sin(ang)
    sa = jnp.where(first, -sin, 0.0)
    sb = jnp.where(first, 0.0, sin)
    padc = lambda t, v: jnp.concatenate([t, jnp.full((CTX_LEN, LANE), v, F32)], axis=0)
    return padc(cos, 1.0), padc(sa, 0.0), padc(sb, 0.0)


def kernel(x, c, ctx, c_ctx, w_ada, b_ada, g_mix_pre, g_mix_post, g_ffn_pre, g_ffn_post, w_in, conv_w, conv_b,
           dt_bias, a_log, d_skip, g_ssd, attn_sink, w_out, w_router, w_gate, w_up, w_down):
    depth = w_ada.shape[0]
    bsz = x.shape[0]
    d = D_MODEL
    cc = jnp.concatenate([c, c_ctx[None, :], jnp.zeros((7, d), F32)], axis=0)
    mod = _modulation(cc, w_ada, b_ada).reshape(depth, bsz + 8, 6, d)
    rope_tabs = _rope_tables()
    chan_tab = _channel_tables()
    cn_lat, sn_lat = _dft_tables(SEQ)
    cn_ctx, sn_ctx = _dft_tables(CTX_LEN)
    q0 = F_WIDTH

    h = (x, ctx)
    f_prev = None
    for l in range(depth):
        last = l == depth - 1
        w_q = _to_slab_order(w_in[l][:, q0:q0 + ATT_WIDTH], 1) * HEAD_DIM ** -0.5
        w_in_p = jnp.concatenate([w_in[l][:, :q0], w_q, w_in[l][:, q0 + ATT_WIDTH:],
                                  jnp.zeros((d, IN_PAD - IN_WIDTH), F32)], axis=1).astype(BF16)
        w_out_p = jnp.concatenate([w_out[l][:q0], _to_slab_order(w_out[l][q0:q0 + ATT_WIDTH], 0),
                                   w_out[l][q0 + ATT_WIDTH:]], axis=0).astype(BF16)
        outs = _in_proj(h, mod[l], mod[l - 1] if l else None, g_mix_pre[l], w_in_p, rope_tabs,
                        f_prev, g_ffn_post[l - 1] if l else None)
        if f_prev is not None:
            h, outs = outs[0], outs[1:]
        fo, q, k, v, z, xbc, dt_raw = outs
        fmix = _fourier(fo, chan_tab, cn_lat, sn_lat, 0, SEQ, out_rows=SEQ if last else S_ALL)
        if not last:
            fmix = _fourier(fo, chan_tab, cn_ctx, sn_ctx, SEQ, CTX_LEN, fmix)
        att = _attention(q, k, v, attn_sink[l], not last)
        y_f, y_b = _ssd_mixer(xbc, dt_raw, conv_w[l], conv_b[l], dt_bias[l], a_log[l], d_skip[l])
        n_rows = SEQ if last else S_ALL
        h, u_packed, logits_t = _out_proj(fmix, att, y_f, y_b, z, h, mod[l], g_ssd[l],
                                          g_mix_post[l], g_ffn_pre[l], w_out_p, w_router[l], n_rows)
        f_prev = _moe(u_packed, logits_t, w_gate, w_up, w_down, l, not last)
    return _final_residual(h, f_prev, mod[depth - 1], g_ffn_post[depth - 1])
```

```python
import functools
import math

import jax
import jax.numpy as jnp
from jax import lax
from jax.experimental import pallas as pl
from jax.experimental.pallas import tpu as pltpu

F32 = jnp.float32
BF16 = jnp.bfloat16

D_MODEL = 1024
SEQ = 4096
CTX_LEN = 256
S_ALL = SEQ + CTX_LEN
GRID_W = 64
HEAD_DIM = 64
EPS = 1e-6
F_WIDTH = 256
ATT_HEADS = 6
ATT_KV_HEADS = 2
ATT_WIDTH = 384
KV_WIDTH = 128
BLOCK = 128
ROPE_THETA = 10000.0
SSD_HEADS = 6
SSD_WIDTH = 384
SSD_GROUPS = 2
SSD_STATE = 128
CHUNK = 128
CONV_K = 5
CONV_DIM = 896
IN_WIDTH = 2188
IN_PAD = 2304
N_EXPERTS = 16
EXPERT_FF = 2816
EC_CAPACITY = 2

LANE = 128
ROW_TILE = 512
VMEM_LIMIT = 56 * 1024 * 1024


def _cparams(sem, vmem=None):
    return pltpu.CompilerParams(dimension_semantics=sem, vmem_limit_bytes=vmem)


def _split_bf16(x, n):
    parts, r = [], x
    for _ in range(n):
        p = r.astype(BF16)
        parts.append(p)
        r = r - p.astype(F32)
    return parts


def _rms(x, g):
    return x * lax.rsqrt(jnp.mean(x * x, axis=-1, keepdims=True) + EPS) * g


def _silu(x):
    return x * jax.nn.sigmoid(x)


def _mod_kernel(cc_ref, w_ref, b_ref, o_ref):
    s = _silu(cc_ref[...])
    o_ref[0] = jnp.dot(s.astype(BF16), w_ref[0].astype(BF16), preferred_element_type=F32) + b_ref[0]


def _modulation(cc, w_ada, b_ada):
    n_l, d, d6 = w_ada.shape
    r = cc.shape[0]
    tn = 1024
    return pl.pallas_call(
        _mod_kernel,
        grid=(n_l, d6 // tn),
        in_specs=[pl.BlockSpec((r, d), lambda l, j: (0, 0)),
                  pl.BlockSpec((1, d, tn), lambda l, j: (l, 0, j)),
                  pl.BlockSpec((1, 1, tn), lambda l, j: (l, 0, j))],
        out_specs=pl.BlockSpec((1, r, tn), lambda l, j: (l, 0, j)),
        out_shape=jax.ShapeDtypeStruct((n_l, r, d6), F32),
        compiler_params=_cparams(("arbitrary", "arbitrary")),
        name="adaln_mod",
    )(cc, w_ada, b_ada.reshape(n_l, 1, d6))


def _stream_specs(h, tm):
    n_lat = SEQ // tm
    if isinstance(h, tuple):
        lat, ctx = h
        d = lat.shape[-1]
        return ([pl.BlockSpec((1, tm, d), lambda b, i: (b, jnp.minimum(i, n_lat - 1), 0)),
                 pl.BlockSpec((1, CTX_LEN, d), lambda b, i: (b, 0, 0))], [lat, ctx])
    return [pl.BlockSpec((1, tm, h.shape[-1]), lambda b, i: (b, i, 0))], [h]


def _stream_tile(refs, tm):
    if len(refs) == 2:
        ctx = jnp.concatenate([refs[1][0]] * (tm // CTX_LEN), axis=0)
        return jnp.where(pl.program_id(1) < SEQ // tm, refs[0][0], ctx)
    return refs[0][0]


def _in_kernel(*refs, residual, n_h, ctx_fourier):
    h_refs, refs = refs[:n_h], refs[n_h:]
    if residual:
        f_ref, gpost_ref, refs = refs[0], refs[1], refs[2:]
    mod_ref, g_ref, w_ref, cos_ref, sa_ref, sb_ref, ch_ref = refs[:7]
    outs = list(refs[7:])
    if residual:
        hout_ref = outs.pop(0)
    if ctx_fourier:
        al_ref, bl_ref, ac_ref, bc_ref, q_ref, k_ref, v_ref, z_ref, xbc_ref, dt_ref = outs
    else:
        al_ref, bl_ref, q_ref, k_ref, v_ref, z_ref, xbc_ref, dt_ref = outs
    x = _stream_tile(h_refs, ROW_TILE)
    mod = mod_ref[0, 0]
    if residual:
        x = x + mod[5:6] * _rms(f_ref[0], gpost_ref[...])
        hout_ref[0] = x
    u = _rms(x, g_ref[...]) * (1.0 + mod[1:2]) + mod[0:1]
    p = jnp.dot(u.astype(BF16), w_ref[...], preferred_element_type=F32)
    cos, sa, sb = cos_ref[...], sa_ref[...], sb_ref[...]

    def rope(t):
        return t * cos + pltpu.roll(t, LANE - 16, 1) * sa + pltpu.roll(t, 16, 1) * sb

    ab = jnp.dot(p[:, 0:F_WIDTH].astype(BF16), ch_ref[...], preferred_element_type=F32).astype(BF16)
    is_lat = pl.program_id(1) < SEQ // ROW_TILE

    @pl.when(is_lat)
    def _():
        al_ref[...] = ab[:, :F_WIDTH]
        bl_ref[...] = ab[:, F_WIDTH:]

    if ctx_fourier:
        @pl.when(jnp.logical_not(is_lat))
        def _():
            ac_ref[...] = ab[:CTX_LEN, :F_WIDTH]
            bc_ref[...] = ab[:CTX_LEN, F_WIDTH:]

    q_ref[0] = jnp.concatenate([rope(p[:, 256 + LANE * s:256 + LANE * (s + 1)]) for s in range(3)],
                               axis=1).astype(BF16)
    k_ref[0] = rope(p[:, 640:768]).astype(BF16)
    v_ref[0] = p[:, 768:896].astype(BF16)
    z_ref[0] = p[:, 896:1280].astype(BF16)
    xbc_ref[0] = p[:, 1280:2176].astype(BF16)
    dt_ref[0] = p[:, 2176:2304]


def _in_proj(h, mod_l, mod_prev, g_pre, w_in_p, rope_tabs, chan_tab, ctx_fourier, f_prev=None, g_post_prev=None):
    bsz = mod_l.shape[0] - 8
    s, d = S_ALL, D_MODEL
    tm = ROW_TILE
    n_lat = SEQ // tm
    residual = f_prev is not None
    row = lambda b, i: (b, i, 0)
    mod_map = lambda b, i: (jnp.where(i < n_lat, b, bsz), 0, 0, 0)
    const2 = lambda b, i: (0, 0)
    tab = pl.BlockSpec((tm, LANE), lambda b, i: (i, 0))
    in_specs, args = _stream_specs(h, tm)
    n_h = len(args)
    if residual:
        in_specs += [pl.BlockSpec((1, tm, d), row), pl.BlockSpec((1, d), const2)]
        args += [f_prev, g_post_prev.reshape(1, d)]
    mod_used = mod_l if not residual else jnp.concatenate([mod_l[:, :5], mod_prev[:, 5:6]], axis=1)
    in_specs += [pl.BlockSpec((1, 1, 6, d), mod_map), pl.BlockSpec((1, d), const2),
                 pl.BlockSpec((d, IN_PAD), const2), tab, tab, tab, pl.BlockSpec(chan_tab.shape, const2)]
    args += [mod_used.reshape(mod_used.shape[0], 1, 6, d), g_pre.reshape(1, d), w_in_p, *rope_tabs, chan_tab]
    widths = [(384, BF16), (128, BF16), (128, BF16), (384, BF16), (896, BF16), (128, F32)]
    out_specs = [pl.BlockSpec((1, tm, w), row) for w, _ in widths]
    out_shape = [jax.ShapeDtypeStruct((bsz, s, w), dt) for w, dt in widths]
    lat_spec = pl.BlockSpec((tm, F_WIDTH), lambda b, i: (jnp.minimum(i, n_lat - 1), b))
    four_specs = [lat_spec, lat_spec]
    four_shape = [jax.ShapeDtypeStruct((SEQ, bsz * F_WIDTH), BF16)] * 2
    if ctx_fourier:
        four_specs += [pl.BlockSpec((CTX_LEN, F_WIDTH), lambda b, i: (0, b))] * 2
        four_shape += [jax.ShapeDtypeStruct((CTX_LEN, bsz * F_WIDTH), BF16)] * 2
    out_specs = four_specs + out_specs
    out_shape = four_shape + out_shape
    if residual:
        out_specs = [pl.BlockSpec((1, tm, d), row)] + out_specs
        out_shape = [jax.ShapeDtypeStruct((bsz, s, d), F32)] + out_shape
    outs = pl.pallas_call(
        functools.partial(_in_kernel, residual=residual, n_h=n_h, ctx_fourier=ctx_fourier),
        grid=(bsz, pl.cdiv(s, tm)),
        in_specs=in_specs, out_specs=out_specs, out_shape=out_shape,
        compiler_params=_cparams(("parallel", "arbitrary"), VMEM_LIMIT),
        name="prenorm_inproj",
    )(*args)
    outs = list(outs)
    head = [outs.pop(0)] if residual else []
    lat_ab = (outs.pop(0), outs.pop(0))
    ctx_ab = (outs.pop(0), outs.pop(0)) if ctx_fourier else None
    return head + [lat_ab, ctx_ab] + outs


def _f2_kernel(c_ref, s_ref, a_ref, b_ref, *rest):
    o_ref = rest[-1]
    y = (jnp.dot(c_ref[...], a_ref[...], preferred_element_type=F32)
         - jnp.dot(s_ref[...], b_ref[...], preferred_element_type=F32))
    o_ref[0] = y.astype(BF16)


def _dft_tables(n):
    sc = n ** -0.5

    def cs(rows, cols, period):
        m = (rows[:, None] * cols[None, :]) % period
        ang = m.astype(F32) * (2.0 * math.pi / period)
        return jnp.cos(ang), jnp.sin(ang)

    k = jnp.arange(n, dtype=jnp.int32)
    r = 64
    if n <= r * r // 16:
        c, s = cs(k, k, n)
        return (c * sc).astype(BF16), (s * sc).astype(BF16)
    ca, sa = cs(k, jnp.arange(n // r, dtype=jnp.int32), n // r)
    cb, sb = cs(k, jnp.arange(r, dtype=jnp.int32), n)
    c = ca[:, :, None] * cb[:, None, :] - sa[:, :, None] * sb[:, None, :]
    s = sa[:, :, None] * cb[:, None, :] + ca[:, :, None] * sb[:, None, :]
    return (c * sc).reshape(n, n).astype(BF16), (s * sc).reshape(n, n).astype(BF16)


def _channel_tables():
    c64, s64 = _dft_tables(HEAD_DIM)
    eye = jnp.eye(F_WIDTH // HEAD_DIM, dtype=BF16)
    return jnp.concatenate([jnp.kron(eye, c64), jnp.kron(eye, s64)], axis=1)


F2_EXT = 16


def _f2_sym_kernel(cm_ref, ce_ref, sm_ref, se_ref, a_ref, b_ref, jr_ref, o_ref, p_ref, q_ref, *, tm, nt):
    i = pl.program_id(1)
    half = nt // 2

    @pl.when(i == 0)
    def _():
        tail = half * tm + F2_EXT
        p_ref[tail:, :] = jnp.zeros((LANE - F2_EXT, p_ref.shape[1]), F32)
        q_ref[tail:, :] = jnp.zeros((LANE - F2_EXT, q_ref.shape[1]), F32)

    def emit(y):
        for j in range(o_ref.shape[0]):
            o_ref[j] = y[:, j * F_WIDTH:(j + 1) * F_WIDTH].astype(BF16)

    @pl.when(i < half)
    def _():
        a, b = a_ref[...], b_ref[...]
        row = pl.multiple_of(i * tm, tm)
        p = jnp.dot(cm_ref[...], a, preferred_element_type=F32)
        q = jnp.dot(sm_ref[...], b, preferred_element_type=F32)
        p_ref[pl.ds(row, tm), :] = p
        q_ref[pl.ds(row, tm), :] = q
        p_ref[pl.ds(row + tm, F2_EXT), :] = jnp.dot(ce_ref[...], a, preferred_element_type=F32)
        q_ref[pl.ds(row + tm, F2_EXT), :] = jnp.dot(se_ref[...], b, preferred_element_type=F32)
        emit(p - q)

    @pl.when(i >= half)
    def _():
        row = pl.multiple_of((nt - 1 - i) * tm, tm)
        s = (p_ref[pl.ds(row, tm + LANE), :] + q_ref[pl.ds(row, tm + LANE), :]).astype(BF16)
        emit(jnp.dot(jr_ref[...], s, preferred_element_type=F32))


def _fourier_positions_sym(cn, sn, a_all, b_all, bsz, n, tm, blk0, out_rows):
    nt = n // tm
    half = nt // 2
    per = tm // F2_EXT
    main = lambda b, i: (jnp.minimum(i, half - 1), 0)
    ext = lambda b, i: ((jnp.minimum(i, half - 1) + 1) * per, 0)
    r = jnp.arange(tm)[:, None]
    flip = (jnp.arange(tm + LANE)[None, :] == tm - r).astype(BF16)
    grp = 2 if bsz % 2 == 0 else 1
    cols = grp * F_WIDTH
    return pl.pallas_call(
        functools.partial(_f2_sym_kernel, tm=tm, nt=nt),
        grid=(bsz // grp, nt),
        in_specs=[pl.BlockSpec((tm, n), main), pl.BlockSpec((F2_EXT, n), ext),
                  pl.BlockSpec((tm, n), main), pl.BlockSpec((F2_EXT, n), ext),
                  pl.BlockSpec((n, cols), lambda b, i: (0, b)),
                  pl.BlockSpec((n, cols), lambda b, i: (0, b)),
                  pl.BlockSpec((tm, tm + LANE), lambda b, i: (0, 0))],
        out_specs=pl.BlockSpec((grp, tm, F_WIDTH), lambda b, i: (b, blk0 + i, 0)),
        out_shape=jax.ShapeDtypeStruct((bsz, out_rows, F_WIDTH), BF16),
        scratch_shapes=[pltpu.VMEM((n // 2 + LANE, cols), F32)] * 2,
        compiler_params=_cparams(("parallel", "arbitrary"), VMEM_LIMIT),
        name="fourier_positions",
    )(cn, cn, sn, sn, a_all, b_all, flip)


def _fourier(ab, cn, sn, row0, out_prev=None, out_rows=S_ALL):
    a_all, b_all = ab
    n = a_all.shape[0]
    bsz = a_all.shape[1] // F_WIDTH
    tm = min(n, 512)
    nt = n // tm
    blk0 = row0 // tm
    if nt >= 4 and out_prev is None:
        return _fourier_positions_sym(cn, sn, a_all, b_all, bsz, n, tm, blk0, out_rows)
    in_specs = [pl.BlockSpec((tm, n), lambda i, b: (i, 0)),
                pl.BlockSpec((tm, n), lambda i, b: (i, 0)),
                pl.BlockSpec((n, F_WIDTH), lambda i, b: (0, b)),
                pl.BlockSpec((n, F_WIDTH), lambda i, b: (0, b))]
    args = [cn, sn, a_all, b_all]
    aliases = {}
    if out_prev is not None:
        in_specs.append(pl.BlockSpec(memory_space=pl.ANY))
        args.append(out_prev)
        aliases = {4: 0}
    return pl.pallas_call(
        _f2_kernel,
        grid=(nt, bsz),
        in_specs=in_specs,
        out_specs=pl.BlockSpec((1, tm, F_WIDTH), lambda i, b: (b, blk0 + i, 0)),
        out_shape=jax.ShapeDtypeStruct((bsz, out_rows, F_WIDTH), BF16),
        input_output_aliases=aliases,
        compiler_params=_cparams(("parallel", "arbitrary"), VMEM_LIMIT),
        name="fourier_positions",
    )(*args)


def _to_slab_order(w, axis):
    r = ATT_HEADS // ATT_KV_HEADS
    shape = w.shape
    w = w.reshape(shape[:axis] + (ATT_KV_HEADS, r, HEAD_DIM) + shape[axis + 1:])
    return jnp.swapaxes(w, axis, axis + 1).reshape(shape)


def _attend(q, k, v, masks, sink_ref):
    n = q.shape[0]
    r = ATT_HEADS // ATT_KV_HEADS
    lo = lax.broadcasted_iota(jnp.int32, (1, LANE), 1) < HEAD_DIM
    rid = lax.broadcasted_iota(jnp.int32, (r * n, 1), 0)
    outs = []
    for g in range(ATT_KV_HEADS):
        keep = lo if g == 0 else jnp.logical_not(lo)
        qg = jnp.concatenate([jnp.where(keep, q[:, LANE * s:LANE * (s + 1)], jnp.zeros((), BF16))
                              for s in range(r)], axis=0)
        snk = jnp.full((r * n, 1), sink_ref[r * g], F32)
        for j in range(1, r):
            snk = jnp.where(rid >= j * n, sink_ref[r * g + j], snk)
        sc = lax.dot_general(qg, k, (((1,), (1,)), ((), ())), preferred_element_type=F32)
        if masks:
            nblk = sc.shape[1] // LANE
            sc = jnp.concatenate(
                [jnp.where(masks[b], sc[:, b * LANE:(b + 1) * LANE], -1e30) if b in masks
                 else sc[:, b * LANE:(b + 1) * LANE] for b in range(nblk)], axis=1)
        mx = jnp.maximum(jnp.max(sc, axis=-1, keepdims=True), snk)
        p = jnp.exp(sc - mx)
        den = jnp.sum(p, axis=-1, keepdims=True) + jnp.exp(snk - mx)
        outs.append(jnp.dot(p.astype(BF16), v, preferred_element_type=F32) / den)
    return jnp.concatenate([jnp.where(lo, outs[0][s * n:(s + 1) * n], outs[1][s * n:(s + 1) * n])
                            for s in range(r)], axis=1)


def _attn_lat_kernel(sink_ref, q_ref, kp_ref, kc_ref, kn_ref, kx_ref, vp_ref, vc_ref, vn_ref, vx_ref, o_ref):
    nb = pl.num_programs(1)
    n = pl.program_id(1)
    rows = (ATT_HEADS // ATT_KV_HEADS) * BLOCK
    i = lax.broadcasted_iota(jnp.int32, (rows, BLOCK), 0) & (BLOCK - 1)
    j = lax.broadcasted_iota(jnp.int32, (rows, BLOCK), 1)
    m_prev = (j >= i) & (n > 0)
    m_next = (j <= i) & (n < nb - 1)
    k = jnp.concatenate([kp_ref[0], kc_ref[0], kn_ref[0], kx_ref[0]], axis=0)
    v = jnp.concatenate([vp_ref[0], vc_ref[0], vn_ref[0], vx_ref[0]], axis=0)
    o_ref[0] = _attend(q_ref[0], k, v, {0: m_prev, 2: m_next}, sink_ref).astype(BF16)


def _attn_ctx_kernel(sink_ref, q_ref, kx_ref, vx_ref, prev_ref, o_ref):
    o_ref[0] = _attend(q_ref[0], kx_ref[0], vx_ref[0], {}, sink_ref).astype(BF16)


def _attention(q, k, v, sink, with_ctx):
    bsz = q.shape[0]
    s = S_ALL if with_ctx else SEQ
    nb = SEQ // BLOCK
    ctx_blk = SEQ // CTX_LEN
    smem = pl.BlockSpec(memory_space=pltpu.SMEM)
    kv = lambda f: pl.BlockSpec((1, BLOCK, KV_WIDTH), f)
    prev = lambda b, n: (b, jnp.maximum(n - 1, 0), 0)
    cur = lambda b, n: (b, n, 0)
    nxt = lambda b, n: (b, jnp.minimum(n + 1, nb - 1), 0)
    ctx = pl.BlockSpec((1, CTX_LEN, KV_WIDTH), lambda b, n: (b, ctx_blk, 0))
    sink8 = jnp.pad(sink, (0, 8 - ATT_HEADS))
    out = pl.pallas_call(
        _attn_lat_kernel,
        grid=(bsz, nb),
        in_specs=[smem, pl.BlockSpec((1, BLOCK, ATT_WIDTH), cur),
                  kv(prev), kv(cur), kv(nxt), ctx, kv(prev), kv(cur), kv(nxt), ctx],
        out_specs=pl.BlockSpec((1, BLOCK, ATT_WIDTH), cur),
        out_shape=jax.ShapeDtypeStruct((bsz, s, ATT_WIDTH), BF16),
        compiler_params=_cparams(("parallel", "arbitrary")),
        name="window_attention",
    )(sink8, q, k, k, k, k, v, v, v, v)
    if not with_ctx:
        return out
    cq = lambda b: (b, ctx_blk, 0)
    return pl.pallas_call(
        _attn_ctx_kernel,
        grid=(bsz,),
        in_specs=[smem, pl.BlockSpec((1, CTX_LEN, ATT_WIDTH), cq),
                  pl.BlockSpec((1, CTX_LEN, KV_WIDTH), cq), pl.BlockSpec((1, CTX_LEN, KV_WIDTH), cq),
                  pl.BlockSpec(memory_space=pl.ANY)],
        out_specs=pl.BlockSpec((1, CTX_LEN, ATT_WIDTH), cq),
        out_shape=jax.ShapeDtypeStruct((bsz, s, ATT_WIDTH), BF16),
        input_output_aliases={4: 0},
        compiler_params=_cparams(("parallel",)),
        name="context_attention",
    )(sink8, q, k, v, out)


HALO = 16
N_SHIFT = CONV_K - 1


def _shift_table():
    i = jnp.arange(CHUNK)[:, None]
    j = jnp.arange(CHUNK + 2 * HALO)[None, :]
    taps = [k for k in range(CONV_K) if k != CONV_K // 2]
    return jnp.concatenate([(j == i + k - CONV_K // 2 + HALO) for k in taps], axis=0).astype(BF16)


def _expand_table(lane0):
    src = jnp.arange(LANE)[:, None] - lane0
    return (src == (jnp.arange(SSD_WIDTH)[None, :] // HEAD_DIM)).astype(BF16)


def _conv_chunk(c, xp_ref, xc_ref, xn_ref, sh_ref, w_ref, b_ref):
    n_lat = SEQ // CHUNK
    last = S_ALL // CHUNK - 1
    has_prev = (c != 0) & (c != n_lat)
    has_next = (c != n_lat - 1) & (c != last)
    zero = jnp.zeros((), BF16)
    cur = xc_ref[0]
    xx = jnp.concatenate([jnp.where(has_prev, xp_ref[0], zero), cur, jnp.where(has_next, xn_ref[0], zero)], axis=0)
    sh = jnp.dot(sh_ref[...], xx, preferred_element_type=F32)
    mid = CONV_K // 2
    acc = b_ref[...] + w_ref[mid:mid + 1, :] * cur.astype(F32)
    for t in range(N_SHIFT):
        k = t if t < mid else t + 1
        acc = acc + w_ref[k:k + 1, :] * sh[t * CHUNK:(t + 1) * CHUNK]
    return _silu(acc)


def _ssd_chunk(xc, dt_raw, dtb, alog, ex_ref, dsk, state_ref, reverse):
    xs = xc[:, :SSD_WIDTH].astype(F32)
    gs = SSD_GROUPS * SSD_STATE
    bm = xc[:, SSD_WIDTH:SSD_WIDTH + gs]
    cm = xc[:, SSD_WIDTH + gs:SSD_WIDTH + 2 * gs]
    raw = dt_raw + dtb
    dt = jnp.maximum(raw, 0.0) + jnp.log(1.0 + jnp.exp(-jnp.abs(raw)))
    dta = dt * (-jnp.exp(alog))
    ri = lax.broadcasted_iota(jnp.int32, (CHUNK, CHUNK), 0)
    ci = lax.broadcasted_iota(jnp.int32, (CHUNK, CHUNK), 1)
    keep = (ri <= ci) if reverse else (ri >= ci)
    tri = keep.astype(BF16)
    acs = sum(jnp.dot(tri, p, preferred_element_type=F32) for p in _split_bf16(dta, 3))
    acs_t = acs.T
    parts = jnp.concatenate(_split_bf16(dt, 2) + _split_bf16(acs, 3), axis=0)
    wide = jnp.dot(parts, ex_ref[...], preferred_element_type=F32)
    dt_x = wide[:CHUNK] + wide[CHUNK:2 * CHUNK]
    acs_x = wide[2 * CHUNK:3 * CHUNK] + wide[3 * CHUNK:4 * CHUNK] + wide[4 * CHUNK:]
    edge = 0 if reverse else CHUNK - 1
    tot_x = acs_x[edge:edge + 1, :]
    rh = SSD_HEADS // SSD_GROUPS
    gw = rh * HEAD_DIM
    lane0 = SSD_HEADS if reverse else 0
    xd = xs * dt_x
    xdw = (xd * jnp.exp(tot_x - acs_x)).astype(BF16)
    xdb = xd.astype(BF16)
    eacs = jnp.exp(acs_x)
    state = state_ref[...]
    ys = []
    new_state = []
    for g in range(SSD_GROUPS):
        bg = bm[:, g * SSD_STATE:(g + 1) * SSD_STATE]
        cg = cm[:, g * SSD_STATE:(g + 1) * SSD_STATE]
        cb = lax.dot_general(cg, bg, (((1,), (1,)), ((), ())), preferred_element_type=F32)
        sg = state[:, g * gw:(g + 1) * gw]
        y_off = jnp.dot(cg, sg.astype(BF16), preferred_element_type=F32) * eacs[:, g * gw:(g + 1) * gw]
        for r in range(rh):
            h = g * rh + r
            ln = lane0 + h
            dec = jnp.where(keep, jnp.exp(acs[:, ln:ln + 1] - acs_t[ln:ln + 1, :]), 0.0)
            yd = jnp.dot((cb * dec).astype(BF16), xdb[:, h * HEAD_DIM:(h + 1) * HEAD_DIM],
                         preferred_element_type=F32)
            ys.append(yd + y_off[:, r * HEAD_DIM:(r + 1) * HEAD_DIM])
        bt = bg.astype(F32).T.astype(BF16)
        new_state.append(jnp.dot(bt, xdw[:, g * gw:(g + 1) * gw], preferred_element_type=F32))
    state_ref[...] = state * jnp.exp(tot_x) + jnp.concatenate(new_state, axis=1)
    y = jnp.concatenate(ys, axis=1)
    return y if dsk is None else y + dsk * xs


def _ssd_kernel(fp_ref, fc_ref, fn_ref, fdt_ref, bp_ref, bc_ref, bn_ref, bdt_ref,
                sh_ref, w_ref, b_ref, dtb_ref, alog_ref, exf_ref, exb_ref, dsk_ref,
                yf_ref, yb_ref, sf_ref, sb_ref, xcache_ref):
    step = pl.program_id(1)
    nc = pl.num_programs(1)
    n_lat = SEQ // CHUNK

    @pl.when(step == 0)
    def _():
        sf_ref[...] = jnp.zeros_like(sf_ref)
        sb_ref[...] = jnp.zeros_like(sb_ref)

    cf = (step + n_lat) % nc
    cbk = nc - 1 - step

    @pl.when((step != 1) & (step < n_lat // 2 + 2))
    def _():
        xcache_ref[cf] = _conv_chunk(cf, fp_ref, fc_ref, fn_ref, sh_ref, w_ref, b_ref).astype(BF16)
        xcache_ref[cbk] = _conv_chunk(cbk, bp_ref, bc_ref, bn_ref, sh_ref, w_ref, b_ref).astype(BF16)

    dtb, alog = dtb_ref[...], alog_ref[...]
    yf_ref[0] = _ssd_chunk(xcache_ref[cf], fdt_ref[0], dtb, alog, exf_ref, dsk_ref[...], sf_ref,
                           False).astype(BF16)
    yb_ref[0] = _ssd_chunk(xcache_ref[cbk], bdt_ref[0], dtb, alog, exb_ref, None, sb_ref, True).astype(BF16)


def _ssd_mixer(xbc, dt_raw, conv_w, conv_b, dt_bias, a_log, d_skip):
    bsz, s, cdim = xbc.shape
    nc = s // CHUNK
    n_lat = SEQ // CHUNK
    per = CHUNK // HALO
    nh = s // HALO
    assert nc == n_lat + 2 and n_lat % 2 == 0
    fwd = lambda t: (t + n_lat) % nc
    bwd = lambda t: nc - 1 - t

    def chunk_specs(order):
        return [pl.BlockSpec((1, HALO, cdim), lambda b, t: (b, jnp.maximum(order(t) * per - 1, 0), 0)),
                pl.BlockSpec((1, CHUNK, cdim), lambda b, t: (b, order(t), 0)),
                pl.BlockSpec((1, HALO, cdim), lambda b, t: (b, jnp.minimum(order(t) * per + per, nh - 1), 0)),
                pl.BlockSpec((1, CHUNK, LANE), lambda b, t: (b, order(t), 0))]

    c2 = lambda b, t: (0, 0)
    full = lambda a: pl.BlockSpec(a.shape, c2)
    pad = lambda p: jnp.pad(p.reshape(1, -1), ((0, 0), (0, LANE - 2 * SSD_HEADS)))
    consts = [_shift_table(), jnp.pad(conv_w, ((0, 8 - CONV_K), (0, 0))), conv_b.reshape(1, cdim),
              pad(dt_bias), pad(a_log), _expand_table(0), _expand_table(SSD_HEADS),
              jnp.repeat(d_skip, HEAD_DIM).reshape(1, SSD_WIDTH)]
    return pl.pallas_call(
        _ssd_kernel,
        grid=(bsz, nc),
        in_specs=chunk_specs(fwd) + chunk_specs(bwd) + [full(a) for a in consts],
        out_specs=[pl.BlockSpec((1, CHUNK, SSD_WIDTH), lambda b, t: (b, fwd(t), 0)),
                   pl.BlockSpec((1, CHUNK, SSD_WIDTH), lambda b, t: (b, bwd(t), 0))],
        out_shape=[jax.ShapeDtypeStruct((bsz, s, SSD_WIDTH), BF16)] * 2,
        scratch_shapes=[pltpu.VMEM((SSD_STATE, SSD_WIDTH), F32)] * 2 + [pltpu.VMEM((nc, CHUNK, cdim), BF16)],
        compiler_params=_cparams(("parallel", "arbitrary"), VMEM_LIMIT),
        name="conv_ssd_scan",
    )(xbc, xbc, xbc, dt_raw, xbc, xbc, xbc, dt_raw, *consts)


def _out_kernel(*refs, n_h):
    (fo_ref, at_ref, yf_ref, yb_ref, z_ref), refs = refs[:5], refs[5:]
    h_refs, refs = refs[:n_h], refs[n_h:]
    mod_ref, gssd_ref, gpost_ref, gpre_ref, wo_ref, wr_ref, hout_ref, u_ref, lg_ref = refs
    y = (yf_ref[0].astype(F32) + yb_ref[0].astype(F32)) * _silu(z_ref[0].astype(F32))
    sm = _rms(y, gssd_ref[...]).astype(BF16)
    m = (jnp.dot(fo_ref[0], wo_ref[0:256, :], preferred_element_type=F32)
         + jnp.dot(at_ref[0], wo_ref[256:640, :], preferred_element_type=F32)
         + jnp.dot(sm, wo_ref[640:1024, :], preferred_element_type=F32))
    mod = mod_ref[0, 0]
    hn = _stream_tile(h_refs, ROW_TILE) + mod[2:3] * _rms(m, gpost_ref[...])
    hout_ref[0] = hn
    u = _rms(hn, gpre_ref[...]) * (1.0 + mod[4:5]) + mod[3:4]
    half = D_MODEL // 2
    lo = pltpu.bitcast(u[:, :half].astype(BF16).astype(F32), jnp.uint32) >> 16
    hi = pltpu.bitcast(u[:, half:].astype(BF16).astype(F32), jnp.uint32) & jnp.uint32(0xFFFF0000)
    u_ref[0] = lo | hi
    uh, ul = _split_bf16(u, 2)
    wh, wl = wr_ref[0], wr_ref[1]
    nt = (((1,), (1,)), ((), ()))
    lg_ref[0] = (lax.dot_general(wh, uh, nt, preferred_element_type=F32)
                 + lax.dot_general(wh, ul, nt, preferred_element_type=F32)
                 + lax.dot_general(wl, uh, nt, preferred_element_type=F32))


def _out_proj(fo, at, yf, yb, z, h, mod_l, g_ssd, g_post, g_pre, w_out_p, w_router, n_rows):
    bsz = fo.shape[0]
    d = D_MODEL
    tm = ROW_TILE
    n_lat = SEQ // tm
    row = lambda b, i: (b, i, 0)
    c2 = lambda b, i: (0, 0)
    wr = jnp.stack(_split_bf16(w_router.T, 2))
    h_specs, h_args = _stream_specs(h, tm)
    return pl.pallas_call(
        functools.partial(_out_kernel, n_h=len(h_args)),
        grid=(bsz, pl.cdiv(n_rows, tm)),
        in_specs=[pl.BlockSpec((1, tm, F_WIDTH), row), pl.BlockSpec((1, tm, ATT_WIDTH), row),
                  pl.BlockSpec((1, tm, SSD_WIDTH), row), pl.BlockSpec((1, tm, SSD_WIDTH), row),
                  pl.BlockSpec((1, tm, SSD_WIDTH), row)] + h_specs + [
                  pl.BlockSpec((1, 1, 6, d), lambda b, i: (jnp.where(i < n_lat, b, bsz), 0, 0, 0)),
                  pl.BlockSpec((1, SSD_WIDTH), c2), pl.BlockSpec((1, d), c2), pl.BlockSpec((1, d), c2),
                  pl.BlockSpec((d, d), c2), pl.BlockSpec((2, N_EXPERTS, d), lambda b, i: (0, 0, 0))],
        out_specs=[pl.BlockSpec((1, tm, d), row), pl.BlockSpec((1, tm, d // 2), row),
                   pl.BlockSpec((1, N_EXPERTS, tm), lambda b, i: (b, 0, i))],
        out_shape=[jax.ShapeDtypeStruct((bsz, n_rows, d), F32),
                   jax.ShapeDtypeStruct((bsz, n_rows, d // 2), jnp.uint32),
                   jax.ShapeDtypeStruct((bsz, N_EXPERTS, n_rows), F32)],
        compiler_params=_cparams(("parallel", "arbitrary"), VMEM_LIMIT),
        name="outproj_norms_router",
    )(fo, at, yf, yb, z, *h_args, mod_l.reshape(mod_l.shape[0], 1, 6, d), g_ssd.reshape(1, SSD_WIDTH),
      g_post.reshape(1, d), g_pre.reshape(1, d), w_out_p, wr)


def _prefix_lanes(mask, upper):
    n_r, t = mask.shape
    nb = t // LANE
    if (nb * n_r) % LANE:
        carry = jnp.zeros((n_r, 1), F32)
        outs = []
        for j in range(nb):
            pre = jnp.dot(mask[:, j * LANE:(j + 1) * LANE].astype(BF16), upper, preferred_element_type=F32) + carry
            outs.append(pre)
            carry = pre[:, LANE - 1:LANE]
        return jnp.concatenate(outs, axis=1)
    stacked = jnp.concatenate([mask[:, j * LANE:(j + 1) * LANE] for j in range(nb)], axis=0).astype(BF16)
    pre = jnp.dot(stacked, upper, preferred_element_type=F32)
    ri = lax.broadcasted_iota(jnp.int32, (LANE, LANE), 0)
    last_row = (ri == LANE - 1).astype(BF16)
    tot = jnp.dot(pre.astype(BF16), last_row, preferred_element_type=F32)
    ra = lax.broadcasted_iota(jnp.int32, (nb * n_r, nb * n_r), 0)
    ca = lax.broadcasted_iota(jnp.int32, (nb * n_r, nb * n_r), 1)
    assert n_r & (n_r - 1) == 0
    earlier = (((ca ^ ra) & (n_r - 1)) == 0) & (ca < ra)
    earlier = earlier.astype(BF16)
    full = pre + jnp.dot(earlier, tot.astype(BF16), preferred_element_type=F32)
    return jnp.concatenate([full[j * n_r:(j + 1) * n_r] for j in range(nb)], axis=1)


def _slot_tokens(csum, cap):
    n_r, t = csum.shape
    nb = t // LANE
    slot = lax.broadcasted_iota(jnp.int32, (1, cap), 1).astype(F32)
    rows = []
    if nb < 8:
        csum_t = jnp.concatenate([csum, jnp.zeros((LANE - n_r, t), F32)], axis=0).T
        for x in range(n_r):
            rows.append(jnp.sum((csum_t[:, x:x + 1] <= slot).astype(F32), axis=0, keepdims=True))
        return jnp.concatenate(rows, axis=0)
    blk_id = lax.broadcasted_iota(jnp.int32, (LANE, 1), 0).astype(F32)
    never = jnp.full((LANE - nb, LANE), float(t), F32)
    for x in range(n_r):
        cx = jnp.concatenate([csum[x:x + 1, j * LANE:(j + 1) * LANE] for j in range(nb)] + [never], axis=0)
        blk = jnp.sum((cx[:, LANE - 1:LANE] <= slot).astype(F32), axis=0, keepdims=True)
        pick = (blk_id == blk).astype(BF16)
        g = sum(jnp.dot(p, pick, preferred_element_type=F32) for p in _split_bf16(cx.T, 2))
        within = jnp.sum((g <= slot).astype(F32), axis=0, keepdims=True)
        rows.append(blk * LANE + within)
    return jnp.concatenate(rows, axis=0)


def _route_kernel(lg_ref, idx_ref, aff_ref, *, cap):
    l = lg_ref[0]
    n_e, t = l.shape
    e = jnp.exp(l - jnp.max(l, axis=0, keepdims=True))
    aff = e / jnp.sum(e, axis=0, keepdims=True)
    aff_ref[0] = aff
    keys = pltpu.bitcast(aff, jnp.int32)

    def search(it, lo):
        cand = lo | (jnp.int32(1) << (30 - it))
        cnt = jnp.sum((keys >= cand).astype(F32), axis=1, keepdims=True)
        return jnp.where(cnt >= cap, cand, lo)

    thr = lax.fori_loop(0, 31, search, jnp.zeros((n_e, 1), jnp.int32))
    gt = keys > thr
    eq = keys == thr
    need = cap - jnp.sum(gt.astype(F32), axis=1, keepdims=True)
    ri = lax.broadcasted_iota(jnp.int32, (LANE, LANE), 0)
    ci = lax.broadcasted_iota(jnp.int32, (LANE, LANE), 1)
    upper = (ri <= ci).astype(BF16)
    eq_rank = _prefix_lanes(eq.astype(F32), upper)
    sel = gt | (eq & (eq_rank <= need))
    csum = _prefix_lanes(sel.astype(F32), upper)
    idx_ref[0] = _slot_tokens(csum, cap).astype(jnp.int32)


def _route(logits_t, row0, t, cap):
    bsz, n_e, _ = logits_t.shape
    return pl.pallas_call(
        functools.partial(_route_kernel, cap=cap),
        grid=(bsz,),
        in_specs=[pl.BlockSpec((1, n_e, t), lambda b: (b, 0, row0 // t))],
        out_specs=[pl.BlockSpec((1, n_e, cap), lambda b: (b, 0, 0)),
                   pl.BlockSpec((1, n_e, t), lambda b: (b, 0, 0))],
        out_shape=[jax.ShapeDtypeStruct((bsz, n_e, cap), jnp.int32),
                   jax.ShapeDtypeStruct((bsz, n_e, t), F32)],
        compiler_params=_cparams(("parallel",), VMEM_LIMIT),
        name="expert_choice_route",
    )(logits_t)


def _smem_row(n):
    return pl.BlockSpec((1, 1, 1, n), lambda b, e: (b, e, 0, 0), memory_space=pltpu.SMEM)


def _gather_kernel(idx_ref, u_ref, o_ref, *, cap):
    def body(i, carry):
        t = idx_ref[0, 0, 0, i]
        o_ref[0, pl.ds(i, 1), :] = u_ref[0, pl.ds(t, 1), :]
        return carry

    lax.fori_loop(0, cap, body, 0, unroll=8)


def _gather(u_packed, idx, row0, t, cap):
    bsz, _, half = u_packed.shape
    n_e = idx.shape[1]
    return pl.pallas_call(
        functools.partial(_gather_kernel, cap=cap),
        grid=(bsz, n_e),
        in_specs=[_smem_row(cap), pl.BlockSpec((1, t, half), lambda b, e: (b, row0 // t, 0))],
        out_specs=pl.BlockSpec((1, cap, half), lambda b, e: (e, b, 0)),
        out_shape=jax.ShapeDtypeStruct((n_e, bsz * cap, half), jnp.uint32),
        compiler_params=_cparams(("parallel", "arbitrary"), VMEM_LIMIT),
        name="expert_gather",
    )(idx[:, :, None, :], u_packed)


FF_TILE = 256


def _ffn_kernel(x_ref, wg_ref, wu_ref, wd_ref, o_ref, xs_ref):
    f = pl.program_id(2)

    @pl.when(f == 0)
    def _():
        w = x_ref[0]
        half = w.shape[1]
        xs_ref[:, :half] = pltpu.bitcast(w << 16, F32).astype(BF16)
        xs_ref[:, half:] = pltpu.bitcast(w & jnp.uint32(0xFFFF0000), F32).astype(BF16)
        o_ref[...] = jnp.zeros_like(o_ref)

    x = xs_ref[...]
    g = jnp.dot(x, wg_ref[0, 0].astype(BF16), preferred_element_type=F32)
    u = jnp.dot(x, wu_ref[0, 0].astype(BF16), preferred_element_type=F32)
    o_ref[0] += jnp.dot((_silu(g) * u).astype(BF16), wd_ref[0, 0].astype(BF16), preferred_element_type=F32)


def _expert_ffn(xe, w_gate, w_up, w_down, layer):
    n_e, m, half = xe.shape
    d = 2 * half
    ff = w_gate.shape[-1]
    tm = min(m, 2048)
    return pl.pallas_call(
        _ffn_kernel,
        grid=(n_e, m // tm, ff // FF_TILE),
        in_specs=[pl.BlockSpec((1, tm, half), lambda e, i, f: (e, i, 0)),
                  pl.BlockSpec((1, 1, d, FF_TILE), lambda e, i, f: (layer, e, 0, f)),
                  pl.BlockSpec((1, 1, d, FF_TILE), lambda e, i, f: (layer, e, 0, f)),
                  pl.BlockSpec((1, 1, FF_TILE, d), lambda e, i, f: (layer, e, f, 0))],
        out_specs=pl.BlockSpec((1, tm, d), lambda e, i, f: (e, i, 0)),
        out_shape=jax.ShapeDtypeStruct((n_e, m, d), F32),
        scratch_shapes=[pltpu.VMEM((tm, d), BF16)],
        compiler_params=_cparams(("parallel", "parallel", "arbitrary"), VMEM_LIMIT),
        name="expert_ffn",
    )(xe, w_gate, w_up, w_down)


COMB_ROWS = 8


def _combine_kernel(idx_ref, aff_ref, y_ref, *rest, cap):
    o_ref = rest[-1]

    @pl.when(pl.program_id(1) == 0)
    def _():
        o_ref[...] = jnp.zeros_like(o_ref)

    def body(c, carry):
        base = pl.multiple_of(c * COMB_ROWS, COMB_ROWS)
        toks = [idx_ref[0, 0, 0, base + r] for r in range(COMB_ROWS)]
        rows = [o_ref[0, pl.ds(t, 1), :] + aff_ref[0, 0, 0, t] * y_ref[0, pl.ds(base + r, 1), :]
                for r, t in enumerate(toks)]
        for t, row in zip(toks, rows):
            o_ref[0, pl.ds(t, 1), :] = row
        return carry

    lax.fori_loop(0, cap // COMB_ROWS, body, 0)


def _combine(y, idx, aff, row0, t, cap, s_total, out_prev=None):
    n_e, m, d = y.shape
    bsz = idx.shape[0]
    in_specs = [_smem_row(cap), _smem_row(t), pl.BlockSpec((1, cap, d), lambda b, e: (e, b, 0))]
    args = [idx[:, :, None, :], aff[:, :, None, :], y]
    aliases = {}
    if out_prev is not None:
        in_specs.append(pl.BlockSpec(memory_space=pl.ANY))
        args.append(out_prev)
        aliases = {3: 0}
    return pl.pallas_call(
        functools.partial(_combine_kernel, cap=cap),
        grid=(bsz, n_e),
        in_specs=in_specs,
        out_specs=pl.BlockSpec((1, t, d), lambda b, e: (b, row0 // t, 0)),
        out_shape=jax.ShapeDtypeStruct((bsz, s_total, d), F32),
        input_output_aliases=aliases,
        compiler_params=_cparams(("parallel", "arbitrary"), VMEM_LIMIT),
        name="expert_combine",
    )(*args)


def _moe(u_packed, logits_t, w_gate, w_up, w_down, layer, with_ctx):
    bsz, rows, _ = u_packed.shape
    out = None
    sets = [(0, SEQ)] + ([(SEQ, CTX_LEN)] if with_ctx else [])
    for row0, t in sets:
        cap = EC_CAPACITY * t // N_EXPERTS
        idx, aff = _route(logits_t, row0, t, cap)
        xe = _gather(u_packed, idx, row0, t, cap)
        y = _expert_ffn(xe, w_gate, w_up, w_down, layer)
        out = _combine(y, idx, aff, row0, t, cap, rows, out)
    return out


def _final_kernel(h_ref, f_ref, mod_ref, g_ref, o_ref):
    o_ref[0] = h_ref[0] + mod_ref[0, 0][5:6] * _rms(f_ref[0], g_ref[...])


def _final_residual(h, f, mod_l, g_post):
    bsz, n, d = h.shape
    tm = ROW_TILE
    row = lambda b, i: (b, i, 0)
    return pl.pallas_call(
        _final_kernel,
        grid=(bsz, n // tm),
        in_specs=[pl.BlockSpec((1, tm, d), row), pl.BlockSpec((1, tm, d), row),
                  pl.BlockSpec((1, 1, 6, d), lambda b, i: (b, 0, 0, 0)), pl.BlockSpec((1, d), lambda b, i: (0, 0))],
        out_specs=pl.BlockSpec((1, tm, d), row),
        out_shape=jax.ShapeDtypeStruct((bsz, n, d), F32),
        compiler_params=_cparams(("parallel", "arbitrary")),
        name="ffn_residual",
    )(h, f, mod_l.reshape(mod_l.shape[0], 1, 6, d), g_post.reshape(1, d))


def _rope_tables():
    n = jnp.arange(SEQ, dtype=jnp.int32)
    row = (n // GRID_W).astype(F32)
    col = (n % GRID_W).astype(F32)
    quarter = HEAD_DIM // 4
    inv = ROPE_THETA ** (-jnp.arange(quarter, dtype=F32) / quarter)
    lane = jnp.arange(LANE)
    pos = jnp.where((lane % HEAD_DIM < HEAD_DIM // 2)[None, :], row[:, None], col[:, None])
    ang = pos * inv[lane % quarter][None, :]
    first = (lane % (2 * quarter) < quarter)[None, :]
    cos = jnp.cos(ang)
    sin = jnp.sin(ang)
    sa = jnp.where(first, -sin, 0.0)
    sb = jnp.where(first, 0.0, sin)
    padc = lambda t, v: jnp.concatenate([t, jnp.full((CTX_LEN, LANE), v, F32)], axis=0)
    return padc(cos, 1.0), padc(sa, 0.0), padc(sb, 0.0)


def kernel(x, c, ctx, c_ctx, w_ada, b_ada, g_mix_pre, g_mix_post, g_ffn_pre, g_ffn_post, w_in, conv_w, conv_b,
           dt_bias, a_log, d_skip, g_ssd, attn_sink, w_out, w_router, w_gate, w_up, w_down):
    depth = w_ada.shape[0]
    bsz = x.shape[0]
    d = D_MODEL
    cc = jnp.concatenate([c, c_ctx[None, :], jnp.zeros((7, d), F32)], axis=0)
    mod = _modulation(cc, w_ada, b_ada).reshape(depth, bsz + 8, 6, d)
    rope_tabs = _rope_tables()
    chan_tab = _channel_tables()
    cn_lat, sn_lat = _dft_tables(SEQ)
    cn_ctx, sn_ctx = _dft_tables(CTX_LEN)
    q0 = F_WIDTH

    h = (x, ctx)
    f_prev = None
    for l in range(depth):
        last = l == depth - 1
        w_q = _to_slab_order(w_in[l][:, q0:q0 + ATT_WIDTH], 1) * HEAD_DIM ** -0.5
        w_in_p = jnp.concatenate([w_in[l][:, :q0], w_q, w_in[l][:, q0 + ATT_WIDTH:],
                                  jnp.zeros((d, IN_PAD - IN_WIDTH), F32)], axis=1).astype(BF16)
        w_out_p = jnp.concatenate([w_out[l][:q0], _to_slab_order(w_out[l][q0:q0 + ATT_WIDTH], 0),
                                   w_out[l][q0 + ATT_WIDTH:]], axis=0).astype(BF16)
        outs = _in_proj(h, mod[l], mod[l - 1] if l else None, g_mix_pre[l], w_in_p, rope_tabs, chan_tab,
                        not last, f_prev, g_ffn_post[l - 1] if l else None)
        if f_prev is not None:
            h, outs = outs[0], outs[1:]
        lat_ab, ctx_ab, q, k, v, z, xbc, dt_raw = outs
        fmix = _fourier(lat_ab, cn_lat, sn_lat, 0, out_rows=SEQ if last else S_ALL)
        if not last:
            fmix = _fourier(ctx_ab, cn_ctx, sn_ctx, SEQ, fmix)
        att = _attention(q, k, v, attn_sink[l], not last)
        y_f, y_b = _ssd_mixer(xbc, dt_raw, conv_w[l], conv_b[l], dt_bias[l], a_log[l], d_skip[l])
        n_rows = SEQ if last else S_ALL
        h, u_packed, logits_t = _out_proj(fmix, att, y_f, y_b, z, h, mod[l], g_ssd[l],
                                          g_mix_post[l], g_ffn_pre[l], w_out_p, w_router[l], n_rows)
        f_prev = _moe(u_packed, logits_t, w_gate, w_up, w_down, l, not last)
    return _final_residual(h, f_prev, mod[depth - 1], g_ffn_post[depth - 1])
```

```python
import functools
import math

import jax
import jax.numpy as jnp
from jax import lax
from jax.experimental import pallas as pl
from jax.experimental.pallas import tpu as pltpu

F32 = jnp.float32
BF16 = jnp.bfloat16

D_MODEL = 1024
SEQ = 4096
CTX_LEN = 256
S_ALL = SEQ + CTX_LEN
GRID_W = 64
HEAD_DIM = 64
EPS = 1e-6
F_WIDTH = 256
ATT_HEADS = 6
ATT_KV_HEADS = 2
ATT_WIDTH = 384
KV_WIDTH = 128
BLOCK = 128
ROPE_THETA = 10000.0
SSD_HEADS = 6
SSD_WIDTH = 384
SSD_GROUPS = 2
SSD_STATE = 128
CHUNK = 128
CONV_K = 5
CONV_DIM = 896
IN_WIDTH = 2188
IN_PAD = 2304
N_EXPERTS = 16
EXPERT_FF = 2816
EC_CAPACITY = 2

LANE = 128
ROW_TILE = 512
VMEM_LIMIT = 56 * 1024 * 1024


def _cparams(sem, vmem=None):
    return pltpu.CompilerParams(dimension_semantics=sem, vmem_limit_bytes=vmem)


def _split_bf16(x, n):
    parts, r = [], x
    for _ in range(n):
        p = r.astype(BF16)
        parts.append(p)
        r = r - p.astype(F32)
    return parts


def _rms(x, g):
    return x * lax.rsqrt(jnp.mean(x * x, axis=-1, keepdims=True) + EPS) * g


def _silu(x):
    return x * jax.nn.sigmoid(x)


def _mod_kernel(cc_ref, w_ref, b_ref, o_ref):
    s = _silu(cc_ref[...])
    o_ref[0] = jnp.dot(s.astype(BF16), w_ref[0].astype(BF16), preferred_element_type=F32) + b_ref[0]


def _modulation(cc, w_ada, b_ada):
    n_l, d, d6 = w_ada.shape
    r = cc.shape[0]
    tn = 1024
    return pl.pallas_call(
        _mod_kernel,
        grid=(n_l, d6 // tn),
        in_specs=[pl.BlockSpec((r, d), lambda l, j: (0, 0)),
                  pl.BlockSpec((1, d, tn), lambda l, j: (l, 0, j)),
                  pl.BlockSpec((1, 1, tn), lambda l, j: (l, 0, j))],
        out_specs=pl.BlockSpec((1, r, tn), lambda l, j: (l, 0, j)),
        out_shape=jax.ShapeDtypeStruct((n_l, r, d6), F32),
        compiler_params=_cparams(("arbitrary", "arbitrary")),
        name="adaln_mod",
    )(cc, w_ada, b_ada.reshape(n_l, 1, d6))


def _stream_specs(h, tm):
    n_lat = SEQ // tm
    if isinstance(h, tuple):
        lat, ctx = h
        d = lat.shape[-1]
        return ([pl.BlockSpec((1, tm, d), lambda b, i: (b, jnp.minimum(i, n_lat - 1), 0)),
                 pl.BlockSpec((1, CTX_LEN, d), lambda b, i: (b, 0, 0))], [lat, ctx])
    return [pl.BlockSpec((1, tm, h.shape[-1]), lambda b, i: (b, i, 0))], [h]


def _stream_tile(refs, tm):
    if len(refs) == 2:
        ctx = jnp.concatenate([refs[1][0]] * (tm // CTX_LEN), axis=0)
        return jnp.where(pl.program_id(1) < SEQ // tm, refs[0][0], ctx)
    return refs[0][0]


def _in_kernel(*refs, residual, n_h, ctx_fourier):
    h_refs, refs = refs[:n_h], refs[n_h:]
    if residual:
        f_ref, gpost_ref, refs = refs[0], refs[1], refs[2:]
    mod_ref, g_ref, w_ref, cos_ref, sa_ref, sb_ref, ch_ref = refs[:7]
    outs = list(refs[7:])
    if residual:
        hout_ref = outs.pop(0)
    if ctx_fourier:
        al_ref, bl_ref, ac_ref, bc_ref, q_ref, k_ref, v_ref, z_ref, xbc_ref, dt_ref = outs
    else:
        al_ref, bl_ref, q_ref, k_ref, v_ref, z_ref, xbc_ref, dt_ref = outs
    x = _stream_tile(h_refs, ROW_TILE)
    mod = mod_ref[0, 0]
    if residual:
        x = x + mod[5:6] * _rms(f_ref[0], gpost_ref[...])
        hout_ref[0] = x
    u = _rms(x, g_ref[...]) * (1.0 + mod[1:2]) + mod[0:1]
    p = jnp.dot(u.astype(BF16), w_ref[...], preferred_element_type=F32)
    cos, sa, sb = cos_ref[...], sa_ref[...], sb_ref[...]

    def rope(t):
        return t * cos + pltpu.roll(t, LANE - 16, 1) * sa + pltpu.roll(t, 16, 1) * sb

    ab = jnp.dot(p[:, 0:F_WIDTH].astype(BF16), ch_ref[...], preferred_element_type=F32).astype(BF16)
    is_lat = pl.program_id(1) < SEQ // ROW_TILE

    @pl.when(is_lat)
    def _():
        al_ref[...] = ab[:, :F_WIDTH]
        bl_ref[...] = ab[:, F_WIDTH:]

    if ctx_fourier:
        @pl.when(jnp.logical_not(is_lat))
        def _():
            ac_ref[...] = ab[:CTX_LEN, :F_WIDTH]
            bc_ref[...] = ab[:CTX_LEN, F_WIDTH:]

    q_ref[0] = jnp.concatenate([rope(p[:, 256 + LANE * s:256 + LANE * (s + 1)]) for s in range(3)],
                               axis=1).astype(BF16)
    k_ref[0] = rope(p[:, 640:768]).astype(BF16)
    v_ref[0] = p[:, 768:896].astype(BF16)
    z_ref[0] = p[:, 896:1280].astype(BF16)
    xbc_ref[0] = p[:, 1280:2176].astype(BF16)
    dt_ref[0] = p[:, 2176:2304]


def _in_proj(h, mod_l, mod_prev, g_pre, w_in_p, rope_tabs, chan_tab, ctx_fourier, f_prev=None, g_post_prev=None):
    bsz = mod_l.shape[0] - 8
    s, d = S_ALL, D_MODEL
    tm = ROW_TILE
    n_lat = SEQ // tm
    residual = f_prev is not None
    row = lambda b, i: (b, i, 0)
    mod_map = lambda b, i: (jnp.where(i < n_lat, b, bsz), 0, 0, 0)
    const2 = lambda b, i: (0, 0)
    tab = pl.BlockSpec((tm, LANE), lambda b, i: (i, 0))
    in_specs, args = _stream_specs(h, tm)
    n_h = len(args)
    if residual:
        in_specs += [pl.BlockSpec((1, tm, d), row), pl.BlockSpec((1, d), const2)]
        args += [f_prev, g_post_prev.reshape(1, d)]
    mod_used = mod_l if not residual else jnp.concatenate([mod_l[:, :5], mod_prev[:, 5:6]], axis=1)
    in_specs += [pl.BlockSpec((1, 1, 6, d), mod_map), pl.BlockSpec((1, d), const2),
                 pl.BlockSpec((d, IN_PAD), const2), tab, tab, tab, pl.BlockSpec(chan_tab.shape, const2)]
    args += [mod_used.reshape(mod_used.shape[0], 1, 6, d), g_pre.reshape(1, d), w_in_p, *rope_tabs, chan_tab]
    widths = [(384, BF16), (128, BF16), (128, BF16), (384, BF16), (896, BF16), (128, F32)]
    out_specs = [pl.BlockSpec((1, tm, w), row) for w, _ in widths]
    out_shape = [jax.ShapeDtypeStruct((bsz, s, w), dt) for w, dt in widths]
    lat_spec = pl.BlockSpec((tm, F_WIDTH), lambda b, i: (jnp.minimum(i, n_lat - 1), b))
    four_specs = [lat_spec, lat_spec]
    four_shape = [jax.ShapeDtypeStruct((SEQ, bsz * F_WIDTH), BF16)] * 2
    if ctx_fourier:
        four_specs += [pl.BlockSpec((CTX_LEN, F_WIDTH), lambda b, i: (0, b))] * 2
        four_shape += [jax.ShapeDtypeStruct((CTX_LEN, bsz * F_WIDTH), BF16)] * 2
    out_specs = four_specs + out_specs
    out_shape = four_shape + out_shape
    if residual:
        out_specs = [pl.BlockSpec((1, tm, d), row)] + out_specs
        out_shape = [jax.ShapeDtypeStruct((bsz, s, d), F32)] + out_shape
    outs = pl.pallas_call(
        functools.partial(_in_kernel, residual=residual, n_h=n_h, ctx_fourier=ctx_fourier),
        grid=(bsz, pl.cdiv(s, tm)),
        in_specs=in_specs, out_specs=out_specs, out_shape=out_shape,
        compiler_params=_cparams(("parallel", "arbitrary"), VMEM_LIMIT),
        name="prenorm_inproj",
    )(*args)
    outs = list(outs)
    head = [outs.pop(0)] if residual else []
    lat_ab = (outs.pop(0), outs.pop(0))
    ctx_ab = (outs.pop(0), outs.pop(0)) if ctx_fourier else None
    return head + [lat_ab, ctx_ab] + outs


def _f2_kernel(c_ref, s_ref, a_ref, b_ref, *rest):
    o_ref = rest[-1]
    y = (jnp.dot(c_ref[...], a_ref[...], preferred_element_type=F32)
         - jnp.dot(s_ref[...], b_ref[...], preferred_element_type=F32))
    o_ref[0] = y.astype(BF16)


def _dft_tables(n):
    sc = n ** -0.5

    def cs(rows, cols, period):
        m = (rows[:, None] * cols[None, :]) % period
        ang = m.astype(F32) * (2.0 * math.pi / period)
        return jnp.cos(ang), jnp.sin(ang)

    k = jnp.arange(n, dtype=jnp.int32)
    r = 64
    if n <= r * r // 16:
        c, s = cs(k, k, n)
        return (c * sc).astype(BF16), (s * sc).astype(BF16)
    ca, sa = cs(k, jnp.arange(n // r, dtype=jnp.int32), n // r)
    cb, sb = cs(k, jnp.arange(r, dtype=jnp.int32), n)
    c = ca[:, :, None] * cb[:, None, :] - sa[:, :, None] * sb[:, None, :]
    s = sa[:, :, None] * cb[:, None, :] + ca[:, :, None] * sb[:, None, :]
    return (c * sc).reshape(n, n).astype(BF16), (s * sc).reshape(n, n).astype(BF16)


def _channel_tables():
    c64, s64 = _dft_tables(HEAD_DIM)
    eye = jnp.eye(F_WIDTH // HEAD_DIM, dtype=BF16)
    return jnp.concatenate([jnp.kron(eye, c64), jnp.kron(eye, s64)], axis=1)


F2_EXT = 16


def _f2_sym_kernel(cm_ref, ce_ref, sm_ref, se_ref, a_ref, b_ref, jr_ref, o_ref, p_ref, q_ref, *, tm, nt):
    i = pl.program_id(1)
    half = nt // 2

    @pl.when(i == 0)
    def _():
        tail = half * tm + F2_EXT
        p_ref[tail:, :] = jnp.zeros((LANE - F2_EXT, p_ref.shape[1]), F32)
        q_ref[tail:, :] = jnp.zeros((LANE - F2_EXT, q_ref.shape[1]), F32)

    def emit(y):
        for j in range(o_ref.shape[0]):
            o_ref[j] = y[:, j * F_WIDTH:(j + 1) * F_WIDTH].astype(BF16)

    @pl.when(i < half)
    def _():
        a, b = a_ref[...], b_ref[...]
        row = pl.multiple_of(i * tm, tm)
        p = jnp.dot(cm_ref[...], a, preferred_element_type=F32)
        q = jnp.dot(sm_ref[...], b, preferred_element_type=F32)
        p_ref[pl.ds(row, tm), :] = p
        q_ref[pl.ds(row, tm), :] = q
        p_ref[pl.ds(row + tm, F2_EXT), :] = jnp.dot(ce_ref[...], a, preferred_element_type=F32)
        q_ref[pl.ds(row + tm, F2_EXT), :] = jnp.dot(se_ref[...], b, preferred_element_type=F32)
        emit(p - q)

    @pl.when(i >= half)
    def _():
        row = pl.multiple_of((nt - 1 - i) * tm, tm)
        s = (p_ref[pl.ds(row, tm + LANE), :] + q_ref[pl.ds(row, tm + LANE), :]).astype(BF16)
        emit(jnp.dot(jr_ref[...], s, preferred_element_type=F32))


def _fourier_positions_sym(cn, sn, a_all, b_all, bsz, n, tm, blk0, out_rows):
    nt = n // tm
    half = nt // 2
    per = tm // F2_EXT
    main = lambda b, i: (jnp.minimum(i, half - 1), 0)
    ext = lambda b, i: ((jnp.minimum(i, half - 1) + 1) * per, 0)
    r = jnp.arange(tm)[:, None]
    flip = (jnp.arange(tm + LANE)[None, :] == tm - r).astype(BF16)
    grp = 2 if bsz % 2 == 0 else 1
    cols = grp * F_WIDTH
    return pl.pallas_call(
        functools.partial(_f2_sym_kernel, tm=tm, nt=nt),
        grid=(bsz // grp, nt),
        in_specs=[pl.BlockSpec((tm, n), main), pl.BlockSpec((F2_EXT, n), ext),
                  pl.BlockSpec((tm, n), main), pl.BlockSpec((F2_EXT, n), ext),
                  pl.BlockSpec((n, cols), lambda b, i: (0, b)),
                  pl.BlockSpec((n, cols), lambda b, i: (0, b)),
                  pl.BlockSpec((tm, tm + LANE), lambda b, i: (0, 0))],
        out_specs=pl.BlockSpec((grp, tm, F_WIDTH), lambda b, i: (b, blk0 + i, 0)),
        out_shape=jax.ShapeDtypeStruct((bsz, out_rows, F_WIDTH), BF16),
        scratch_shapes=[pltpu.VMEM((n // 2 + LANE, cols), F32)] * 2,
        compiler_params=_cparams(("parallel", "arbitrary"), VMEM_LIMIT),
        name="fourier_positions",
    )(cn, cn, sn, sn, a_all, b_all, flip)


def _fourier(ab, cn, sn, row0, out_prev=None, out_rows=S_ALL):
    a_all, b_all = ab
    n = a_all.shape[0]
    bsz = a_all.shape[1] // F_WIDTH
    tm = min(n, 512)
    nt = n // tm
    blk0 = row0 // tm
    if nt >= 4 and out_prev is None:
        return _fourier_positions_sym(cn, sn, a_all, b_all, bsz, n, tm, blk0, out_rows)
    in_specs = [pl.BlockSpec((tm, n), lambda i, b: (i, 0)),
                pl.BlockSpec((tm, n), lambda i, b: (i, 0)),
                pl.BlockSpec((n, F_WIDTH), lambda i, b: (0, b)),
                pl.BlockSpec((n, F_WIDTH), lambda i, b: (0, b))]
    args = [cn, sn, a_all, b_all]
    aliases = {}
    if out_prev is not None:
        in_specs.append(pl.BlockSpec(memory_space=pl.ANY))
        args.append(out_prev)
        aliases = {4: 0}
    return pl.pallas_call(
        _f2_kernel,
        grid=(nt, bsz),
        in_specs=in_specs,
        out_specs=pl.BlockSpec((1, tm, F_WIDTH), lambda i, b: (b, blk0 + i, 0)),
        out_shape=jax.ShapeDtypeStruct((bsz, out_rows, F_WIDTH), BF16),
        input_output_aliases=aliases,
        compiler_params=_cparams(("parallel", "arbitrary"), VMEM_LIMIT),
        name="fourier_positions",
    )(*args)


def _to_slab_order(w, axis):
    r = ATT_HEADS // ATT_KV_HEADS
    shape = w.shape
    w = w.reshape(shape[:axis] + (ATT_KV_HEADS, r, HEAD_DIM) + shape[axis + 1:])
    return jnp.swapaxes(w, axis, axis + 1).reshape(shape)


def _attend(q, k, v, masks, sink_ref):
    n = q.shape[0]
    r = ATT_HEADS // ATT_KV_HEADS
    lo = lax.broadcasted_iota(jnp.int32, (1, LANE), 1) < HEAD_DIM
    rid = lax.broadcasted_iota(jnp.int32, (r * n, 1), 0)
    outs = []
    for g in range(ATT_KV_HEADS):
        keep = lo if g == 0 else jnp.logical_not(lo)
        qg = jnp.concatenate([jnp.where(keep, q[:, LANE * s:LANE * (s + 1)], jnp.zeros((), BF16))
                              for s in range(r)], axis=0)
        snk = jnp.full((r * n, 1), sink_ref[r * g], F32)
        for j in range(1, r):
            snk = jnp.where(rid >= j * n, sink_ref[r * g + j], snk)
        sc = lax.dot_general(qg, k, (((1,), (1,)), ((), ())), preferred_element_type=F32)
        if masks:
            nblk = sc.shape[1] // LANE
            sc = jnp.concatenate(
                [jnp.where(masks[b], sc[:, b * LANE:(b + 1) * LANE], -1e30) if b in masks
                 else sc[:, b * LANE:(b + 1) * LANE] for b in range(nblk)], axis=1)
        mx = jnp.maximum(jnp.max(sc, axis=-1, keepdims=True), snk)
        p = jnp.exp(sc - mx)
        den = jnp.sum(p, axis=-1, keepdims=True) + jnp.exp(snk - mx)
        outs.append(jnp.dot(p.astype(BF16), v, preferred_element_type=F32) / den)
    return jnp.concatenate([jnp.where(lo, outs[0][s * n:(s + 1) * n], outs[1][s * n:(s + 1) * n])
                            for s in range(r)], axis=1)


def _attn_lat_kernel(sink_ref, q_ref, kp_ref, kc_ref, kn_ref, kx_ref, vp_ref, vc_ref, vn_ref, vx_ref, o_ref):
    nb = pl.num_programs(1)
    n = pl.program_id(1)
    rows = (ATT_HEADS // ATT_KV_HEADS) * BLOCK
    i = lax.broadcasted_iota(jnp.int32, (rows, BLOCK), 0) & (BLOCK - 1)
    j = lax.broadcasted_iota(jnp.int32, (rows, BLOCK), 1)
    m_prev = (j >= i) & (n > 0)
    m_next = (j <= i) & (n < nb - 1)
    k = jnp.concatenate([kp_ref[0], kc_ref[0], kn_ref[0], kx_ref[0]], axis=0)
    v = jnp.concatenate([vp_ref[0], vc_ref[0], vn_ref[0], vx_ref[0]], axis=0)
    o_ref[0] = _attend(q_ref[0], k, v, {0: m_prev, 2: m_next}, sink_ref).astype(BF16)


def _attn_ctx_kernel(sink_ref, q_ref, kx_ref, vx_ref, prev_ref, o_ref):
    o_ref[0] = _attend(q_ref[0], kx_ref[0], vx_ref[0], {}, sink_ref).astype(BF16)


def _attention(q, k, v, sink, with_ctx):
    bsz = q.shape[0]
    s = S_ALL if with_ctx else SEQ
    nb = SEQ // BLOCK
    ctx_blk = SEQ // CTX_LEN
    smem = pl.BlockSpec(memory_space=pltpu.SMEM)
    kv = lambda f: pl.BlockSpec((1, BLOCK, KV_WIDTH), f)
    prev = lambda b, n: (b, jnp.maximum(n - 1, 0), 0)
    cur = lambda b, n: (b, n, 0)
    nxt = lambda b, n: (b, jnp.minimum(n + 1, nb - 1), 0)
    ctx = pl.BlockSpec((1, CTX_LEN, KV_WIDTH), lambda b, n: (b, ctx_blk, 0))
    sink8 = jnp.pad(sink, (0, 8 - ATT_HEADS))
    out = pl.pallas_call(
        _attn_lat_kernel,
        grid=(bsz, nb),
        in_specs=[smem, pl.BlockSpec((1, BLOCK, ATT_WIDTH), cur),
                  kv(prev), kv(cur), kv(nxt), ctx, kv(prev), kv(cur), kv(nxt), ctx],
        out_specs=pl.BlockSpec((1, BLOCK, ATT_WIDTH), cur),
        out_shape=jax.ShapeDtypeStruct((bsz, s, ATT_WIDTH), BF16),
        compiler_params=_cparams(("parallel", "arbitrary")),
        name="window_attention",
    )(sink8, q, k, k, k, k, v, v, v, v)
    if not with_ctx:
        return out
    cq = lambda b: (b, ctx_blk, 0)
    return pl.pallas_call(
        _attn_ctx_kernel,
        grid=(bsz,),
        in_specs=[smem, pl.BlockSpec((1, CTX_LEN, ATT_WIDTH), cq),
                  pl.BlockSpec((1, CTX_LEN, KV_WIDTH), cq), pl.BlockSpec((1, CTX_LEN, KV_WIDTH), cq),
                  pl.BlockSpec(memory_space=pl.ANY)],
        out_specs=pl.BlockSpec((1, CTX_LEN, ATT_WIDTH), cq),
        out_shape=jax.ShapeDtypeStruct((bsz, s, ATT_WIDTH), BF16),
        input_output_aliases={4: 0},
        compiler_params=_cparams(("parallel",)),
        name="context_attention",
    )(sink8, q, k, v, out)


HALO = 16
N_SHIFT = CONV_K - 1


def _shift_table():
    i = jnp.arange(CHUNK)[:, None]
    j = jnp.arange(CHUNK + 2 * HALO)[None, :]
    taps = [k for k in range(CONV_K) if k != CONV_K // 2]
    return jnp.concatenate([(j == i + k - CONV_K // 2 + HALO) for k in taps], axis=0).astype(BF16)


def _expand_table(lane0):
    src = jnp.arange(LANE)[:, None] - lane0
    return (src == (jnp.arange(SSD_WIDTH)[None, :] // HEAD_DIM)).astype(BF16)


def _conv_chunk(c, xp_ref, xc_ref, xn_ref, sh_ref, w_ref, b_ref):
    n_lat = SEQ // CHUNK
    last = S_ALL // CHUNK - 1
    has_prev = (c != 0) & (c != n_lat)
    has_next = (c != n_lat - 1) & (c != last)
    zero = jnp.zeros((), BF16)
    cur = xc_ref[0]
    xx = jnp.concatenate([jnp.where(has_prev, xp_ref[0], zero), cur, jnp.where(has_next, xn_ref[0], zero)], axis=0)
    sh = jnp.dot(sh_ref[...], xx, preferred_element_type=F32)
    mid = CONV_K // 2
    acc = b_ref[...] + w_ref[mid:mid + 1, :] * cur.astype(F32)
    for t in range(N_SHIFT):
        k = t if t < mid else t + 1
        acc = acc + w_ref[k:k + 1, :] * sh[t * CHUNK:(t + 1) * CHUNK]
    return _silu(acc)


def _ssd_chunk(xc, dt_raw, dtb, alog, ex_ref, dsk, state_ref, reverse):
    xs = xc[:, :SSD_WIDTH].astype(F32)
    gs = SSD_GROUPS * SSD_STATE
    bm = xc[:, SSD_WIDTH:SSD_WIDTH + gs]
    cm = xc[:, SSD_WIDTH + gs:SSD_WIDTH + 2 * gs]
    raw = dt_raw + dtb
    dt = jnp.maximum(raw, 0.0) + jnp.log(1.0 + jnp.exp(-jnp.abs(raw)))
    dta = dt * (-jnp.exp(alog))
    ri = lax.broadcasted_iota(jnp.int32, (CHUNK, CHUNK), 0)
    ci = lax.broadcasted_iota(jnp.int32, (CHUNK, CHUNK), 1)
    keep = (ri <= ci) if reverse else (ri >= ci)
    tri = keep.astype(BF16)
    acs = sum(jnp.dot(tri, p, preferred_element_type=F32) for p in _split_bf16(dta, 3))
    acs_t = acs.T
    parts = jnp.concatenate(_split_bf16(dt, 2) + _split_bf16(acs, 3), axis=0)
    wide = jnp.dot(parts, ex_ref[...], preferred_element_type=F32)
    dt_x = wide[:CHUNK] + wide[CHUNK:2 * CHUNK]
    acs_x = wide[2 * CHUNK:3 * CHUNK] + wide[3 * CHUNK:4 * CHUNK] + wide[4 * CHUNK:]
    edge = 0 if reverse else CHUNK - 1
    tot_x = acs_x[edge:edge + 1, :]
    rh = SSD_HEADS // SSD_GROUPS
    gw = rh * HEAD_DIM
    lane0 = SSD_HEADS if reverse else 0
    xd = xs * dt_x
    xdw = (xd * jnp.exp(tot_x - acs_x)).astype(BF16)
    xdb = xd.astype(BF16)
    eacs = jnp.exp(acs_x)
    state = state_ref[...]
    ys = []
    new_state = []
    for g in range(SSD_GROUPS):
        bg = bm[:, g * SSD_STATE:(g + 1) * SSD_STATE]
        cg = cm[:, g * SSD_STATE:(g + 1) * SSD_STATE]
        cb = lax.dot_general(cg, bg, (((1,), (1,)), ((), ())), preferred_element_type=F32)
        sg = state[:, g * gw:(g + 1) * gw]
        y_off = jnp.dot(cg, sg.astype(BF16), preferred_element_type=F32) * eacs[:, g * gw:(g + 1) * gw]
        for r in range(rh):
            h = g * rh + r
            ln = lane0 + h
            dec = jnp.where(keep, jnp.exp(acs[:, ln:ln + 1] - acs_t[ln:ln + 1, :]), 0.0)
            yd = jnp.dot((cb * dec).astype(BF16), xdb[:, h * HEAD_DIM:(h + 1) * HEAD_DIM],
                         preferred_element_type=F32)
            ys.append(yd + y_off[:, r * HEAD_DIM:(r + 1) * HEAD_DIM])
        bt = bg.astype(F32).T.astype(BF16)
        new_state.append(jnp.dot(bt, xdw[:, g * gw:(g + 1) * gw], preferred_element_type=F32))
    state_ref[...] = state * jnp.exp(tot_x) + jnp.concatenate(new_state, axis=1)
    y = jnp.concatenate(ys, axis=1)
    return y if dsk is None else y + dsk * xs


def _ssd_kernel(fp_ref, fc_ref, fn_ref, fdt_ref, bp_ref, bc_ref, bn_ref, bdt_ref,
                sh_ref, w_ref, b_ref, dtb_ref, alog_ref, exf_ref, exb_ref, dsk_ref,
                yf_ref, yb_ref, sf_ref, sb_ref, xcache_ref):
    step = pl.program_id(1)
    nc = pl.num_programs(1)
    n_lat = SEQ // CHUNK

    @pl.when(step == 0)
    def _():
        sf_ref[...] = jnp.zeros_like(sf_ref)
        sb_ref[...] = jnp.zeros_like(sb_ref)

    cf = (step + n_lat) % nc
    cbk = nc - 1 - step

    @pl.when((step != 1) & (step < n_lat // 2 + 2))
    def _():
        xcache_ref[cf] = _conv_chunk(cf, fp_ref, fc_ref, fn_ref, sh_ref, w_ref, b_ref).astype(BF16)
        xcache_ref[cbk] = _conv_chunk(cbk, bp_ref, bc_ref, bn_ref, sh_ref, w_ref, b_ref).astype(BF16)

    dtb, alog = dtb_ref[...], alog_ref[...]
    yf_ref[0] = _ssd_chunk(xcache_ref[cf], fdt_ref[0], dtb, alog, exf_ref, dsk_ref[...], sf_ref,
                           False).astype(BF16)
    yb_ref[0] = _ssd_chunk(xcache_ref[cbk], bdt_ref[0], dtb, alog, exb_ref, None, sb_ref, True).astype(BF16)


def _ssd_mixer(xbc, dt_raw, conv_w, conv_b, dt_bias, a_log, d_skip):
    bsz, s, cdim = xbc.shape
    nc = s // CHUNK
    n_lat = SEQ // CHUNK
    per = CHUNK // HALO
    nh = s // HALO
    assert nc == n_lat + 2 and n_lat % 2 == 0
    fwd = lambda t: (t + n_lat) % nc
    bwd = lambda t: nc - 1 - t

    def chunk_specs(order):
        return [pl.BlockSpec((1, HALO, cdim), lambda b, t: (b, jnp.maximum(order(t) * per - 1, 0), 0)),
                pl.BlockSpec((1, CHUNK, cdim), lambda b, t: (b, order(t), 0)),
                pl.BlockSpec((1, HALO, cdim), lambda b, t: (b, jnp.minimum(order(t) * per + per, nh - 1), 0)),
                pl.BlockSpec((1, CHUNK, LANE), lambda b, t: (b, order(t), 0))]

    c2 = lambda b, t: (0, 0)
    full = lambda a: pl.BlockSpec(a.shape, c2)
    pad = lambda p: jnp.pad(p.reshape(1, -1), ((0, 0), (0, LANE - 2 * SSD_HEADS)))
    consts = [_shift_table(), jnp.pad(conv_w, ((0, 8 - CONV_K), (0, 0))), conv_b.reshape(1, cdim),
              pad(dt_bias), pad(a_log), _expand_table(0), _expand_table(SSD_HEADS),
              jnp.repeat(d_skip, HEAD_DIM).reshape(1, SSD_WIDTH)]
    return pl.pallas_call(
        _ssd_kernel,
        grid=(bsz, nc),
        in_specs=chunk_specs(fwd) + chunk_specs(bwd) + [full(a) for a in consts],
        out_specs=[pl.BlockSpec((1, CHUNK, SSD_WIDTH), lambda b, t: (b, fwd(t), 0)),
                   pl.BlockSpec((1, CHUNK, SSD_WIDTH), lambda b, t: (b, bwd(t), 0))],
        out_shape=[jax.ShapeDtypeStruct((bsz, s, SSD_WIDTH), BF16)] * 2,
        scratch_shapes=[pltpu.VMEM((SSD_STATE, SSD_WIDTH), F32)] * 2 + [pltpu.VMEM((nc, CHUNK, cdim), BF16)],
        compiler_params=_cparams(("parallel", "arbitrary"), VMEM_LIMIT),
        name="conv_ssd_scan",
    )(xbc, xbc, xbc, dt_raw, xbc, xbc, xbc, dt_raw, *consts)


def _out_kernel(*refs, n_h):
    (fo_ref, at_ref, yf_ref, yb_ref, z_ref), refs = refs[:5], refs[5:]
    h_refs, refs = refs[:n_h], refs[n_h:]
    mod_ref, gssd_ref, gpost_ref, gpre_ref, wo_ref, wr_ref, hout_ref, u_ref, lg_ref = refs
    y = (yf_ref[0].astype(F32) + yb_ref[0].astype(F32)) * _silu(z_ref[0].astype(F32))
    sm = _rms(y, gssd_ref[...]).astype(BF16)
    m = (jnp.dot(fo_ref[0], wo_ref[0:256, :], preferred_element_type=F32)
         + jnp.dot(at_ref[0], wo_ref[256:640, :], preferred_element_type=F32)
         + jnp.dot(sm, wo_ref[640:1024, :], preferred_element_type=F32))
    mod = mod_ref[0, 0]
    hn = _stream_tile(h_refs, ROW_TILE) + mod[2:3] * _rms(m, gpost_ref[...])
    hout_ref[0] = hn
    u = _rms(hn, gpre_ref[...]) * (1.0 + mod[4:5]) + mod[3:4]
    half = D_MODEL // 2
    lo = pltpu.bitcast(u[:, :half].astype(BF16).astype(F32), jnp.uint32) >> 16
    hi = pltpu.bitcast(u[:, half:].astype(BF16).astype(F32), jnp.uint32) & jnp.uint32(0xFFFF0000)
    u_ref[0] = lo | hi
    uh, ul = _split_bf16(u, 2)
    wh, wl = wr_ref[0], wr_ref[1]
    nt = (((1,), (1,)), ((), ()))
    lg_ref[0] = (lax.dot_general(wh, uh, nt, preferred_element_type=F32)
                 + lax.dot_general(wh, ul, nt, preferred_element_type=F32)
                 + lax.dot_general(wl, uh, nt, preferred_element_type=F32))


def _out_proj(fo, at, yf, yb, z, h, mod_l, g_ssd, g_post, g_pre, w_out_p, w_router, n_rows):
    bsz = fo.shape[0]
    d = D_MODEL
    tm = ROW_TILE
    n_lat = SEQ // tm
    row = lambda b, i: (b, i, 0)
    c2 = lambda b, i: (0, 0)
    wr = jnp.stack(_split_bf16(w_router.T, 2))
    h_specs, h_args = _stream_specs(h, tm)
    return pl.pallas_call(
        functools.partial(_out_kernel, n_h=len(h_args)),
        grid=(bsz, pl.cdiv(n_rows, tm)),
        in_specs=[pl.BlockSpec((1, tm, F_WIDTH), row), pl.BlockSpec((1, tm, ATT_WIDTH), row),
                  pl.BlockSpec((1, tm, SSD_WIDTH), row), pl.BlockSpec((1, tm, SSD_WIDTH), row),
                  pl.BlockSpec((1, tm, SSD_WIDTH), row)] + h_specs + [
                  pl.BlockSpec((1, 1, 6, d), lambda b, i: (jnp.where(i < n_lat, b, bsz), 0, 0, 0)),
                  pl.BlockSpec((1, SSD_WIDTH), c2), pl.BlockSpec((1, d), c2), pl.BlockSpec((1, d), c2),
                  pl.BlockSpec((d, d), c2), pl.BlockSpec((2, N_EXPERTS, d), lambda b, i: (0, 0, 0))],
        out_specs=[pl.BlockSpec((1, tm, d), row), pl.BlockSpec((1, tm, d // 2), row),
                   pl.BlockSpec((1, N_EXPERTS, tm), lambda b, i: (b, 0, i))],
        out_shape=[jax.ShapeDtypeStruct((bsz, n_rows, d), F32),
                   jax.ShapeDtypeStruct((bsz, n_rows, d // 2), jnp.uint32),
                   jax.ShapeDtypeStruct((bsz, N_EXPERTS, n_rows), F32)],
        compiler_params=_cparams(("parallel", "arbitrary"), VMEM_LIMIT),
        name="outproj_norms_router",
    )(fo, at, yf, yb, z, *h_args, mod_l.reshape(mod_l.shape[0], 1, 6, d), g_ssd.reshape(1, SSD_WIDTH),
      g_post.reshape(1, d), g_pre.reshape(1, d), w_out_p, wr)


def _prefix_lanes(mask, upper):
    n_r, t = mask.shape
    nb = t // LANE
    if (nb * n_r) % LANE:
        carry = jnp.zeros((n_r, 1), F32)
        outs = []
        for j in range(nb):
            pre = jnp.dot(mask[:, j * LANE:(j + 1) * LANE].astype(BF16), upper, preferred_element_type=F32) + carry
            outs.append(pre)
            carry = pre[:, LANE - 1:LANE]
        return jnp.concatenate(outs, axis=1)
    stacked = jnp.concatenate([mask[:, j * LANE:(j + 1) * LANE] for j in range(nb)], axis=0).astype(BF16)
    pre = jnp.dot(stacked, upper, preferred_element_type=F32)
    ri = lax.broadcasted_iota(jnp.int32, (LANE, LANE), 0)
    last_row = (ri == LANE - 1).astype(BF16)
    tot = jnp.dot(pre.astype(BF16), last_row, preferred_element_type=F32)
    ra = lax.broadcasted_iota(jnp.int32, (nb * n_r, nb * n_r), 0)
    ca = lax.broadcasted_iota(jnp.int32, (nb * n_r, nb * n_r), 1)
    assert n_r & (n_r - 1) == 0
    earlier = (((ca ^ ra) & (n_r - 1)) == 0) & (ca < ra)
    earlier = earlier.astype(BF16)
    full = pre + jnp.dot(earlier, tot.astype(BF16), preferred_element_type=F32)
    return jnp.concatenate([full[j * n_r:(j + 1) * n_r] for j in range(nb)], axis=1)


def _slot_tokens(csum, cap):
    n_r, t = csum.shape
    nb = t // LANE
    slot = lax.broadcasted_iota(jnp.int32, (1, cap), 1).astype(F32)
    rows = []
    if nb < 8:
        csum_t = jnp.concatenate([csum, jnp.zeros((LANE - n_r, t), F32)], axis=0).T
        for x in range(n_r):
            rows.append(jnp.sum((csum_t[:, x:x + 1] <= slot).astype(F32), axis=0, keepdims=True))
        return jnp.concatenate(rows, axis=0)
    blk_id = lax.broadcasted_iota(jnp.int32, (LANE, 1), 0).astype(F32)
    never = jnp.full((LANE - nb, LANE), float(t), F32)
    for x in range(n_r):
        cx = jnp.concatenate([csum[x:x + 1, j * LANE:(j + 1) * LANE] for j in range(nb)] + [never], axis=0)
        blk = jnp.sum((cx[:, LANE - 1:LANE] <= slot).astype(F32), axis=0, keepdims=True)
        pick = (blk_id == blk).astype(BF16)
        g = sum(jnp.dot(p, pick, preferred_element_type=F32) for p in _split_bf16(cx.T, 2))
        within = jnp.sum((g <= slot).astype(F32), axis=0, keepdims=True)
        rows.append(blk * LANE + within)
    return jnp.concatenate(rows, axis=0)


def _route_kernel(lg_ref, idx_ref, aff_ref, *, cap):
    l = lg_ref[0]
    n_e, t = l.shape
    e = jnp.exp(l - jnp.max(l, axis=0, keepdims=True))
    aff = e / jnp.sum(e, axis=0, keepdims=True)
    aff_ref[0] = aff
    keys = pltpu.bitcast(aff, jnp.int32)

    def search(it, lo):
        cand = lo | (jnp.int32(1) << (30 - it))
        cnt = jnp.sum((keys >= cand).astype(F32), axis=1, keepdims=True)
        return jnp.where(cnt >= cap, cand, lo)

    thr = lax.fori_loop(0, 31, search, jnp.zeros((n_e, 1), jnp.int32))
    gt = keys > thr
    eq = keys == thr
    need = cap - jnp.sum(gt.astype(F32), axis=1, keepdims=True)
    ri = lax.broadcasted_iota(jnp.int32, (LANE, LANE), 0)
    ci = lax.broadcasted_iota(jnp.int32, (LANE, LANE), 1)
    upper = (ri <= ci).astype(BF16)
    eq_rank = _prefix_lanes(eq.astype(F32), upper)
    sel = gt | (eq & (eq_rank <= need))
    csum = _prefix_lanes(sel.astype(F32), upper)
    idx_ref[0] = _slot_tokens(csum, cap).astype(jnp.int32)


def _route(logits_t, row0, t, cap):
    bsz, n_e, _ = logits_t.shape
    return pl.pallas_call(
        functools.partial(_route_kernel, cap=cap),
        grid=(bsz,),
        in_specs=[pl.BlockSpec((1, n_e, t), lambda b: (b, 0, row0 // t))],
        out_specs=[pl.BlockSpec((1, n_e, cap), lambda b: (b, 0, 0)),
                   pl.BlockSpec((1, n_e, t), lambda b: (b, 0, 0))],
        out_shape=[jax.ShapeDtypeStruct((bsz, n_e, cap), jnp.int32),
                   jax.ShapeDtypeStruct((bsz, n_e, t), F32)],
        compiler_params=_cparams(("parallel",), VMEM_LIMIT),
        name="expert_choice_route",
    )(logits_t)


def _smem_row(n):
    return pl.BlockSpec((1, 1, 1, n), lambda b, e: (b, e, 0, 0), memory_space=pltpu.SMEM)


def _gather_kernel(idx_ref, u_ref, o_ref, *, cap):
    group = 16

    def body(c, carry):
        base = pl.multiple_of(c * group, group)
        rows = [u_ref[0, pl.ds(idx_ref[0, 0, 0, base + r], 1), :] for r in range(group)]
        o_ref[0, pl.ds(base, group), :] = jnp.concatenate(rows, axis=0)
        return carry

    lax.fori_loop(0, cap // group, body, 0)


def _gather(u_packed, idx, row0, t, cap):
    bsz, _, half = u_packed.shape
    n_e = idx.shape[1]
    return pl.pallas_call(
        functools.partial(_gather_kernel, cap=cap),
        grid=(bsz, n_e),
        in_specs=[_smem_row(cap), pl.BlockSpec((1, t, half), lambda b, e: (b, row0 // t, 0))],
        out_specs=pl.BlockSpec((1, cap, half), lambda b, e: (e, b, 0)),
        out_shape=jax.ShapeDtypeStruct((n_e, bsz * cap, half), jnp.uint32),
        compiler_params=_cparams(("parallel", "arbitrary"), VMEM_LIMIT),
        name="expert_gather",
    )(idx[:, :, None, :], u_packed)


FF_TILE = 256


def _ffn_kernel(x_ref, wg_ref, wu_ref, wd_ref, o_ref, xs_ref):
    f = pl.program_id(2)

    @pl.when(f == 0)
    def _():
        w = x_ref[0]
        half = w.shape[1]
        xs_ref[:, :half] = pltpu.bitcast(w << 16, F32).astype(BF16)
        xs_ref[:, half:] = pltpu.bitcast(w & jnp.uint32(0xFFFF0000), F32).astype(BF16)
        o_ref[...] = jnp.zeros_like(o_ref)

    x = xs_ref[...]
    g = jnp.dot(x, wg_ref[0, 0].astype(BF16), preferred_element_type=F32)
    u = jnp.dot(x, wu_ref[0, 0].astype(BF16), preferred_element_type=F32)
    o_ref[0] += jnp.dot((_silu(g) * u).astype(BF16), wd_ref[0, 0].astype(BF16), preferred_element_type=F32)


def _expert_ffn(xe, w_gate, w_up, w_down, layer):
    n_e, m, half = xe.shape
    d = 2 * half
    ff = w_gate.shape[-1]
    tm = min(m, 2048)
    return pl.pallas_call(
        _ffn_kernel,
        grid=(n_e, m // tm, ff // FF_TILE),
        in_specs=[pl.BlockSpec((1, tm, half), lambda e, i, f: (e, i, 0)),
                  pl.BlockSpec((1, 1, d, FF_TILE), lambda e, i, f: (layer, e, 0, f)),
                  pl.BlockSpec((1, 1, d, FF_TILE), lambda e, i, f: (layer, e, 0, f)),
                  pl.BlockSpec((1, 1, FF_TILE, d), lambda e, i, f: (layer, e, f, 0))],
        out_specs=pl.BlockSpec((1, tm, d), lambda e, i, f: (e, i, 0)),
        out_shape=jax.ShapeDtypeStruct((n_e, m, d), F32),
        scratch_shapes=[pltpu.VMEM((tm, d), BF16)],
        compiler_params=_cparams(("parallel", "parallel", "arbitrary"), VMEM_LIMIT),
        name="expert_ffn",
    )(xe, w_gate, w_up, w_down)


COMB_ROWS = 8


def _combine_kernel(idx_ref, aff_ref, y_ref, *rest, cap):
    o_ref = rest[-1]

    @pl.when(pl.program_id(1) == 0)
    def _():
        o_ref[...] = jnp.zeros_like(o_ref)

    def body(c, carry):
        base = pl.multiple_of(c * COMB_ROWS, COMB_ROWS)
        toks = [idx_ref[0, 0, 0, base + r] for r in range(COMB_ROWS)]
        rows = [o_ref[0, pl.ds(t, 1), :] + aff_ref[0, 0, 0, t] * y_ref[0, pl.ds(base + r, 1), :]
                for r, t in enumerate(toks)]
        for t, row in zip(toks, rows):
            o_ref[0, pl.ds(t, 1), :] = row
        return carry

    lax.fori_loop(0, cap // COMB_ROWS, body, 0)


def _combine(y, idx, aff, row0, t, cap, s_total, out_prev=None):
    n_e, m, d = y.shape
    bsz = idx.shape[0]
    in_specs = [_smem_row(cap), _smem_row(t), pl.BlockSpec((1, cap, d), lambda b, e: (e, b, 0))]
    args = [idx[:, :, None, :], aff[:, :, None, :], y]
    aliases = {}
    if out_prev is not None:
        in_specs.append(pl.BlockSpec(memory_space=pl.ANY))
        args.append(out_prev)
        aliases = {3: 0}
    return pl.pallas_call(
        functools.partial(_combine_kernel, cap=cap),
        grid=(bsz, n_e),
        in_specs=in_specs,
        out_specs=pl.BlockSpec((1, t, d), lambda b, e: (b, row0 // t, 0)),
        out_shape=jax.ShapeDtypeStruct((bsz, s_total, d), F32),
        input_output_aliases=aliases,
        compiler_params=_cparams(("parallel", "arbitrary"), VMEM_LIMIT),
        name="expert_combine",
    )(*args)


def _moe(u_packed, logits_t, w_gate, w_up, w_down, layer, with_ctx):
    bsz, rows, _ = u_packed.shape
    out = None
    sets = [(0, SEQ)] + ([(SEQ, CTX_LEN)] if with_ctx else [])
    for row0, t in sets:
        cap = EC_CAPACITY * t // N_EXPERTS
        idx, aff = _route(logits_t, row0, t, cap)
        xe = _gather(u_packed, idx, row0, t, cap)
        y = _expert_ffn(xe, w_gate, w_up, w_down, layer)
        out = _combine(y, idx, aff, row0, t, cap, rows, out)
    return out


def _final_kernel(h_ref, f_ref, mod_ref, g_ref, o_ref):
    o_ref[0] = h_ref[0] + mod_ref[0, 0][5:6] * _rms(f_ref[0], g_ref[...])


def _final_residual(h, f, mod_l, g_post):
    bsz, n, d = h.shape
    tm = ROW_TILE
    row = lambda b, i: (b, i, 0)
    return pl.pallas_call(
        _final_kernel,
        grid=(bsz, n // tm),
        in_specs=[pl.BlockSpec((1, tm, d), row), pl.BlockSpec((1, tm, d), row),
                  pl.BlockSpec((1, 1, 6, d), lambda b, i: (b, 0, 0, 0)), pl.BlockSpec((1, d), lambda b, i: (0, 0))],
        out_specs=pl.BlockSpec((1, tm, d), row),
        out_shape=jax.ShapeDtypeStruct((bsz, n, d), F32),
        compiler_params=_cparams(("parallel", "arbitrary")),
        name="ffn_residual",
    )(h, f, mod_l.reshape(mod_l.shape[0], 1, 6, d), g_post.reshape(1, d))


def _rope_tables():
    n = jnp.arange(SEQ, dtype=jnp.int32)
    row = (n // GRID_W).astype(F32)
    col = (n % GRID_W).astype(F32)
    quarter = HEAD_DIM // 4
    inv = ROPE_THETA ** (-jnp.arange(quarter, dtype=F32) / quarter)
    lane = jnp.arange(LANE)
    pos = jnp.where((lane % HEAD_DIM < HEAD_DIM // 2)[None, :], row[:, None], col[:, None])
    ang = pos * inv[lane % quarter][None, :]
    first = (lane % (2 * quarter) < quarter)[None, :]
    cos = jnp.cos(ang)
    sin = jnp.sin(ang)
    sa = jnp.where(first, -sin, 0.0)
    sb = jnp.where(first, 0.0, sin)
    padc = lambda t, v: jnp.concatenate([t, jnp.full((CTX_LEN, LANE), v, F32)], axis=0)
    return padc(cos, 1.0), padc(sa, 0.0), padc(sb, 0.0)


def kernel(x, c, ctx, c_ctx, w_ada, b_ada, g_mix_pre, g_mix_post, g_ffn_pre, g_ffn_post, w_in, conv_w, conv_b,
           dt_bias, a_log, d_skip, g_ssd, attn_sink, w_out, w_router, w_gate, w_up, w_down):
    depth = w_ada.shape[0]
    bsz = x.shape[0]
    d = D_MODEL
    cc = jnp.concatenate([c, c_ctx[None, :], jnp.zeros((7, d), F32)], axis=0)
    mod = _modulation(cc, w_ada, b_ada).reshape(depth, bsz + 8, 6, d)
    rope_tabs = _rope_tables()
    chan_tab = _channel_tables()
    cn_lat, sn_lat = _dft_tables(SEQ)
    cn_ctx, sn_ctx = _dft_tables(CTX_LEN)
    q0 = F_WIDTH

    h = (x, ctx)
    f_prev = None
    for l in range(depth):
        last = l == depth - 1
        w_q = _to_slab_order(w_in[l][:, q0:q0 + ATT_WIDTH], 1) * HEAD_DIM ** -0.5
        w_in_p = jnp.concatenate([w_in[l][:, :q0], w_q, w_in[l][:, q0 + ATT_WIDTH:],
                                  jnp.zeros((d, IN_PAD - IN_WIDTH), F32)], axis=1).astype(BF16)
        w_out_p = jnp.concatenate([w_out[l][:q0], _to_slab_order(w_out[l][q0:q0 + ATT_WIDTH], 0),
                                   w_out[l][q0 + ATT_WIDTH:]], axis=0).astype(BF16)
        outs = _in_proj(h, mod[l], mod[l - 1] if l else None, g_mix_pre[l], w_in_p, rope_tabs, chan_tab,
                        not last, f_prev, g_ffn_post[l - 1] if l else None)
        if f_prev is not None:
            h, outs = outs[0], outs[1:]
        lat_ab, ctx_ab, q, k, v, z, xbc, dt_raw = outs
        fmix = _fourier(lat_ab, cn_lat, sn_lat, 0, out_rows=SEQ if last else S_ALL)
        if not last:
            fmix = _fourier(ctx_ab, cn_ctx, sn_ctx, SEQ, fmix)
        att = _attention(q, k, v, attn_sink[l], not last)
        y_f, y_b = _ssd_mixer(xbc, dt_raw, conv_w[l], conv_b[l], dt_bias[l], a_log[l], d_skip[l])
        n_rows = SEQ if last else S_ALL
        h, u_packed, logits_t = _out_proj(fmix, att, y_f, y_b, z, h, mod[l], g_ssd[l],
                                          g_mix_post[l], g_ffn_pre[l], w_out_p, w_router[l], n_rows)
        f_prev = _moe(u_packed, logits_t, w_gate, w_up, w_down, l, not last)
    return _final_residual(h, f_prev, mod[depth - 1], g_ffn_post[depth - 1])
```

```python
import functools
import math

import jax
import jax.numpy as jnp
from jax import lax
from jax.experimental import pallas as pl
from jax.experimental.pallas import tpu as pltpu

F32 = jnp.float32
BF16 = jnp.bfloat16

D_MODEL = 1024
SEQ = 4096
CTX_LEN = 256
S_ALL = SEQ + CTX_LEN
GRID_W = 64
HEAD_DIM = 64
EPS = 1e-6
F_WIDTH = 256
ATT_HEADS = 6
ATT_KV_HEADS = 2
ATT_WIDTH = 384
KV_WIDTH = 128
BLOCK = 128
ROPE_THETA = 10000.0
SSD_HEADS = 6
SSD_WIDTH = 384
SSD_GROUPS = 2
SSD_STATE = 128
CHUNK = 128
CONV_K = 5
CONV_DIM = 896
IN_WIDTH = 2188
IN_PAD = 2304
N_EXPERTS = 16
EXPERT_FF = 2816
EC_CAPACITY = 2

LANE = 128
ROW_TILE = 512
VMEM_LIMIT = 56 * 1024 * 1024


def _cparams(sem, vmem=None):
    return pltpu.CompilerParams(dimension_semantics=sem, vmem_limit_bytes=vmem)


def _split_bf16(x, n):
    parts, r = [], x
    for _ in range(n):
        p = r.astype(BF16)
        parts.append(p)
        r = r - p.astype(F32)
    return parts


def _rms(x, g):
    return x * lax.rsqrt(jnp.mean(x * x, axis=-1, keepdims=True) + EPS) * g


def _silu(x):
    return x * jax.nn.sigmoid(x)


def _mod_kernel(cc_ref, w_ref, b_ref, o_ref):
    s = _silu(cc_ref[...])
    o_ref[0] = jnp.dot(s.astype(BF16), w_ref[0].astype(BF16), preferred_element_type=F32) + b_ref[0]


def _modulation(cc, w_ada, b_ada):
    n_l, d, d6 = w_ada.shape
    r = cc.shape[0]
    tn = 1024
    return pl.pallas_call(
        _mod_kernel,
        grid=(n_l, d6 // tn),
        in_specs=[pl.BlockSpec((r, d), lambda l, j: (0, 0)),
                  pl.BlockSpec((1, d, tn), lambda l, j: (l, 0, j)),
                  pl.BlockSpec((1, 1, tn), lambda l, j: (l, 0, j))],
        out_specs=pl.BlockSpec((1, r, tn), lambda l, j: (l, 0, j)),
        out_shape=jax.ShapeDtypeStruct((n_l, r, d6), F32),
        compiler_params=_cparams(("arbitrary", "arbitrary")),
        name="adaln_mod",
    )(cc, w_ada, b_ada.reshape(n_l, 1, d6))


def _stream_specs(h, tm):
    n_lat = SEQ // tm
    if isinstance(h, tuple):
        lat, ctx = h
        d = lat.shape[-1]
        return ([pl.BlockSpec((1, tm, d), lambda b, i: (b, jnp.minimum(i, n_lat - 1), 0)),
                 pl.BlockSpec((1, CTX_LEN, d), lambda b, i: (b, 0, 0))], [lat, ctx])
    return [pl.BlockSpec((1, tm, h.shape[-1]), lambda b, i: (b, i, 0))], [h]


def _stream_tile(refs, tm):
    if len(refs) == 2:
        ctx = jnp.concatenate([refs[1][0]] * (tm // CTX_LEN), axis=0)
        return jnp.where(pl.program_id(1) < SEQ // tm, refs[0][0], ctx)
    return refs[0][0]


def _in_kernel(*refs, residual, n_h, ctx_fourier):
    h_refs, refs = refs[:n_h], refs[n_h:]
    if residual:
        f_ref, gpost_ref, refs = refs[0], refs[1], refs[2:]
    mod_ref, g_ref, w_ref, cos_ref, sa_ref, sb_ref, ch_ref = refs[:7]
    outs = list(refs[7:])
    if residual:
        hout_ref = outs.pop(0)
    if ctx_fourier:
        al_ref, bl_ref, ac_ref, bc_ref, q_ref, k_ref, v_ref, z_ref, xbc_ref, dt_ref = outs
    else:
        al_ref, bl_ref, q_ref, k_ref, v_ref, z_ref, xbc_ref, dt_ref = outs
    x = _stream_tile(h_refs, ROW_TILE)
    mod = mod_ref[0, 0]
    if residual:
        x = x + mod[5:6] * _rms(f_ref[0], gpost_ref[...])
        hout_ref[0] = x
    u = _rms(x, g_ref[...]) * (1.0 + mod[1:2]) + mod[0:1]
    p = jnp.dot(u.astype(BF16), w_ref[...], preferred_element_type=F32)
    cos, sa, sb = cos_ref[...], sa_ref[...], sb_ref[...]

    def rope(t):
        return t * cos + pltpu.roll(t, LANE - 16, 1) * sa + pltpu.roll(t, 16, 1) * sb

    ab = jnp.dot(p[:, 0:F_WIDTH].astype(BF16), ch_ref[...], preferred_element_type=F32).astype(BF16)
    is_lat = pl.program_id(1) < SEQ // ROW_TILE

    @pl.when(is_lat)
    def _():
        al_ref[...] = ab[:, :F_WIDTH]
        bl_ref[...] = ab[:, F_WIDTH:]

    if ctx_fourier:
        @pl.when(jnp.logical_not(is_lat))
        def _():
            ac_ref[...] = ab[:CTX_LEN, :F_WIDTH]
            bc_ref[...] = ab[:CTX_LEN, F_WIDTH:]

    q_ref[0] = jnp.concatenate([rope(p[:, 256 + LANE * s:256 + LANE * (s + 1)]) for s in range(3)],
                               axis=1).astype(BF16)
    k_ref[0] = rope(p[:, 640:768]).astype(BF16)
    v_ref[0] = p[:, 768:896].astype(BF16)
    z_ref[0] = p[:, 896:1280].astype(BF16)
    xbc_ref[0] = p[:, 1280:2176].astype(BF16)
    dt_ref[0] = p[:, 2176:2304]


def _in_proj(h, mod_l, mod_prev, g_pre, w_in_p, rope_tabs, chan_tab, ctx_fourier, f_prev=None, g_post_prev=None):
    bsz = mod_l.shape[0] - 8
    s, d = S_ALL, D_MODEL
    tm = ROW_TILE
    n_lat = SEQ // tm
    residual = f_prev is not None
    row = lambda b, i: (b, i, 0)
    mod_map = lambda b, i: (jnp.where(i < n_lat, b, bsz), 0, 0, 0)
    const2 = lambda b, i: (0, 0)
    tab = pl.BlockSpec((tm, LANE), lambda b, i: (i, 0))
    in_specs, args = _stream_specs(h, tm)
    n_h = len(args)
    if residual:
        in_specs += [pl.BlockSpec((1, tm, d), row), pl.BlockSpec((1, d), const2)]
        args += [f_prev, g_post_prev.reshape(1, d)]
    mod_used = mod_l if not residual else jnp.concatenate([mod_l[:, :5], mod_prev[:, 5:6]], axis=1)
    in_specs += [pl.BlockSpec((1, 1, 6, d), mod_map), pl.BlockSpec((1, d), const2),
                 pl.BlockSpec((d, IN_PAD), const2), tab, tab, tab, pl.BlockSpec(chan_tab.shape, const2)]
    args += [mod_used.reshape(mod_used.shape[0], 1, 6, d), g_pre.reshape(1, d), w_in_p, *rope_tabs, chan_tab]
    widths = [(384, BF16), (128, BF16), (128, BF16), (384, BF16), (896, BF16), (128, F32)]
    out_specs = [pl.BlockSpec((1, tm, w), row) for w, _ in widths]
    out_shape = [jax.ShapeDtypeStruct((bsz, s, w), dt) for w, dt in widths]
    lat_spec = pl.BlockSpec((tm, F_WIDTH), lambda b, i: (jnp.minimum(i, n_lat - 1), b))
    four_specs = [lat_spec, lat_spec]
    four_shape = [jax.ShapeDtypeStruct((SEQ, bsz * F_WIDTH), BF16)] * 2
    if ctx_fourier:
        four_specs += [pl.BlockSpec((CTX_LEN, F_WIDTH), lambda b, i: (0, b))] * 2
        four_shape += [jax.ShapeDtypeStruct((CTX_LEN, bsz * F_WIDTH), BF16)] * 2
    out_specs = four_specs + out_specs
    out_shape = four_shape + out_shape
    if residual:
        out_specs = [pl.BlockSpec((1, tm, d), row)] + out_specs
        out_shape = [jax.ShapeDtypeStruct((bsz, s, d), F32)] + out_shape
    outs = pl.pallas_call(
        functools.partial(_in_kernel, residual=residual, n_h=n_h, ctx_fourier=ctx_fourier),
        grid=(bsz, pl.cdiv(s, tm)),
        in_specs=in_specs, out_specs=out_specs, out_shape=out_shape,
        compiler_params=_cparams(("parallel", "arbitrary"), VMEM_LIMIT),
        name="prenorm_inproj",
    )(*args)
    outs = list(outs)
    head = [outs.pop(0)] if residual else []
    lat_ab = (outs.pop(0), outs.pop(0))
    ctx_ab = (outs.pop(0), outs.pop(0)) if ctx_fourier else None
    return head + [lat_ab, ctx_ab] + outs


def _f2_kernel(c_ref, s_ref, a_ref, b_ref, *rest):
    o_ref = rest[-1]
    y = (jnp.dot(c_ref[...], a_ref[...], preferred_element_type=F32)
         - jnp.dot(s_ref[...], b_ref[...], preferred_element_type=F32))
    o_ref[0] = y.astype(BF16)


def _dft_tables(n):
    sc = n ** -0.5

    def cs(rows, cols, period):
        m = (rows[:, None] * cols[None, :]) % period
        ang = m.astype(F32) * (2.0 * math.pi / period)
        return jnp.cos(ang), jnp.sin(ang)

    k = jnp.arange(n, dtype=jnp.int32)
    r = 64
    if n <= r * r // 16:
        c, s = cs(k, k, n)
        return (c * sc).astype(BF16), (s * sc).astype(BF16)
    ca, sa = cs(k, jnp.arange(n // r, dtype=jnp.int32), n // r)
    cb, sb = cs(k, jnp.arange(r, dtype=jnp.int32), n)
    c = ca[:, :, None] * cb[:, None, :] - sa[:, :, None] * sb[:, None, :]
    s = sa[:, :, None] * cb[:, None, :] + ca[:, :, None] * sb[:, None, :]
    return (c * sc).reshape(n, n).astype(BF16), (s * sc).reshape(n, n).astype(BF16)


def _channel_tables():
    c64, s64 = _dft_tables(HEAD_DIM)
    eye = jnp.eye(F_WIDTH // HEAD_DIM, dtype=BF16)
    return jnp.concatenate([jnp.kron(eye, c64), jnp.kron(eye, s64)], axis=1)


F2_EXT = 16


def _f2_sym_kernel(cm_ref, ce_ref, sm_ref, se_ref, a_ref, b_ref, jr_ref, o_ref, p_ref, q_ref, *, tm, nt):
    i = pl.program_id(1)
    half = nt // 2

    @pl.when(i == 0)
    def _():
        tail = half * tm + F2_EXT
        p_ref[tail:, :] = jnp.zeros((LANE - F2_EXT, p_ref.shape[1]), F32)
        q_ref[tail:, :] = jnp.zeros((LANE - F2_EXT, q_ref.shape[1]), F32)

    def emit(y):
        for j in range(o_ref.shape[0]):
            o_ref[j] = y[:, j * F_WIDTH:(j + 1) * F_WIDTH].astype(BF16)

    @pl.when(i < half)
    def _():
        a, b = a_ref[...], b_ref[...]
        row = pl.multiple_of(i * tm, tm)
        p = jnp.dot(cm_ref[...], a, preferred_element_type=F32)
        q = jnp.dot(sm_ref[...], b, preferred_element_type=F32)
        p_ref[pl.ds(row, tm), :] = p
        q_ref[pl.ds(row, tm), :] = q
        p_ref[pl.ds(row + tm, F2_EXT), :] = jnp.dot(ce_ref[...], a, preferred_element_type=F32)
        q_ref[pl.ds(row + tm, F2_EXT), :] = jnp.dot(se_ref[...], b, preferred_element_type=F32)
        emit(p - q)

    @pl.when(i >= half)
    def _():
        row = pl.multiple_of((nt - 1 - i) * tm, tm)
        s = (p_ref[pl.ds(row, tm + LANE), :] + q_ref[pl.ds(row, tm + LANE), :]).astype(BF16)
        emit(jnp.dot(jr_ref[...], s, preferred_element_type=F32))


def _fourier_positions_sym(cn, sn, a_all, b_all, bsz, n, tm, blk0, out_rows):
    nt = n // tm
    half = nt // 2
    per = tm // F2_EXT
    main = lambda b, i: (jnp.minimum(i, half - 1), 0)
    ext = lambda b, i: ((jnp.minimum(i, half - 1) + 1) * per, 0)
    r = jnp.arange(tm)[:, None]
    flip = (jnp.arange(tm + LANE)[None, :] == tm - r).astype(BF16)
    grp = 2 if bsz % 2 == 0 else 1
    cols = grp * F_WIDTH
    return pl.pallas_call(
        functools.partial(_f2_sym_kernel, tm=tm, nt=nt),
        grid=(bsz // grp, nt),
        in_specs=[pl.BlockSpec((tm, n), main), pl.BlockSpec((F2_EXT, n), ext),
                  pl.BlockSpec((tm, n), main), pl.BlockSpec((F2_EXT, n), ext),
                  pl.BlockSpec((n, cols), lambda b, i: (0, b)),
                  pl.BlockSpec((n, cols), lambda b, i: (0, b)),
                  pl.BlockSpec((tm, tm + LANE), lambda b, i: (0, 0))],
        out_specs=pl.BlockSpec((grp, tm, F_WIDTH), lambda b, i: (b, blk0 + i, 0)),
        out_shape=jax.ShapeDtypeStruct((bsz, out_rows, F_WIDTH), BF16),
        scratch_shapes=[pltpu.VMEM((n // 2 + LANE, cols), F32)] * 2,
        compiler_params=_cparams(("parallel", "arbitrary"), VMEM_LIMIT),
        name="fourier_positions",
    )(cn, cn, sn, sn, a_all, b_all, flip)


def _fourier(ab, cn, sn, row0, out_prev=None, out_rows=S_ALL):
    a_all, b_all = ab
    n = a_all.shape[0]
    bsz = a_all.shape[1] // F_WIDTH
    tm = min(n, 512)
    nt = n // tm
    blk0 = row0 // tm
    if nt >= 4 and out_prev is None:
        return _fourier_positions_sym(cn, sn, a_all, b_all, bsz, n, tm, blk0, out_rows)
    in_specs = [pl.BlockSpec((tm, n), lambda i, b: (i, 0)),
                pl.BlockSpec((tm, n), lambda i, b: (i, 0)),
                pl.BlockSpec((n, F_WIDTH), lambda i, b: (0, b)),
                pl.BlockSpec((n, F_WIDTH), lambda i, b: (0, b))]
    args = [cn, sn, a_all, b_all]
    aliases = {}
    if out_prev is not None:
        in_specs.append(pl.BlockSpec(memory_space=pl.ANY))
        args.append(out_prev)
        aliases = {4: 0}
    return pl.pallas_call(
        _f2_kernel,
        grid=(nt, bsz),
        in_specs=in_specs,
        out_specs=pl.BlockSpec((1, tm, F_WIDTH), lambda i, b: (b, blk0 + i, 0)),
        out_shape=jax.ShapeDtypeStruct((bsz, out_rows, F_WIDTH), BF16),
        input_output_aliases=aliases,
        compiler_params=_cparams(("parallel", "arbitrary"), VMEM_LIMIT),
        name="fourier_positions",
    )(*args)


def _to_slab_order(w, axis):
    r = ATT_HEADS // ATT_KV_HEADS
    shape = w.shape
    w = w.reshape(shape[:axis] + (ATT_KV_HEADS, r, HEAD_DIM) + shape[axis + 1:])
    return jnp.swapaxes(w, axis, axis + 1).reshape(shape)


def _attend(q, k, v, masks, sink_ref):
    n = q.shape[0]
    r = ATT_HEADS // ATT_KV_HEADS
    lo = lax.broadcasted_iota(jnp.int32, (1, LANE), 1) < HEAD_DIM
    rid = lax.broadcasted_iota(jnp.int32, (r * n, 1), 0)
    outs = []
    for g in range(ATT_KV_HEADS):
        keep = lo if g == 0 else jnp.logical_not(lo)
        qg = jnp.concatenate([jnp.where(keep, q[:, LANE * s:LANE * (s + 1)], jnp.zeros((), BF16))
                              for s in range(r)], axis=0)
        snk = jnp.full((r * n, 1), sink_ref[r * g], F32)
        for j in range(1, r):
            snk = jnp.where(rid >= j * n, sink_ref[r * g + j], snk)
        sc = lax.dot_general(qg, k, (((1,), (1,)), ((), ())), preferred_element_type=F32)
        if masks:
            nblk = sc.shape[1] // LANE
            sc = jnp.concatenate(
                [jnp.where(masks[b], sc[:, b * LANE:(b + 1) * LANE], -1e30) if b in masks
                 else sc[:, b * LANE:(b + 1) * LANE] for b in range(nblk)], axis=1)
        mx = jnp.maximum(jnp.max(sc, axis=-1, keepdims=True), snk)
        p = jnp.exp(sc - mx)
        den = jnp.sum(p, axis=-1, keepdims=True) + jnp.exp(snk - mx)
        outs.append(jnp.dot(p.astype(BF16), v, preferred_element_type=F32) / den)
    return jnp.concatenate([jnp.where(lo, outs[0][s * n:(s + 1) * n], outs[1][s * n:(s + 1) * n])
                            for s in range(r)], axis=1)


def _attn_lat_kernel(sink_ref, q_ref, kp_ref, kc_ref, kn_ref, kx_ref, vp_ref, vc_ref, vn_ref, vx_ref, o_ref):
    nb = pl.num_programs(1)
    n = pl.program_id(1)
    rows = (ATT_HEADS // ATT_KV_HEADS) * BLOCK
    i = lax.broadcasted_iota(jnp.int32, (rows, BLOCK), 0) & (BLOCK - 1)
    j = lax.broadcasted_iota(jnp.int32, (rows, BLOCK), 1)
    m_prev = (j >= i) & (n > 0)
    m_next = (j <= i) & (n < nb - 1)
    k = jnp.concatenate([kp_ref[0], kc_ref[0], kn_ref[0], kx_ref[0]], axis=0)
    v = jnp.concatenate([vp_ref[0], vc_ref[0], vn_ref[0], vx_ref[0]], axis=0)
    o_ref[0] = _attend(q_ref[0], k, v, {0: m_prev, 2: m_next}, sink_ref).astype(BF16)


def _attn_ctx_kernel(sink_ref, q_ref, kx_ref, vx_ref, prev_ref, o_ref):
    o_ref[0] = _attend(q_ref[0], kx_ref[0], vx_ref[0], {}, sink_ref).astype(BF16)


def _attention(q, k, v, sink, with_ctx):
    bsz = q.shape[0]
    s = S_ALL if with_ctx else SEQ
    nb = SEQ // BLOCK
    ctx_blk = SEQ // CTX_LEN
    smem = pl.BlockSpec(memory_space=pltpu.SMEM)
    kv = lambda f: pl.BlockSpec((1, BLOCK, KV_WIDTH), f)
    prev = lambda b, n: (b, jnp.maximum(n - 1, 0), 0)
    cur = lambda b, n: (b, n, 0)
    nxt = lambda b, n: (b, jnp.minimum(n + 1, nb - 1), 0)
    ctx = pl.BlockSpec((1, CTX_LEN, KV_WIDTH), lambda b, n: (b, ctx_blk, 0))
    sink8 = jnp.pad(sink, (0, 8 - ATT_HEADS))
    out = pl.pallas_call(
        _attn_lat_kernel,
        grid=(bsz, nb),
        in_specs=[smem, pl.BlockSpec((1, BLOCK, ATT_WIDTH), cur),
                  kv(prev), kv(cur), kv(nxt), ctx, kv(prev), kv(cur), kv(nxt), ctx],
        out_specs=pl.BlockSpec((1, BLOCK, ATT_WIDTH), cur),
        out_shape=jax.ShapeDtypeStruct((bsz, s, ATT_WIDTH), BF16),
        compiler_params=_cparams(("parallel", "arbitrary")),
        name="window_attention",
    )(sink8, q, k, k, k, k, v, v, v, v)
    if not with_ctx:
        return out
    cq = lambda b: (b, ctx_blk, 0)
    return pl.pallas_call(
        _attn_ctx_kernel,
        grid=(bsz,),
        in_specs=[smem, pl.BlockSpec((1, CTX_LEN, ATT_WIDTH), cq),
                  pl.BlockSpec((1, CTX_LEN, KV_WIDTH), cq), pl.BlockSpec((1, CTX_LEN, KV_WIDTH), cq),
                  pl.BlockSpec(memory_space=pl.ANY)],
        out_specs=pl.BlockSpec((1, CTX_LEN, ATT_WIDTH), cq),
        out_shape=jax.ShapeDtypeStruct((bsz, s, ATT_WIDTH), BF16),
        input_output_aliases={4: 0},
        compiler_params=_cparams(("parallel",)),
        name="context_attention",
    )(sink8, q, k, v, out)


HALO = 16
N_SHIFT = CONV_K - 1


def _shift_table():
    i = jnp.arange(CHUNK)[:, None]
    j = jnp.arange(CHUNK + 2 * HALO)[None, :]
    taps = [k for k in range(CONV_K) if k != CONV_K // 2]
    return jnp.concatenate([(j == i + k - CONV_K // 2 + HALO) for k in taps], axis=0).astype(BF16)


def _expand_table(lane0):
    src = jnp.arange(LANE)[:, None] - lane0
    return (src == (jnp.arange(SSD_WIDTH)[None, :] // HEAD_DIM)).astype(BF16)


def _conv_chunk(c, xp_ref, xc_ref, xn_ref, sh_ref, w_ref, b_ref):
    n_lat = SEQ // CHUNK
    last = S_ALL // CHUNK - 1
    has_prev = (c != 0) & (c != n_lat)
    has_next = (c != n_lat - 1) & (c != last)
    zero = jnp.zeros((), BF16)
    cur = xc_ref[0]
    xx = jnp.concatenate([jnp.where(has_prev, xp_ref[0], zero), cur, jnp.where(has_next, xn_ref[0], zero)], axis=0)
    sh = jnp.dot(sh_ref[...], xx, preferred_element_type=F32)
    mid = CONV_K // 2
    acc = b_ref[...] + w_ref[mid:mid + 1, :] * cur.astype(F32)
    for t in range(N_SHIFT):
        k = t if t < mid else t + 1
        acc = acc + w_ref[k:k + 1, :] * sh[t * CHUNK:(t + 1) * CHUNK]
    return _silu(acc)


def _ssd_chunk(xc, dt_raw, dtb, alog, ex_ref, dsk, state_ref, reverse):
    xs = xc[:, :SSD_WIDTH].astype(F32)
    gs = SSD_GROUPS * SSD_STATE
    bm = xc[:, SSD_WIDTH:SSD_WIDTH + gs]
    cm = xc[:, SSD_WIDTH + gs:SSD_WIDTH + 2 * gs]
    raw = dt_raw + dtb
    dt = jnp.maximum(raw, 0.0) + jnp.log(1.0 + jnp.exp(-jnp.abs(raw)))
    dta = dt * (-jnp.exp(alog))
    ri = lax.broadcasted_iota(jnp.int32, (CHUNK, CHUNK), 0)
    ci = lax.broadcasted_iota(jnp.int32, (CHUNK, CHUNK), 1)
    keep = (ri <= ci) if reverse else (ri >= ci)
    tri = keep.astype(BF16)
    acs = sum(jnp.dot(tri, p, preferred_element_type=F32) for p in _split_bf16(dta, 3))
    acs_t = acs.T
    parts = jnp.concatenate(_split_bf16(dt, 2) + _split_bf16(acs, 3), axis=0)
    wide = jnp.dot(parts, ex_ref[...], preferred_element_type=F32)
    dt_x = wide[:CHUNK] + wide[CHUNK:2 * CHUNK]
    acs_x = wide[2 * CHUNK:3 * CHUNK] + wide[3 * CHUNK:4 * CHUNK] + wide[4 * CHUNK:]
    edge = 0 if reverse else CHUNK - 1
    tot_x = acs_x[edge:edge + 1, :]
    rh = SSD_HEADS // SSD_GROUPS
    gw = rh * HEAD_DIM
    lane0 = SSD_HEADS if reverse else 0
    xd = xs * dt_x
    xdw = (xd * jnp.exp(tot_x - acs_x)).astype(BF16)
    xdb = xd.astype(BF16)
    eacs = jnp.exp(acs_x)
    state = state_ref[...]
    ys = []
    new_state = []
    for g in range(SSD_GROUPS):
        bg = bm[:, g * SSD_STATE:(g + 1) * SSD_STATE]
        cg = cm[:, g * SSD_STATE:(g + 1) * SSD_STATE]
        cb = lax.dot_general(cg, bg, (((1,), (1,)), ((), ())), preferred_element_type=F32)
        sg = state[:, g * gw:(g + 1) * gw]
        y_off = jnp.dot(cg, sg.astype(BF16), preferred_element_type=F32) * eacs[:, g * gw:(g + 1) * gw]
        for r in range(rh):
            h = g * rh + r
            ln = lane0 + h
            dec = jnp.where(keep, jnp.exp(acs[:, ln:ln + 1] - acs_t[ln:ln + 1, :]), 0.0)
            yd = jnp.dot((cb * dec).astype(BF16), xdb[:, h * HEAD_DIM:(h + 1) * HEAD_DIM],
                         preferred_element_type=F32)
            ys.append(yd + y_off[:, r * HEAD_DIM:(r + 1) * HEAD_DIM])
        bt = bg.astype(F32).T.astype(BF16)
        new_state.append(jnp.dot(bt, xdw[:, g * gw:(g + 1) * gw], preferred_element_type=F32))
    state_ref[...] = state * jnp.exp(tot_x) + jnp.concatenate(new_state, axis=1)
    y = jnp.concatenate(ys, axis=1)
    return y if dsk is None else y + dsk * xs


def _ssd_kernel(fp_ref, fc_ref, fn_ref, fdt_ref, bp_ref, bc_ref, bn_ref, bdt_ref,
                sh_ref, w_ref, b_ref, dtb_ref, alog_ref, exf_ref, exb_ref, dsk_ref,
                yf_ref, yb_ref, sf_ref, sb_ref, xcache_ref):
    step = pl.program_id(1)
    nc = pl.num_programs(1)
    n_lat = SEQ // CHUNK

    @pl.when(step == 0)
    def _():
        sf_ref[...] = jnp.zeros_like(sf_ref)
        sb_ref[...] = jnp.zeros_like(sb_ref)

    cf = (step + n_lat) % nc
    cbk = nc - 1 - step

    @pl.when((step != 1) & (step < n_lat // 2 + 2))
    def _():
        xcache_ref[cf] = _conv_chunk(cf, fp_ref, fc_ref, fn_ref, sh_ref, w_ref, b_ref).astype(BF16)
        xcache_ref[cbk] = _conv_chunk(cbk, bp_ref, bc_ref, bn_ref, sh_ref, w_ref, b_ref).astype(BF16)

    dtb, alog = dtb_ref[...], alog_ref[...]
    yf_ref[0] = _ssd_chunk(xcache_ref[cf], fdt_ref[0], dtb, alog, exf_ref, dsk_ref[...], sf_ref,
                           False).astype(BF16)
    yb_ref[0] = _ssd_chunk(xcache_ref[cbk], bdt_ref[0], dtb, alog, exb_ref, None, sb_ref, True).astype(BF16)


def _ssd_mixer(xbc, dt_raw, conv_w, conv_b, dt_bias, a_log, d_skip):
    bsz, s, cdim = xbc.shape
    nc = s // CHUNK
    n_lat = SEQ // CHUNK
    per = CHUNK // HALO
    nh = s // HALO
    assert nc == n_lat + 2 and n_lat % 2 == 0
    fwd = lambda t: (t + n_lat) % nc
    bwd = lambda t: nc - 1 - t

    def chunk_specs(order):
        return [pl.BlockSpec((1, HALO, cdim), lambda b, t: (b, jnp.maximum(order(t) * per - 1, 0), 0)),
                pl.BlockSpec((1, CHUNK, cdim), lambda b, t: (b, order(t), 0)),
                pl.BlockSpec((1, HALO, cdim), lambda b, t: (b, jnp.minimum(order(t) * per + per, nh - 1), 0)),
                pl.BlockSpec((1, CHUNK, LANE), lambda b, t: (b, order(t), 0))]

    c2 = lambda b, t: (0, 0)
    full = lambda a: pl.BlockSpec(a.shape, c2)
    pad = lambda p: jnp.pad(p.reshape(1, -1), ((0, 0), (0, LANE - 2 * SSD_HEADS)))
    consts = [_shift_table(), jnp.pad(conv_w, ((0, 8 - CONV_K), (0, 0))), conv_b.reshape(1, cdim),
              pad(dt_bias), pad(a_log), _expand_table(0), _expand_table(SSD_HEADS),
              jnp.repeat(d_skip, HEAD_DIM).reshape(1, SSD_WIDTH)]
    return pl.pallas_call(
        _ssd_kernel,
        grid=(bsz, nc),
        in_specs=chunk_specs(fwd) + chunk_specs(bwd) + [full(a) for a in consts],
        out_specs=[pl.BlockSpec((1, CHUNK, SSD_WIDTH), lambda b, t: (b, fwd(t), 0)),
                   pl.BlockSpec((1, CHUNK, SSD_WIDTH), lambda b, t: (b, bwd(t), 0))],
        out_shape=[jax.ShapeDtypeStruct((bsz, s, SSD_WIDTH), BF16)] * 2,
        scratch_shapes=[pltpu.VMEM((SSD_STATE, SSD_WIDTH), F32)] * 2 + [pltpu.VMEM((nc, CHUNK, cdim), BF16)],
        compiler_params=_cparams(("parallel", "arbitrary"), VMEM_LIMIT),
        name="conv_ssd_scan",
    )(xbc, xbc, xbc, dt_raw, xbc, xbc, xbc, dt_raw, *consts)


def _out_kernel(*refs, n_h):
    (fo_ref, at_ref, yf_ref, yb_ref, z_ref), refs = refs[:5], refs[5:]
    h_refs, refs = refs[:n_h], refs[n_h:]
    mod_ref, gssd_ref, gpost_ref, gpre_ref, wo_ref, wr_ref, hout_ref, u_ref, lg_ref = refs
    y = (yf_ref[0].astype(F32) + yb_ref[0].astype(F32)) * _silu(z_ref[0].astype(F32))
    sm = _rms(y, gssd_ref[...]).astype(BF16)
    m = (jnp.dot(fo_ref[0], wo_ref[0:256, :], preferred_element_type=F32)
         + jnp.dot(at_ref[0], wo_ref[256:640, :], preferred_element_type=F32)
         + jnp.dot(sm, wo_ref[640:1024, :], preferred_element_type=F32))
    mod = mod_ref[0, 0]
    hn = _stream_tile(h_refs, ROW_TILE) + mod[2:3] * _rms(m, gpost_ref[...])
    hout_ref[0] = hn
    u = _rms(hn, gpre_ref[...]) * (1.0 + mod[4:5]) + mod[3:4]
    half = D_MODEL // 2
    lo = pltpu.bitcast(u[:, :half].astype(BF16).astype(F32), jnp.uint32) >> 16
    hi = pltpu.bitcast(u[:, half:].astype(BF16).astype(F32), jnp.uint32) & jnp.uint32(0xFFFF0000)
    u_ref[0] = lo | hi
    uh, ul = _split_bf16(u, 2)
    wh, wl = wr_ref[0], wr_ref[1]
    nt = (((1,), (1,)), ((), ()))
    lg_ref[0] = (lax.dot_general(wh, uh, nt, preferred_element_type=F32)
                 + lax.dot_general(wh, ul, nt, preferred_element_type=F32)
                 + lax.dot_general(wl, uh, nt, preferred_element_type=F32))


def _out_proj(fo, at, yf, yb, z, h, mod_l, g_ssd, g_post, g_pre, w_out_p, w_router, n_rows):
    bsz = fo.shape[0]
    d = D_MODEL
    tm = ROW_TILE
    n_lat = SEQ // tm
    row = lambda b, i: (b, i, 0)
    c2 = lambda b, i: (0, 0)
    wr = jnp.stack(_split_bf16(w_router.T, 2))
    h_specs, h_args = _stream_specs(h, tm)
    return pl.pallas_call(
        functools.partial(_out_kernel, n_h=len(h_args)),
        grid=(bsz, pl.cdiv(n_rows, tm)),
        in_specs=[pl.BlockSpec((1, tm, F_WIDTH), row), pl.BlockSpec((1, tm, ATT_WIDTH), row),
                  pl.BlockSpec((1, tm, SSD_WIDTH), row), pl.BlockSpec((1, tm, SSD_WIDTH), row),
                  pl.BlockSpec((1, tm, SSD_WIDTH), row)] + h_specs + [
                  pl.BlockSpec((1, 1, 6, d), lambda b, i: (jnp.where(i < n_lat, b, bsz), 0, 0, 0)),
                  pl.BlockSpec((1, SSD_WIDTH), c2), pl.BlockSpec((1, d), c2), pl.BlockSpec((1, d), c2),
                  pl.BlockSpec((d, d), c2), pl.BlockSpec((2, N_EXPERTS, d), lambda b, i: (0, 0, 0))],
        out_specs=[pl.BlockSpec((1, tm, d), row), pl.BlockSpec((1, tm, d // 2), row),
                   pl.BlockSpec((1, N_EXPERTS, tm), lambda b, i: (b, 0, i))],
        out_shape=[jax.ShapeDtypeStruct((bsz, n_rows, d), F32),
                   jax.ShapeDtypeStruct((bsz, n_rows, d // 2), jnp.uint32),
                   jax.ShapeDtypeStruct((bsz, N_EXPERTS, n_rows), F32)],
        compiler_params=_cparams(("parallel", "arbitrary"), VMEM_LIMIT),
        name="outproj_norms_router",
    )(fo, at, yf, yb, z, *h_args, mod_l.reshape(mod_l.shape[0], 1, 6, d), g_ssd.reshape(1, SSD_WIDTH),
      g_post.reshape(1, d), g_pre.reshape(1, d), w_out_p, wr)


def _prefix_lanes(mask, upper):
    n_r, t = mask.shape
    nb = t // LANE
    if (nb * n_r) % LANE:
        carry = jnp.zeros((n_r, 1), F32)
        outs = []
        for j in range(nb):
            pre = jnp.dot(mask[:, j * LANE:(j + 1) * LANE].astype(BF16), upper, preferred_element_type=F32) + carry
            outs.append(pre)
            carry = pre[:, LANE - 1:LANE]
        return jnp.concatenate(outs, axis=1)
    stacked = jnp.concatenate([mask[:, j * LANE:(j + 1) * LANE] for j in range(nb)], axis=0).astype(BF16)
    pre = jnp.dot(stacked, upper, preferred_element_type=F32)
    ri = lax.broadcasted_iota(jnp.int32, (LANE, LANE), 0)
    last_row = (ri == LANE - 1).astype(BF16)
    tot = jnp.dot(pre.astype(BF16), last_row, preferred_element_type=F32)
    ra = lax.broadcasted_iota(jnp.int32, (nb * n_r, nb * n_r), 0)
    ca = lax.broadcasted_iota(jnp.int32, (nb * n_r, nb * n_r), 1)
    assert n_r & (n_r - 1) == 0
    earlier = (((ca ^ ra) & (n_r - 1)) == 0) & (ca < ra)
    earlier = earlier.astype(BF16)
    full = pre + jnp.dot(earlier, tot.astype(BF16), preferred_element_type=F32)
    return jnp.concatenate([full[j * n_r:(j + 1) * n_r] for j in range(nb)], axis=1)


def _slot_tokens(csum, cap):
    n_r, t = csum.shape
    nb = t // LANE
    slot = lax.broadcasted_iota(jnp.int32, (1, cap), 1).astype(F32)
    rows = []
    if nb < 8:
        csum_t = jnp.concatenate([csum, jnp.zeros((LANE - n_r, t), F32)], axis=0).T
        for x in range(n_r):
            rows.append(jnp.sum((csum_t[:, x:x + 1] <= slot).astype(F32), axis=0, keepdims=True))
        return jnp.concatenate(rows, axis=0)
    blk_id = lax.broadcasted_iota(jnp.int32, (LANE, 1), 0).astype(F32)
    never = jnp.full((LANE - nb, LANE), float(t), F32)
    for x in range(n_r):
        cx = jnp.concatenate([csum[x:x + 1, j * LANE:(j + 1) * LANE] for j in range(nb)] + [never], axis=0)
        blk = jnp.sum((cx[:, LANE - 1:LANE] <= slot).astype(F32), axis=0, keepdims=True)
        pick = (blk_id == blk).astype(BF16)
        g = sum(jnp.dot(p, pick, preferred_element_type=F32) for p in _split_bf16(cx.T, 2))
        within = jnp.sum((g <= slot).astype(F32), axis=0, keepdims=True)
        rows.append(blk * LANE + within)
    return jnp.concatenate(rows, axis=0)


def _route_kernel(lg_ref, idx_ref, aff_ref, *, cap):
    l = lg_ref[0]
    n_e, t = l.shape
    e = jnp.exp(l - jnp.max(l, axis=0, keepdims=True))
    aff = e / jnp.sum(e, axis=0, keepdims=True)
    aff_ref[0] = aff
    keys = pltpu.bitcast(aff, jnp.int32)

    def search(it, lo):
        cand = lo | (jnp.int32(1) << (30 - it))
        cnt = jnp.sum((keys >= cand).astype(F32), axis=1, keepdims=True)
        return jnp.where(cnt >= cap, cand, lo)

    thr = lax.fori_loop(0, 31, search, jnp.zeros((n_e, 1), jnp.int32))
    gt = keys > thr
    eq = keys == thr
    need = cap - jnp.sum(gt.astype(F32), axis=1, keepdims=True)
    ri = lax.broadcasted_iota(jnp.int32, (LANE, LANE), 0)
    ci = lax.broadcasted_iota(jnp.int32, (LANE, LANE), 1)
    upper = (ri <= ci).astype(BF16)
    eq_rank = _prefix_lanes(eq.astype(F32), upper)
    sel = gt | (eq & (eq_rank <= need))
    csum = _prefix_lanes(sel.astype(F32), upper)
    idx_ref[0] = _slot_tokens(csum, cap).astype(jnp.int32)


def _route(logits_t, row0, t, cap):
    bsz, n_e, _ = logits_t.shape
    return pl.pallas_call(
        functools.partial(_route_kernel, cap=cap),
        grid=(bsz,),
        in_specs=[pl.BlockSpec((1, n_e, t), lambda b: (b, 0, row0 // t))],
        out_specs=[pl.BlockSpec((1, n_e, cap), lambda b: (b, 0, 0)),
                   pl.BlockSpec((1, n_e, t), lambda b: (b, 0, 0))],
        out_shape=[jax.ShapeDtypeStruct((bsz, n_e, cap), jnp.int32),
                   jax.ShapeDtypeStruct((bsz, n_e, t), F32)],
        compiler_params=_cparams(("parallel",), VMEM_LIMIT),
        name="expert_choice_route",
    )(logits_t)


def _smem_row(n):
    return pl.BlockSpec((1, 1, 1, n), lambda b, e: (b, e, 0, 0), memory_space=pltpu.SMEM)


def _gather_kernel(idx_ref, u_ref, o_ref, *, cap):
    group = 32

    def body(c, carry):
        base = pl.multiple_of(c * group, group)
        rows = [u_ref[0, pl.ds(idx_ref[0, 0, 0, base + r], 1), :] for r in range(group)]
        o_ref[0, pl.ds(base, group), :] = jnp.concatenate(rows, axis=0)
        return carry

    lax.fori_loop(0, cap // group, body, 0)


def _gather(u_packed, idx, row0, t, cap):
    bsz, _, half = u_packed.shape
    n_e = idx.shape[1]
    return pl.pallas_call(
        functools.partial(_gather_kernel, cap=cap),
        grid=(bsz, n_e),
        in_specs=[_smem_row(cap), pl.BlockSpec((1, t, half), lambda b, e: (b, row0 // t, 0))],
        out_specs=pl.BlockSpec((1, cap, half), lambda b, e: (e, b, 0)),
        out_shape=jax.ShapeDtypeStruct((n_e, bsz * cap, half), jnp.uint32),
        compiler_params=_cparams(("parallel", "arbitrary"), VMEM_LIMIT),
        name="expert_gather",
    )(idx[:, :, None, :], u_packed)


FF_TILE = 256


def _ffn_kernel(x_ref, wg_ref, wu_ref, wd_ref, o_ref, xs_ref):
    f = pl.program_id(2)

    @pl.when(f == 0)
    def _():
        w = x_ref[0]
        half = w.shape[1]
        xs_ref[:, :half] = pltpu.bitcast(w << 16, F32).astype(BF16)
        xs_ref[:, half:] = pltpu.bitcast(w & jnp.uint32(0xFFFF0000), F32).astype(BF16)
        o_ref[...] = jnp.zeros_like(o_ref)

    x = xs_ref[...]
    g = jnp.dot(x, wg_ref[0, 0].astype(BF16), preferred_element_type=F32)
    u = jnp.dot(x, wu_ref[0, 0].astype(BF16), preferred_element_type=F32)
    o_ref[0] += jnp.dot((_silu(g) * u).astype(BF16), wd_ref[0, 0].astype(BF16), preferred_element_type=F32)


def _expert_ffn(xe, w_gate, w_up, w_down, layer):
    n_e, m, half = xe.shape
    d = 2 * half
    ff = w_gate.shape[-1]
    tm = min(m, 2048)
    return pl.pallas_call(
        _ffn_kernel,
        grid=(n_e, m // tm, ff // FF_TILE),
        in_specs=[pl.BlockSpec((1, tm, half), lambda e, i, f: (e, i, 0)),
                  pl.BlockSpec((1, 1, d, FF_TILE), lambda e, i, f: (layer, e, 0, f)),
                  pl.BlockSpec((1, 1, d, FF_TILE), lambda e, i, f: (layer, e, 0, f)),
                  pl.BlockSpec((1, 1, FF_TILE, d), lambda e, i, f: (layer, e, f, 0))],
        out_specs=pl.BlockSpec((1, tm, d), lambda e, i, f: (e, i, 0)),
        out_shape=jax.ShapeDtypeStruct((n_e, m, d), F32),
        scratch_shapes=[pltpu.VMEM((tm, d), BF16)],
        compiler_params=_cparams(("parallel", "parallel", "arbitrary"), VMEM_LIMIT),
        name="expert_ffn",
    )(xe, w_gate, w_up, w_down)


COMB_ROWS = 8


def _combine_kernel(idx_ref, aff_ref, y_ref, *rest, cap):
    o_ref = rest[-1]

    @pl.when(pl.program_id(1) == 0)
    def _():
        o_ref[...] = jnp.zeros_like(o_ref)

    def body(c, carry):
        base = pl.multiple_of(c * COMB_ROWS, COMB_ROWS)
        toks = [idx_ref[0, 0, 0, base + r] for r in range(COMB_ROWS)]
        rows = [o_ref[0, pl.ds(t, 1), :] + aff_ref[0, 0, 0, t] * y_ref[0, pl.ds(base + r, 1), :]
                for r, t in enumerate(toks)]
        for t, row in zip(toks, rows):
            o_ref[0, pl.ds(t, 1), :] = row
        return carry

    lax.fori_loop(0, cap // COMB_ROWS, body, 0)


def _combine(y, idx, aff, row0, t, cap, s_total, out_prev=None):
    n_e, m, d = y.shape
    bsz = idx.shape[0]
    in_specs = [_smem_row(cap), _smem_row(t), pl.BlockSpec((1, cap, d), lambda b, e: (e, b, 0))]
    args = [idx[:, :, None, :], aff[:, :, None, :], y]
    aliases = {}
    if out_prev is not None:
        in_specs.append(pl.BlockSpec(memory_space=pl.ANY))
        args.append(out_prev)
        aliases = {3: 0}
    return pl.pallas_call(
        functools.partial(_combine_kernel, cap=cap),
        grid=(bsz, n_e),
        in_specs=in_specs,
        out_specs=pl.BlockSpec((1, t, d), lambda b, e: (b, row0 // t, 0)),
        out_shape=jax.ShapeDtypeStruct((bsz, s_total, d), F32),
        input_output_aliases=aliases,
        compiler_params=_cparams(("parallel", "arbitrary"), VMEM_LIMIT),
        name="expert_combine",
    )(*args)


def _moe(u_packed, logits_t, w_gate, w_up, w_down, layer, with_ctx):
    bsz, rows, _ = u_packed.shape
    out = None
    sets = [(0, SEQ)] + ([(SEQ, CTX_LEN)] if with_ctx else [])
    for row0, t in sets:
        cap = EC_CAPACITY * t // N_EXPERTS
        idx, aff = _route(logits_t, row0, t, cap)
        xe = _gather(u_packed, idx, row0, t, cap)
        y = _expert_ffn(xe, w_gate, w_up, w_down, layer)
        out = _combine(y, idx, aff, row0, t, cap, rows, out)
    return out


def _final_kernel(h_ref, f_ref, mod_ref, g_ref, o_ref):
    o_ref[0] = h_ref[0] + mod_ref[0, 0][5:6] * _rms(f_ref[0], g_ref[...])


def _final_residual(h, f, mod_l, g_post):
    bsz, n, d = h.shape
    tm = ROW_TILE
    row = lambda b, i: (b, i, 0)
    return pl.pallas_call(
        _final_kernel,
        grid=(bsz, n // tm),
        in_specs=[pl.BlockSpec((1, tm, d), row), pl.BlockSpec((1, tm, d), row),
                  pl.BlockSpec((1, 1, 6, d), lambda b, i: (b, 0, 0, 0)), pl.BlockSpec((1, d), lambda b, i: (0, 0))],
        out_specs=pl.BlockSpec((1, tm, d), row),
        out_shape=jax.ShapeDtypeStruct((bsz, n, d), F32),
        compiler_params=_cparams(("parallel", "arbitrary")),
        name="ffn_residual",
    )(h, f, mod_l.reshape(mod_l.shape[0], 1, 6, d), g_post.reshape(1, d))


def _rope_tables():
    n = jnp.arange(SEQ, dtype=jnp.int32)
    row = (n // GRID_W).astype(F32)
    col = (n % GRID_W).astype(F32)
    quarter = HEAD_DIM // 4
    inv = ROPE_THETA ** (-jnp.arange(quarter, dtype=F32) / quarter)
    lane = jnp.arange(LANE)
    pos = jnp.where((lane % HEAD_DIM < HEAD_DIM // 2)[None, :], row[:, None], col[:, None])
    ang = pos * inv[lane % quarter][None, :]
    first = (lane % (2 * quarter) < quarter)[None, :]
    cos = jnp.cos(ang)
    sin = jnp.sin(ang)
    sa = jnp.where(first, -sin, 0.0)
    sb = jnp.where(first, 0.0, sin)
    padc = lambda t, v: jnp.concatenate([t, jnp.full((CTX_LEN, LANE), v, F32)], axis=0)
    return padc(cos, 1.0), padc(sa, 0.0), padc(sb, 0.0)


def kernel(x, c, ctx, c_ctx, w_ada, b_ada, g_mix_pre, g_mix_post, g_ffn_pre, g_ffn_post, w_in, conv_w, conv_b,
           dt_bias, a_log, d_skip, g_ssd, attn_sink, w_out, w_router, w_gate, w_up, w_down):
    depth = w_ada.shape[0]
    bsz = x.shape[0]
    d = D_MODEL
    cc = jnp.concatenate([c, c_ctx[None, :], jnp.zeros((7, d), F32)], axis=0)
    mod = _modulation(cc, w_ada, b_ada).reshape(depth, bsz + 8, 6, d)
    rope_tabs = _rope_tables()
    chan_tab = _channel_tables()
    cn_lat, sn_lat = _dft_tables(SEQ)
    cn_ctx, sn_ctx = _dft_tables(CTX_LEN)
    q0 = F_WIDTH

    h = (x, ctx)
    f_prev = None
    for l in range(depth):
        last = l == depth - 1
        w_q = _to_slab_order(w_in[l][:, q0:q0 + ATT_WIDTH], 1) * HEAD_DIM ** -0.5
        w_in_p = jnp.concatenate([w_in[l][:, :q0], w_q, w_in[l][:, q0 + ATT_WIDTH:],
                                  jnp.zeros((d, IN_PAD - IN_WIDTH), F32)], axis=1).astype(BF16)
        w_out_p = jnp.concatenate([w_out[l][:q0], _to_slab_order(w_out[l][q0:q0 + ATT_WIDTH], 0),
                                   w_out[l][q0 + ATT_WIDTH:]], axis=0).astype(BF16)
        outs = _in_proj(h, mod[l], mod[l - 1] if l else None, g_mix_pre[l], w_in_p, rope_tabs, chan_tab,
                        not last, f_prev, g_ffn_post[l - 1] if l else None)
        if f_prev is not None:
            h, outs = outs[0], outs[1:]
        lat_ab, ctx_ab, q, k, v, z, xbc, dt_raw = outs
        fmix = _fourier(lat_ab, cn_lat, sn_lat, 0, out_rows=SEQ if last else S_ALL)
        if not last:
            fmix = _fourier(ctx_ab, cn_ctx, sn_ctx, SEQ, fmix)
        att = _attention(q, k, v, attn_sink[l], not last)
        y_f, y_b = _ssd_mixer(xbc, dt_raw, conv_w[l], conv_b[l], dt_bias[l], a_log[l], d_skip[l])
        n_rows = SEQ if last else S_ALL
        h, u_packed, logits_t = _out_proj(fmix, att, y_f, y_b, z, h, mod[l], g_ssd[l],
                                          g_mix_post[l], g_ffn_pre[l], w_out_p, w_router[l], n_rows)
        f_prev = _moe(u_packed, logits_t, w_gate, w_up, w_down, l, not last)
    return _final_residual(h, f_prev, mod[depth - 1], g_ffn_post[depth - 1])
```
